```python
import math
import jax
import jax.numpy as jnp
from jax import lax
import numpy as np


D_MODEL = 1024
BATCH = 8
SEQ = 2048
DEPTH = 2
DEC_BATCH = 128
DEC_SEQ = 8
PAST_LEN = 16384
PAGE_SIZE = 128

N_MIXERS = 2
SSM_WIDTH = D_MODEL // 2
SSM_GROUP = 16
SSM_GROUPS = SSM_WIDTH // SSM_GROUP
SSM_STATE = 64
SGU_WIDTH = D_MODEL
SGU_HEADS = 8
SGU_HEAD_DIM = SGU_WIDTH // SGU_HEADS
CHUNK = 128
N_EXPERT_GROUPS = 4
EXPERTS_PER_GROUP = 4
N_EXPERTS = N_EXPERT_GROUPS * EXPERTS_PER_GROUP
TOP_K = 2
D_EXPERT = D_MODEL // 4
EPS = 1e-6

kernel_name = "hybrid_s5_sgu_hmoe_step"


def rmsnorm(x, g):
    xf = x.astype(jnp.float32)
    y = xf * lax.rsqrt(jnp.mean(xf * xf, axis=-1, keepdims=True) + EPS)
    return (y * g.astype(jnp.float32)).astype(x.dtype)


def layernorm(x, g, b):
    xf = x.astype(jnp.float32)
    xc = xf - jnp.mean(xf, axis=-1, keepdims=True)
    y = xc * lax.rsqrt(jnp.mean(xc * xc, axis=-1, keepdims=True) + EPS)
    return (y * g.astype(jnp.float32) + b.astype(jnp.float32)).astype(x.dtype)


def _linear_recurrence_combine(e1, e2):
    a1r, a1i, b1r, b1i = e1
    a2r, a2i, b2r, b2i = e2
    return (a2r * a1r - a2i * a1i,
            a2r * a1i + a2i * a1r,
            a2r * b1r - a2i * b1i + b2r,
            a2r * b1i + a2i * b1r + b2i)


def ssm_mixer(x, s0_re, s0_im, w_in, lam_re, lam_im, log_step, b_re, b_im, c_re, c_im, d, w_out):
    f32 = jnp.float32
    nb, ns, _ = x.shape
    u = (x @ w_in).astype(f32).reshape(nb, ns, SSM_GROUPS, SSM_GROUP)
    lr = lam_re.astype(f32)
    li = lam_im.astype(f32)
    dt = jnp.exp(log_step.astype(f32))[:, None]
    mag = jnp.exp(lr * dt)
    ang = li * dt
    lb_re = mag * jnp.cos(ang)
    lb_im = mag * jnp.sin(ang)
    den = lr * lr + li * li
    nr = lb_re - 1.0
    coef_re = (nr * lr + lb_im * li) / den
    coef_im = (lb_im * lr - nr * li) / den
    br = b_re.astype(f32)
    bi = b_im.astype(f32)
    bb_re = coef_re[..., None] * br - coef_im[..., None] * bi
    bb_im = coef_re[..., None] * bi + coef_im[..., None] * br
    bu_re = jnp.einsum('bsgc,gpc->bsgp', u, bb_re)
    bu_im = jnp.einsum('bsgc,gpc->bsgp', u, bb_im)
    s0r = s0_re.astype(f32)
    s0i = s0_im.astype(f32)
    bu_re = bu_re.at[:, 0].add(lb_re * s0r - lb_im * s0i)
    bu_im = bu_im.at[:, 0].add(lb_re * s0i + lb_im * s0r)
    a_re = jnp.broadcast_to(lb_re, bu_re.shape)
    a_im = jnp.broadcast_to(lb_im, bu_im.shape)
    _, _, st_re, st_im = lax.associative_scan(
        _linear_recurrence_combine, (a_re, a_im, bu_re, bu_im), axis=1)
    y = (jnp.einsum('gcp,bsgp->bsgc', c_re.astype(f32), st_re)
         - jnp.einsum('gcp,bsgp->bsgc', c_im.astype(f32), st_im)
         + d.astype(f32) * u)
    y = jax.nn.gelu(y.reshape(nb, ns, SSM_WIDTH)).astype(x.dtype)
    a = y @ w_out
    out = a[..., :D_MODEL] * jax.nn.sigmoid(a[..., D_MODEL:])
    return out, st_re[:, -1], st_im[:, -1]


def sgu_mixer(x, w_in, norm_g, norm_b, w_s, b_s, w_out, chunk_len):
    nb, ns, _ = x.shape
    h = jax.nn.gelu(x @ w_in)
    u, v = jnp.split(h, 2, axis=-1)
    v = layernorm(v, norm_g, norm_b)
    L = chunk_len
    vc = v.reshape(nb, ns // L, L, SGU_HEADS, SGU_HEAD_DIM)
    mask = jnp.tril(jnp.ones((L, L), dtype=bool))
    wm = jnp.where(mask, w_s[:, :L, :L], jnp.zeros((), w_s.dtype))
    z = jnp.einsum('hts,bnshd->bnthd', wm, vc) + b_s[:, :L].T[None, None, :, :, None]
    z = z.reshape(nb, ns, SGU_WIDTH)
    return (u * z) @ w_out, v


def hier_moe(x, w_rg, w_re, w_gate, w_up, w_down):
    nb, ns, dm = x.shape
    xt = x.reshape(nb * ns, dm)
    nt = xt.shape[0]
    rows = jnp.arange(nt)
    pg = jax.nn.softmax((xt @ w_rg).astype(jnp.float32), axis=-1)
    g_idx = jnp.argmax(pg, axis=-1)
    g_w = pg[rows, g_idx]
    le = (xt @ w_re).astype(jnp.float32).reshape(nt, N_EXPERT_GROUPS, EXPERTS_PER_GROUP)
    pe = jax.nn.softmax(le[rows, g_idx], axis=-1)
    top_w, top_i = lax.top_k(pe, TOP_K)
    top_w = top_w / jnp.sum(top_w, axis=-1, keepdims=True)
    eid = g_idx[:, None] * EXPERTS_PER_GROUP + top_i
    combine = jnp.sum(jax.nn.one_hot(eid, N_EXPERTS, dtype=jnp.float32)
                      * (g_w[:, None] * top_w)[..., None], axis=1)
    h = jax.nn.silu(jnp.einsum('td,edf->tef', xt, w_gate)) * jnp.einsum('td,edf->tef', xt, w_up)
    h = h * combine.astype(h.dtype)[..., None]
    return jnp.einsum('tef,efd->td', h, w_down).reshape(nb, ns, dm)


def _normal(k, shape, scale):
    return scale * jax.random.normal(k, shape, jnp.float32)


def setup_inputs(seed: int = 0) -> dict:
    key = jax.random.key(seed)
    ks = jax.random.split(key, 32)
    f32 = jnp.float32
    n = jnp.arange(SSM_STATE, dtype=f32)
    return {
        "x_prompt": _normal(ks[0], (BATCH, SEQ, D_MODEL), 1.0),
        "x_sample": _normal(ks[1], (DEC_BATCH, DEC_SEQ, D_MODEL), 1.0),
        "state_ssm_re": _normal(ks[2], (DEC_BATCH, SSM_GROUPS, SSM_STATE), 0.1),
        "state_ssm_im": _normal(ks[3], (DEC_BATCH, SSM_GROUPS, SSM_STATE), 0.1),
        "norm_mix": 1.0 + _normal(ks[4], (DEPTH, D_MODEL), 0.05),
        "norm_ffn": 1.0 + _normal(ks[5], (DEPTH, D_MODEL), 0.05),
        "norm_final": 1.0 + _normal(ks[6], (D_MODEL,), 0.05),
        "ssm_w_in": _normal(ks[7], (D_MODEL, SSM_WIDTH), D_MODEL ** -0.5),
        "ssm_lambda_re": -0.5 + _normal(ks[8], (SSM_GROUPS, SSM_STATE), 0.01),
        "ssm_lambda_im": math.pi * n[None, :] + _normal(ks[9], (SSM_GROUPS, SSM_STATE), 0.01),
        "ssm_log_step": jax.random.uniform(ks[10], (SSM_GROUPS,), f32, math.log(1e-3), math.log(1e-1)),
        "ssm_b_re": _normal(ks[11], (SSM_GROUPS, SSM_STATE, SSM_GROUP), (2 * SSM_GROUP) ** -0.5),
        "ssm_b_im": _normal(ks[12], (SSM_GROUPS, SSM_STATE, SSM_GROUP), (2 * SSM_GROUP) ** -0.5),
        "ssm_c_re": _normal(ks[13], (SSM_GROUPS, SSM_GROUP, SSM_STATE), (2 * SSM_STATE) ** -0.5),
        "ssm_c_im": _normal(ks[14], (SSM_GROUPS, SSM_GROUP, SSM_STATE), (2 * SSM_STATE) ** -0.5),
        "ssm_d": 1.0 + _normal(ks[15], (SSM_GROUPS, SSM_GROUP), 0.1),
        "ssm_w_out": _normal(ks[16], (SSM_WIDTH, 2 * D_MODEL), SSM_WIDTH ** -0.5),
        "sgu_w_in": _normal(ks[17], (D_MODEL, 2 * SGU_WIDTH), D_MODEL ** -0.5),
        "sgu_norm_g": 1.0 + _normal(ks[18], (SGU_WIDTH,), 0.05),
        "sgu_norm_b": _normal(ks[19], (SGU_WIDTH,), 0.02),
        "sgu_w_s": _normal(ks[20], (SGU_HEADS, CHUNK, CHUNK), CHUNK ** -0.5),
        "sgu_b_s": 1.0 + _normal(ks[21], (SGU_HEADS, CHUNK), 0.1),
        "sgu_w_out": _normal(ks[22], (SGU_WIDTH, D_MODEL), SGU_WIDTH ** -0.5),
        "moe_router_group": _normal(ks[23], (DEPTH, D_MODEL, N_EXPERT_GROUPS), D_MODEL ** -0.5),
        "moe_router_expert": _normal(ks[24], (DEPTH, D_MODEL, N_EXPERTS), D_MODEL ** -0.5),
        "moe_w_gate": _normal(ks[25], (DEPTH, N_EXPERTS, D_MODEL, D_EXPERT), D_MODEL ** -0.5),
        "moe_w_up": _normal(ks[26], (DEPTH, N_EXPERTS, D_MODEL, D_EXPERT), D_MODEL ** -0.5),
        "moe_w_down": _normal(ks[27], (DEPTH, N_EXPERTS, D_EXPERT, D_MODEL), D_EXPERT ** -0.5),
    }


def reference(x_prompt, x_sample, state_ssm_re, state_ssm_im, norm_mix, norm_ffn, norm_final,
              ssm_w_in, ssm_lambda_re, ssm_lambda_im, ssm_log_step, ssm_b_re, ssm_b_im,
              ssm_c_re, ssm_c_im, ssm_d, ssm_w_out,
              sgu_w_in, sgu_norm_g, sgu_norm_b, sgu_w_s, sgu_b_s, sgu_w_out,
              moe_router_group, moe_router_expert, moe_w_gate, moe_w_up, moe_w_down):
    xp = x_prompt
    xs = x_sample
    ssm_re_prompt = ssm_im_prompt = ssm_re_sample = ssm_im_sample = sgu_v_sample = None
    for i in range(DEPTH):
        hp = rmsnorm(xp, norm_mix[i])
        hs = rmsnorm(xs, norm_mix[i])
        if i % N_MIXERS == 0:
            zero_state = jnp.zeros((xp.shape[0], SSM_GROUPS, SSM_STATE), jnp.float32)
            mp, ssm_re_prompt, ssm_im_prompt = ssm_mixer(
                hp, zero_state, zero_state, ssm_w_in, ssm_lambda_re, ssm_lambda_im, ssm_log_step,
                ssm_b_re, ssm_b_im, ssm_c_re, ssm_c_im, ssm_d, ssm_w_out)
            ms, ssm_re_sample, ssm_im_sample = ssm_mixer(
                hs, state_ssm_re, state_ssm_im, ssm_w_in, ssm_lambda_re, ssm_lambda_im, ssm_log_step,
                ssm_b_re, ssm_b_im, ssm_c_re, ssm_c_im, ssm_d, ssm_w_out)
        else:
            mp, _ = sgu_mixer(hp, sgu_w_in, sgu_norm_g, sgu_norm_b, sgu_w_s, sgu_b_s, sgu_w_out, CHUNK)
            ms, sgu_v_sample = sgu_mixer(hs, sgu_w_in, sgu_norm_g, sgu_norm_b, sgu_w_s, sgu_b_s,
                                         sgu_w_out, DEC_SEQ)
        xp = xp + mp
        xs = xs + ms
        xp = xp + hier_moe(rmsnorm(xp, norm_ffn[i]), moe_router_group[i], moe_router_expert[i],
                           moe_w_gate[i], moe_w_up[i], moe_w_down[i])
        xs = xs + hier_moe(rmsnorm(xs, norm_ffn[i]), moe_router_group[i], moe_router_expert[i],
                           moe_w_gate[i], moe_w_up[i], moe_w_down[i])
    y_prompt = rmsnorm(xp, norm_final)
    y_sample = rmsnorm(xs, norm_final)
    return (y_prompt, y_sample, ssm_re_prompt, ssm_im_prompt, ssm_re_sample, ssm_im_sample, sgu_v_sample)
```

```python
import functools
import math

import jax
import jax.numpy as jnp
from jax import lax
from jax.experimental import pallas as pl
from jax.experimental.pallas import tpu as pltpu

D_MODEL = 1024
SSM_WIDTH = D_MODEL // 2
SSM_GROUP = 16
SSM_GROUPS = SSM_WIDTH // SSM_GROUP
SSM_STATE = 64
SGU_WIDTH = D_MODEL
SGU_HEADS = 8
SGU_HEAD_DIM = SGU_WIDTH // SGU_HEADS
CHUNK = 128
N_EXPERT_GROUPS = 4
EXPERTS_PER_GROUP = 4
N_EXPERTS = N_EXPERT_GROUPS * EXPERTS_PER_GROUP
D_EXPERT = D_MODEL // 4
EPS = 1e-6

SUBLANES = 8
LANES = 128
VMEM_LIMIT_BYTES = 56 * 1024 * 1024

SLAB_GROUPS = 16
N_SLABS = SSM_GROUPS // SLAB_GROUPS
SLAB_HALF = SLAB_GROUPS * SSM_STATE
SLAB = 2 * SLAB_HALF
STATE_COLS = N_SLABS * SLAB
SLAB_IN = SLAB_GROUPS * SSM_GROUP
REC_COLS = 512

ROUTER_LANES = LANES
BF16 = jnp.bfloat16
F32 = jnp.float32


def _rms(x, g):
    return x * lax.rsqrt(jnp.mean(x * x, axis=-1, keepdims=True) + EPS) * g


def _s5_prep_kernel(lr_ref, li_ref, ls_ref, br_ref, bi_ref, lbr_ref, lbi_ref, bbr_ref, bbi_ref):
    lr = lr_ref[...]
    li = li_ref[...]
    dt = jnp.exp(ls_ref[...])
    mag = jnp.exp(lr * dt)
    ang = li * dt
    lb_re = mag * jnp.cos(ang)
    lb_im = mag * jnp.sin(ang)
    den = lr * lr + li * li
    nr = lb_re - 1.0
    coef_re = (nr * lr + lb_im * li) / den
    coef_im = (lb_im * lr - nr * li) / den
    br = br_ref[...]
    bi = bi_ref[...]
    lbr_ref[...] = lb_re
    lbi_ref[...] = lb_im
    bbr_ref[...] = coef_re * br - coef_im * bi
    bbi_ref[...] = coef_re * bi + coef_im * br


def _s5_prep(lam_re, lam_im, log_step, b_re, b_im):
    rows = SSM_GROUPS * SSM_GROUP
    rep = lambda a: jnp.repeat(a.astype(F32), SSM_GROUP, axis=0)
    lr = rep(lam_re)
    li = rep(lam_im)
    ls = rep(jnp.broadcast_to(log_step.astype(F32)[:, None], (SSM_GROUPS, SSM_STATE)))
    br = jnp.transpose(b_re.astype(F32), (0, 2, 1)).reshape(rows, SSM_STATE)
    bi = jnp.transpose(b_im.astype(F32), (0, 2, 1)).reshape(rows, SSM_STATE)
    sds = jax.ShapeDtypeStruct((rows, SSM_STATE), F32)
    lb_re, lb_im, bb_re, bb_im = pl.pallas_call(
        _s5_prep_kernel, out_shape=(sds, sds, sds, sds), name="s5_prep")(lr, li, ls, br, bi)
    return lb_re, lb_im, bb_re, bb_im


def _to_state_cols(a):
    return a.reshape(a.shape[:-2] + (N_SLABS, SLAB_HALF))


def _pack_state(re, im):
    return jnp.concatenate([_to_state_cols(re), _to_state_cols(im)], axis=-1).reshape(re.shape[0], STATE_COLS)


def _unpack_state(s):
    s = s.reshape(s.shape[0], N_SLABS, 2, SLAB_GROUPS, SSM_STATE)
    re = s[:, :, 0].reshape(s.shape[0], SSM_GROUPS, SSM_STATE)
    im = s[:, :, 1].reshape(s.shape[0], SSM_GROUPS, SSM_STATE)
    return re, im


def _s5_matrices(lb_re, lb_im, bb_re, bb_im, c_re, c_im, d):
    eye = jnp.eye(SLAB_GROUPS, dtype=F32)
    def in_side(bb):
        bb = bb.reshape(N_SLABS, SLAB_GROUPS, SSM_GROUP, SSM_STATE)
        m = bb[:, :, :, None, :] * eye[None, :, None, :, None]
        return m.reshape(N_SLABS, SLAB_IN, SLAB_HALF)
    bmat = jnp.concatenate([in_side(bb_re), in_side(bb_im)], axis=-1).astype(BF16)
    def out_side(c):
        c = jnp.transpose(c.astype(F32), (0, 2, 1)).reshape(N_SLABS, SLAB_GROUPS, SSM_STATE, SSM_GROUP)
        m = c[:, :, :, None, :] * eye[None, :, None, :, None]
        return m.reshape(N_SLABS, SLAB_HALF, SLAB_IN)
    cmat = jnp.concatenate([out_side(c_re), -out_side(c_im)], axis=1).astype(BF16)
    lam = lambda lb: _to_state_cols(lb.reshape(SSM_GROUPS, SSM_GROUP, SSM_STATE)[:, 0])
    a_re = jnp.concatenate([lam(lb_re), lam(lb_re)], axis=-1).reshape(1, STATE_COLS)
    a_im = jnp.concatenate([lam(lb_im), lam(lb_im)], axis=-1).reshape(1, STATE_COLS)
    return bmat, cmat, a_re, a_im, d.astype(F32).reshape(1, SSM_WIDTH)


def _s5_kernel(x_ref, g_ref, win_ref, bm_ref, are_ref, aim_ref, cm_ref, d_ref, wout_ref, s0_ref,
               xo_ref, so_ref, bu_ref, st_ref, *, n_seq_groups, steps, carry):
    x = x_ref[...]
    h = _rms(x, g_ref[...]).astype(BF16)
    u = jnp.dot(h, win_ref[...], preferred_element_type=F32)
    ub = u.astype(BF16)
    for j in range(N_SLABS):
        bu_ref[:, j * SLAB:(j + 1) * SLAB] = jnp.dot(
            ub[:, j * SLAB_IN:(j + 1) * SLAB_IN], bm_ref[j], preferred_element_type=F32)

    if carry:
        @pl.when(pl.program_id(0) == 0)
        def _():
            st_ref[...] = s0_ref[0]

    for sg in range(n_seq_groups):
        for j in range(N_SLABS):
            for k in range(SLAB_HALF // REC_COLS):
                re0 = j * SLAB + k * REC_COLS
                im0 = re0 + SLAB_HALF
                ar = jnp.broadcast_to(are_ref[:, re0:re0 + REC_COLS], (SUBLANES, REC_COLS))
                ai = jnp.broadcast_to(aim_ref[:, re0:re0 + REC_COLS], (SUBLANES, REC_COLS))
                if carry:
                    sr = st_ref[:, re0:re0 + REC_COLS]
                    si = st_ref[:, im0:im0 + REC_COLS]
                else:
                    sr = s0_ref[sg, :, re0:re0 + REC_COLS]
                    si = s0_ref[sg, :, im0:im0 + REC_COLS]

                def step(t, c, re0=re0, im0=im0, ar=ar, ai=ai, sg=sg):
                    sr, si = c
                    row = pl.multiple_of((sg * steps + t) * SUBLANES, SUBLANES)
                    br = bu_ref[pl.ds(row, SUBLANES), re0:re0 + REC_COLS]
                    bi = bu_ref[pl.ds(row, SUBLANES), im0:im0 + REC_COLS]
                    nr = ar * sr - ai * si + br
                    ni = ar * si + ai * sr + bi
                    bu_ref[pl.ds(row, SUBLANES), re0:re0 + REC_COLS] = nr
                    bu_ref[pl.ds(row, SUBLANES), im0:im0 + REC_COLS] = ni
                    return nr, ni

                sr, si = lax.fori_loop(0, steps, step, (sr, si), unroll=min(steps, 8))
                if carry:
                    st_ref[:, re0:re0 + REC_COLS] = sr
                    st_ref[:, im0:im0 + REC_COLS] = si
                    so_ref[0, :, re0:re0 + REC_COLS] = sr
                    so_ref[0, :, im0:im0 + REC_COLS] = si
                else:
                    so_ref[sg, :, re0:re0 + REC_COLS] = sr
                    so_ref[sg, :, im0:im0 + REC_COLS] = si

    ys = [jnp.dot(bu_ref[:, j * SLAB:(j + 1) * SLAB].astype(BF16), cm_ref[j], preferred_element_type=F32)
          for j in range(N_SLABS)]
    y = jnp.concatenate(ys, axis=-1) + d_ref[...] * u
    y = jax.nn.gelu(y).astype(BF16)
    a = jnp.dot(y, wout_ref[...], preferred_element_type=F32)
    xo_ref[...] = x + a[:, :D_MODEL] * jax.nn.sigmoid(a[:, D_MODEL:])


def _s5_layer(x_rows, s0, g, win, bmat, a_re, a_im, cmat, dvec, wout, *, n_seq_groups, steps, carry):
    rows = n_seq_groups * steps * SUBLANES
    n_blocks = x_rows.shape[0] // rows
    const = lambda *shape: pl.BlockSpec(shape, lambda i: (0,) * len(shape))
    s_idx = (lambda i: (0, 0, 0)) if carry else (lambda i: (i, 0, 0))
    kern = functools.partial(_s5_kernel, n_seq_groups=n_seq_groups, steps=steps, carry=carry)
    n_state_groups = 1 if carry else n_blocks * n_seq_groups
    return pl.pallas_call(
        kern,
        grid=(n_blocks,),
        in_specs=[
            pl.BlockSpec((rows, D_MODEL), lambda i: (i, 0)),
            const(1, D_MODEL), const(D_MODEL, SSM_WIDTH), const(N_SLABS, SLAB_IN, SLAB),
            const(1, STATE_COLS), const(1, STATE_COLS), const(N_SLABS, SLAB, SLAB_IN),
            const(1, SSM_WIDTH), const(SSM_WIDTH, 2 * D_MODEL),
            pl.BlockSpec((n_seq_groups, SUBLANES, STATE_COLS), s_idx),
        ],
        out_specs=[
            pl.BlockSpec((rows, D_MODEL), lambda i: (i, 0)),
            pl.BlockSpec((n_seq_groups, SUBLANES, STATE_COLS), s_idx),
        ],
        out_shape=[
            jax.ShapeDtypeStruct(x_rows.shape, F32),
            jax.ShapeDtypeStruct((n_state_groups, SUBLANES, STATE_COLS), F32),
        ],
        scratch_shapes=[pltpu.VMEM((rows, STATE_COLS), F32), pltpu.VMEM((SUBLANES, STATE_COLS), F32)],
        compiler_params=pltpu.CompilerParams(
            dimension_semantics=("arbitrary",), vmem_limit_bytes=VMEM_LIMIT_BYTES),
        name="s5_carry" if carry else "s5_step",
    )(x_rows, g, win, bmat, a_re, a_im, cmat, dvec, wout, s0)


def _sgu_kernel(x_ref, g_ref, win_ref, ng_ref, nb_ref, ws_ref, bs_ref, wout_ref, xo_ref, *maybe_v_ref,
                block_diag):
    x = x_ref[...]
    rows = x.shape[0]
    h = _rms(x, g_ref[...]).astype(BF16)
    hh = jax.nn.gelu(jnp.dot(h, win_ref[...], preferred_element_type=F32))
    u = hh[:, :SGU_WIDTH]
    v = hh[:, SGU_WIDTH:]
    vc = v - jnp.mean(v, axis=-1, keepdims=True)
    v = vc * lax.rsqrt(jnp.mean(vc * vc, axis=-1, keepdims=True) + EPS) * ng_ref[...] + nb_ref[...]
    if maybe_v_ref:
        maybe_v_ref[0][...] = v
    r = lax.broadcasted_iota(jnp.int32, (CHUNK, CHUNK), 0)
    c = lax.broadcasted_iota(jnp.int32, (CHUNK, CHUNK), 1)
    if block_diag:
        keep = (r // SUBLANES == c // SUBLANES) & (r >= c)
    else:
        keep = r >= c
    vb = v.astype(BF16)
    n_chunks = rows // CHUNK
    zs = []
    for hd in range(SGU_HEADS):
        wm = jnp.where(keep, ws_ref[hd], 0.0).astype(BF16)
        cols = slice(hd * SGU_HEAD_DIM, (hd + 1) * SGU_HEAD_DIM)
        vcat = jnp.concatenate([vb[ck * CHUNK:(ck + 1) * CHUNK, cols] for ck in range(n_chunks)], axis=-1)
        zcat = jnp.dot(wm, vcat, preferred_element_type=F32)
        zs.append([zcat[:, ck * SGU_HEAD_DIM:(ck + 1) * SGU_HEAD_DIM] for ck in range(n_chunks)])
    z = jnp.concatenate(
        [jnp.concatenate([zs[hd][ck] for hd in range(SGU_HEADS)], axis=-1) + bs_ref[...]
         for ck in range(n_chunks)], axis=0)
    o = jnp.dot((u * z).astype(BF16), wout_ref[...], preferred_element_type=F32)
    xo_ref[...] = x + o


def _sgu_layer(x_rows, g, win, ng, nb, ws, bs, wout, *, rows, block_diag, emit_v):
    n_blocks = x_rows.shape[0] // rows
    const = lambda *shape: pl.BlockSpec(shape, lambda i: (0,) * len(shape))
    row_spec = pl.BlockSpec((rows, D_MODEL), lambda i: (i, 0))
    out_shape = [jax.ShapeDtypeStruct(x_rows.shape, F32)]
    out_specs = [row_spec]
    if emit_v:
        out_shape.append(jax.ShapeDtypeStruct(x_rows.shape, F32))
        out_specs.append(row_spec)
    return pl.pallas_call(
        functools.partial(_sgu_kernel, block_diag=block_diag),
        grid=(n_blocks,),
        in_specs=[row_spec, const(1, D_MODEL), const(D_MODEL, 2 * SGU_WIDTH), const(1, SGU_WIDTH),
                  const(1, SGU_WIDTH), const(SGU_HEADS, CHUNK, CHUNK), const(CHUNK, SGU_WIDTH),
                  const(SGU_WIDTH, D_MODEL)],
        out_specs=out_specs,
        out_shape=out_shape,
        compiler_params=pltpu.CompilerParams(
            dimension_semantics=("arbitrary",), vmem_limit_bytes=VMEM_LIMIT_BYTES),
        name="sgu_block_diag" if block_diag else "sgu_chunked",
    )(x_rows, g, win, ng, nb, ws, bs, wout)


def _route(logits):
    lane = lax.broadcasted_iota(jnp.int32, logits.shape, 1)
    neg = jnp.float32(-jnp.inf)
    big = jnp.int32(ROUTER_LANES)
    is_g = lane < N_EXPERT_GROUPS
    lg = jnp.where(is_g, logits, neg)
    mg = jnp.max(lg, axis=-1, keepdims=True)
    g_idx = jnp.min(jnp.where(lg == mg, lane, big), axis=-1, keepdims=True)
    g_w = 1.0 / jnp.sum(jnp.where(is_g, jnp.exp(lg - mg), 0.0), axis=-1, keepdims=True)
    first = N_EXPERT_GROUPS + g_idx * EXPERTS_PER_GROUP
    in_grp = (lane >= first) & (lane < first + EXPERTS_PER_GROUP)
    le = jnp.where(in_grp, logits, neg)
    m1 = jnp.max(le, axis=-1, keepdims=True)
    i1 = jnp.min(jnp.where(le == m1, lane, big), axis=-1, keepdims=True)
    le2 = jnp.where(lane == i1, neg, le)
    m2 = jnp.max(le2, axis=-1, keepdims=True)
    i2 = jnp.min(jnp.where(le2 == m2, lane, big), axis=-1, keepdims=True)
    e2 = jnp.exp(m2 - m1)
    w1 = 1.0 / (1.0 + e2)
    w2 = e2 / (1.0 + e2)
    return jnp.where(lane == i1, g_w * w1, jnp.where(lane == i2, g_w * w2, 0.0))


def _moe_dense_kernel(x_ref, g_ref, wr_ref, wg_ref, wu_ref, wd_ref, gf_ref, o_ref, xn_ref, cmb_ref, acc_ref,
                      *, final_norm):
    e = pl.program_id(1)

    @pl.when(e == 0)
    def _():
        xn = _rms(x_ref[...], g_ref[...])
        xn_ref[...] = xn.astype(BF16)
        logits = jnp.dot(xn, wr_ref[...], preferred_element_type=F32, precision=lax.Precision.HIGHEST)
        cmb_ref[...] = _route(logits)
        acc_ref[...] = jnp.zeros_like(acc_ref)

    xn = xn_ref[...]
    lane = lax.broadcasted_iota(jnp.int32, cmb_ref.shape, 1)
    c_e = jnp.sum(jnp.where(lane == N_EXPERT_GROUPS + e, cmb_ref[...], 0.0), axis=-1, keepdims=True)
    gate = jnp.dot(xn, wg_ref[0].astype(BF16), preferred_element_type=F32)
    up = jnp.dot(xn, wu_ref[0].astype(BF16), preferred_element_type=F32)
    hcur = (jax.nn.silu(gate) * up * c_e).astype(BF16)
    acc_ref[...] += jnp.dot(hcur, wd_ref[0].astype(BF16), preferred_element_type=F32)

    @pl.when(e == N_EXPERTS - 1)
    def _():
        y = x_ref[...] + acc_ref[...]
        if final_norm:
            y = _rms(y, gf_ref[...])
        o_ref[...] = y


def _moe_layer(x_rows, g, w_router, w_gate, w_up, w_down, g_final, *, rows, final_norm):
    n_blocks = x_rows.shape[0] // rows
    const = lambda *shape: pl.BlockSpec(shape, lambda i, e: (0,) * len(shape))
    row_spec = pl.BlockSpec((rows, D_MODEL), lambda i, e: (i, 0))
    return pl.pallas_call(
        functools.partial(_moe_dense_kernel, final_norm=final_norm),
        grid=(n_blocks, N_EXPERTS),
        in_specs=[row_spec, const(1, D_MODEL), const(D_MODEL, ROUTER_LANES),
                  pl.BlockSpec((1, D_MODEL, D_EXPERT), lambda i, e: (e, 0, 0)),
                  pl.BlockSpec((1, D_MODEL, D_EXPERT), lambda i, e: (e, 0, 0)),
                  pl.BlockSpec((1, D_EXPERT, D_MODEL), lambda i, e: (e, 0, 0)),
                  const(1, D_MODEL)],
        out_specs=row_spec,
        out_shape=jax.ShapeDtypeStruct(x_rows.shape, F32),
        scratch_shapes=[pltpu.VMEM((rows, D_MODEL), BF16), pltpu.VMEM((rows, ROUTER_LANES), F32),
                        pltpu.VMEM((rows, D_MODEL), F32)],
        compiler_params=pltpu.CompilerParams(
            dimension_semantics=("arbitrary", "arbitrary"), vmem_limit_bytes=VMEM_LIMIT_BYTES),
        name="moe_dense_final" if final_norm else "moe_dense",
    )(x_rows, g, w_router, w_gate, w_up, w_down, g_final)


def _router_weights(w_rg, w_re):
    pad = ROUTER_LANES - N_EXPERT_GROUPS - N_EXPERTS
    return jnp.concatenate([w_rg.astype(F32), w_re.astype(F32), jnp.zeros((D_MODEL, pad), F32)], axis=-1)


def kernel(x_prompt, x_sample, state_ssm_re, state_ssm_im, norm_mix, norm_ffn, norm_final, ssm_w_in, ssm_lambda_re, ssm_lambda_im, ssm_log_step, ssm_b_re, ssm_b_im, ssm_c_re, ssm_c_im, ssm_d, ssm_w_out, sgu_w_in, sgu_norm_g, sgu_norm_b, sgu_w_s, sgu_b_s, sgu_w_out, moe_router_group, moe_router_expert, moe_w_gate, moe_w_up, moe_w_down):
    nb, ns, dm = x_prompt.shape
    db, ds, _ = x_sample.shape
    assert nb == SUBLANES and dm == D_MODEL and db % SUBLANES == 0 and ns % CHUNK == 0
    row = lambda a: a.astype(F32).reshape(1, -1)

    lb_re, lb_im, bb_re, bb_im = _s5_prep(ssm_lambda_re, ssm_lambda_im, ssm_log_step, ssm_b_re, ssm_b_im)
    bmat, cmat, a_re, a_im, dvec = _s5_matrices(lb_re, lb_im, bb_re, bb_im, ssm_c_re, ssm_c_im, ssm_d)
    win = ssm_w_in.astype(BF16)
    wout = ssm_w_out.astype(BF16)
    g0 = row(norm_mix[0])

    xp = jnp.transpose(x_prompt, (1, 0, 2)).reshape(ns * nb, dm)
    n_sg = db // SUBLANES
    xs = jnp.transpose(x_sample.reshape(n_sg, SUBLANES, ds, dm), (0, 2, 1, 3)).reshape(db * ds, dm)
    zero_state = jnp.zeros((1, SUBLANES, STATE_COLS), F32)
    s0 = _pack_state(state_ssm_re.astype(F32), state_ssm_im.astype(F32)).reshape(n_sg, SUBLANES, STATE_COLS)

    s5 = functools.partial(_s5_layer, g=g0, win=win, bmat=bmat, a_re=a_re, a_im=a_im, cmat=cmat,
                           dvec=dvec, wout=wout)
    xp, st_p = s5(xp, zero_state, n_seq_groups=1, steps=64, carry=True)
    xs, st_s = s5(xs, s0, n_seq_groups=8, steps=ds, carry=False)
    ssm_re_p, ssm_im_p = _unpack_state(st_p.reshape(nb, STATE_COLS))
    ssm_re_s, ssm_im_s = _unpack_state(st_s.reshape(db, STATE_COLS))

    moe = lambda x, i, rows, final: _moe_layer(
        x, row(norm_ffn[i]), _router_weights(moe_router_group[i], moe_router_expert[i]),
        moe_w_gate[i], moe_w_up[i], moe_w_down[i], row(norm_final), rows=min(rows, x.shape[0]),
        final_norm=final)
    xp = moe(xp, 0, 1024, False)
    xs = moe(xs, 0, 1024, False)

    xp = jnp.transpose(xp.reshape(ns, nb, dm), (1, 0, 2)).reshape(nb * ns, dm)
    xs = jnp.transpose(xs.reshape(n_sg, ds, SUBLANES, dm), (0, 2, 1, 3)).reshape(db * ds, dm)
    g1 = row(norm_mix[1])
    swin = sgu_w_in.astype(BF16)
    swout = sgu_w_out.astype(BF16)
    ng = row(sgu_norm_g)
    nbias = row(sgu_norm_b)
    ws_p = sgu_w_s.astype(F32)[:, :CHUNK, :CHUNK]
    bs_p = jnp.repeat(sgu_b_s.astype(F32)[:, :CHUNK].T, SGU_HEAD_DIM, axis=1)
    reps = CHUNK // ds
    ws_s = jnp.tile(sgu_w_s.astype(F32)[:, :ds, :ds], (1, reps, reps))
    bs_s = jnp.tile(jnp.repeat(sgu_b_s.astype(F32)[:, :ds].T, SGU_HEAD_DIM, axis=1), (reps, 1))
    xp, = _sgu_layer(xp, g1, swin, ng, nbias, ws_p, bs_p, swout, rows=512, block_diag=False, emit_v=False)
    xs, v_s = _sgu_layer(xs, g1, swin, ng, nbias, ws_s, bs_s, swout, rows=512, block_diag=True, emit_v=True)

    yp = moe(xp, 1, 1024, True)
    ys = moe(xs, 1, 1024, True)
    return (yp.reshape(nb, ns, dm), ys.reshape(db, ds, dm), ssm_re_p, ssm_im_p, ssm_re_s, ssm_im_s,
            v_s.reshape(db, ds, SGU_WIDTH))
```

```python
import functools

import jax
import jax.numpy as jnp
from jax import lax
from jax.experimental import pallas as pl
from jax.experimental.pallas import tpu as pltpu

D_MODEL = 1024
SSM_WIDTH = D_MODEL // 2
SSM_GROUP = 16
SSM_GROUPS = SSM_WIDTH // SSM_GROUP
SSM_STATE = 64
SGU_WIDTH = D_MODEL
SGU_HEADS = 8
SGU_HEAD_DIM = SGU_WIDTH // SGU_HEADS
CHUNK = 128
N_EXPERT_GROUPS = 4
EXPERTS_PER_GROUP = 4
N_EXPERTS = N_EXPERT_GROUPS * EXPERTS_PER_GROUP
D_EXPERT = D_MODEL // 4
EPS = 1e-6

SUBLANES = 8
LANES = 128
VMEM_LIMIT_BYTES = 56 * 1024 * 1024

SLAB_GROUPS = 16
N_SLABS = SSM_GROUPS // SLAB_GROUPS
SLAB_HALF = SLAB_GROUPS * SSM_STATE
SLAB = 2 * SLAB_HALF
STATE_COLS = N_SLABS * SLAB
SLAB_IN = SLAB_GROUPS * SSM_GROUP
REC_COLS = 512

PAIR_LO = (0, 0, 0, 1, 1, 2)
PAIR_HI = (1, 2, 3, 2, 3, 3)
N_PAIRS = len(PAIR_LO)
N_BUCKETS = N_EXPERT_GROUPS * N_PAIRS
MOE_TILE = 256
MIX_ROWS = 512
META_LANES = LANES
ROW_W = D_MODEL + META_LANES
META_BUCKET, META_W_LO, META_W_HI, META_RANK = 0, 1, 2, 3
ROUTER_LANES = LANES
BF16 = jnp.bfloat16
F32 = jnp.float32


def _rms(x, g):
    return x * lax.rsqrt(jnp.mean(x * x, axis=-1, keepdims=True) + EPS) * g


def _s5_prep_kernel(lr_ref, li_ref, ls_ref, br_ref, bi_ref, lbr_ref, lbi_ref, bbr_ref, bbi_ref):
    lr = lr_ref[...]
    li = li_ref[...]
    dt = jnp.exp(ls_ref[...])
    mag = jnp.exp(lr * dt)
    ang = li * dt
    lb_re = mag * jnp.cos(ang)
    lb_im = mag * jnp.sin(ang)
    den = lr * lr + li * li
    nr = lb_re - 1.0
    coef_re = (nr * lr + lb_im * li) / den
    coef_im = (lb_im * lr - nr * li) / den
    br = br_ref[...]
    bi = bi_ref[...]
    lbr_ref[...] = lb_re
    lbi_ref[...] = lb_im
    bbr_ref[...] = coef_re * br - coef_im * bi
    bbi_ref[...] = coef_re * bi + coef_im * br


def _s5_prep(lam_re, lam_im, log_step, b_re, b_im):
    rows = SSM_GROUPS * SSM_GROUP
    rep = lambda a: jnp.repeat(a.astype(F32), SSM_GROUP, axis=0)
    lr = rep(lam_re)
    li = rep(lam_im)
    ls = rep(jnp.broadcast_to(log_step.astype(F32)[:, None], (SSM_GROUPS, SSM_STATE)))
    br = jnp.transpose(b_re.astype(F32), (0, 2, 1)).reshape(rows, SSM_STATE)
    bi = jnp.transpose(b_im.astype(F32), (0, 2, 1)).reshape(rows, SSM_STATE)
    sds = jax.ShapeDtypeStruct((rows, SSM_STATE), F32)
    lb_re, lb_im, bb_re, bb_im = pl.pallas_call(
        _s5_prep_kernel, out_shape=(sds, sds, sds, sds), name="s5_prep")(lr, li, ls, br, bi)
    return lb_re, lb_im, bb_re, bb_im


def _to_state_cols(a):
    return a.reshape(a.shape[:-2] + (N_SLABS, SLAB_HALF))


def _pack_state(re, im):
    return jnp.concatenate([_to_state_cols(re), _to_state_cols(im)], axis=-1).reshape(re.shape[0], STATE_COLS)


def _unpack_state(s):
    s = s.reshape(s.shape[0], N_SLABS, 2, SLAB_GROUPS, SSM_STATE)
    re = s[:, :, 0].reshape(s.shape[0], SSM_GROUPS, SSM_STATE)
    im = s[:, :, 1].reshape(s.shape[0], SSM_GROUPS, SSM_STATE)
    return re, im


def _s5_matrices(lb_re, lb_im, bb_re, bb_im, c_re, c_im, d):
    eye = jnp.eye(SLAB_GROUPS, dtype=F32)
    def in_side(bb):
        bb = bb.reshape(N_SLABS, SLAB_GROUPS, SSM_GROUP, SSM_STATE)
        m = bb[:, :, :, None, :] * eye[None, :, None, :, None]
        return m.reshape(N_SLABS, SLAB_IN, SLAB_HALF)
    bmat = jnp.concatenate([in_side(bb_re), in_side(bb_im)], axis=-1).astype(BF16)
    def out_side(c):
        c = jnp.transpose(c.astype(F32), (0, 2, 1)).reshape(N_SLABS, SLAB_GROUPS, SSM_STATE, SSM_GROUP)
        m = c[:, :, :, None, :] * eye[None, :, None, :, None]
        return m.reshape(N_SLABS, SLAB_HALF, SLAB_IN)
    cmat = jnp.concatenate([out_side(c_re), -out_side(c_im)], axis=1).astype(BF16)
    lam = lambda lb: _to_state_cols(lb.reshape(SSM_GROUPS, SSM_GROUP, SSM_STATE)[:, 0])
    a_re = jnp.concatenate([lam(lb_re), lam(lb_re)], axis=-1).reshape(1, STATE_COLS)
    a_im = jnp.concatenate([lam(lb_im), lam(lb_im)], axis=-1).reshape(1, STATE_COLS)
    return bmat, cmat, a_re, a_im, d.astype(F32).reshape(1, SSM_WIDTH)


def _route_meta(x, g, wr, cnt_ref):
    xn = _rms(x, g)
    logits = jnp.dot(xn, wr, preferred_element_type=F32, precision=lax.Precision.HIGHEST)
    rows = logits.shape[0]
    lane = lax.broadcasted_iota(jnp.int32, logits.shape, 1)
    neg = jnp.float32(-jnp.inf)
    big = jnp.int32(ROUTER_LANES)
    is_g = lane < N_EXPERT_GROUPS
    lg = jnp.where(is_g, logits, neg)
    mg = jnp.max(lg, axis=-1, keepdims=True)
    g_idx = jnp.min(jnp.where(lg == mg, lane, big), axis=-1, keepdims=True)
    g_w = 1.0 / jnp.sum(jnp.where(is_g, jnp.exp(lg - mg), 0.0), axis=-1, keepdims=True)
    first = N_EXPERT_GROUPS + g_idx * EXPERTS_PER_GROUP
    in_grp = (lane >= first) & (lane < first + EXPERTS_PER_GROUP)
    le = jnp.where(in_grp, logits, neg)
    m1 = jnp.max(le, axis=-1, keepdims=True)
    i1 = jnp.min(jnp.where(le == m1, lane, big), axis=-1, keepdims=True)
    le2 = jnp.where(lane == i1, neg, le)
    m2 = jnp.max(le2, axis=-1, keepdims=True)
    i2 = jnp.min(jnp.where(le2 == m2, lane, big), axis=-1, keepdims=True)
    e2 = jnp.exp(m2 - m1)
    w1 = g_w / (1.0 + e2)
    w2 = g_w * e2 / (1.0 + e2)
    lo = jnp.minimum(i1, i2) - first
    hi = jnp.maximum(i1, i2) - first
    pair = jnp.where(lo == 0, 0, jnp.where(lo == 1, 3, 5)) + (hi - lo - 1)
    bucket = g_idx * N_PAIRS + pair
    w_lo = jnp.where(i1 < i2, w1, w2)
    w_hi = jnp.where(i1 < i2, w2, w1)
    onehot = lane == bucket
    oh = jnp.where(onehot, 1.0, 0.0)
    r_i = lax.broadcasted_iota(jnp.int32, (rows, rows), 0)
    c_i = lax.broadcasted_iota(jnp.int32, (rows, rows), 1)
    before = jnp.where(r_i > c_i, 1.0, 0.0).astype(BF16)
    prefix = jnp.dot(before, oh.astype(BF16), preferred_element_type=F32)
    cnt = cnt_ref[...]
    rank = jnp.sum(jnp.where(onehot, prefix + cnt, 0.0), axis=-1, keepdims=True)
    cnt_ref[...] = cnt + jnp.sum(oh, axis=0, keepdims=True)
    return jnp.where(lane == META_BUCKET, bucket.astype(F32),
                     jnp.where(lane == META_W_LO, w_lo,
                               jnp.where(lane == META_W_HI, w_hi,
                                         jnp.where(lane == META_RANK, rank, 0.0))))


def _emit_rows(x, rows_ref, cnt_in_ref, cnt_ref, gf_ref, wr_ref):
    @pl.when(pl.program_id(0) == 0)
    def _():
        cnt_ref[...] = cnt_in_ref[...]
    rows_ref[:, :D_MODEL] = x
    rows_ref[:, D_MODEL:] = _route_meta(x, gf_ref[...], wr_ref[...], cnt_ref)


def _s5_kernel(x_ref, g_ref, win_ref, bm_ref, are_ref, aim_ref, cm_ref, d_ref, wout_ref, s0_ref,
               gf_ref, wr_ref, cnt_in_ref, rows_alias_ref,
               rows_ref, so_ref, cnt_ref, bu_ref, st_ref, *, n_seq_groups, steps, carry):
    del rows_alias_ref
    x = x_ref[...]
    h = _rms(x, g_ref[...]).astype(BF16)
    u = jnp.dot(h, win_ref[...], preferred_element_type=F32)
    ub = u.astype(BF16)
    for j in range(N_SLABS):
        bu_ref[:, j * SLAB:(j + 1) * SLAB] = jnp.dot(
            ub[:, j * SLAB_IN:(j + 1) * SLAB_IN], bm_ref[j], preferred_element_type=F32)

    if carry:
        @pl.when(pl.program_id(0) == 0)
        def _():
            st_ref[...] = s0_ref[0]

    for sg in range(n_seq_groups):
        for j in range(N_SLABS):
            for k in range(SLAB_HALF // REC_COLS):
                re0 = j * SLAB + k * REC_COLS
                im0 = re0 + SLAB_HALF
                ar = jnp.broadcast_to(are_ref[:, re0:re0 + REC_COLS], (SUBLANES, REC_COLS))
                ai = jnp.broadcast_to(aim_ref[:, re0:re0 + REC_COLS], (SUBLANES, REC_COLS))
                if carry:
                    sr = st_ref[:, re0:re0 + REC_COLS]
                    si = st_ref[:, im0:im0 + REC_COLS]
                else:
                    sr = s0_ref[sg, :, re0:re0 + REC_COLS]
                    si = s0_ref[sg, :, im0:im0 + REC_COLS]

                def step(t, c, re0=re0, im0=im0, ar=ar, ai=ai, sg=sg):
                    sr, si = c
                    row = pl.multiple_of((sg * steps + t) * SUBLANES, SUBLANES)
                    br = bu_ref[pl.ds(row, SUBLANES), re0:re0 + REC_COLS]
                    bi = bu_ref[pl.ds(row, SUBLANES), im0:im0 + REC_COLS]
                    nr = ar * sr - ai * si + br
                    ni = ar * si + ai * sr + bi
                    bu_ref[pl.ds(row, SUBLANES), re0:re0 + REC_COLS] = nr
                    bu_ref[pl.ds(row, SUBLANES), im0:im0 + REC_COLS] = ni
                    return nr, ni

                sr, si = lax.fori_loop(0, steps, step, (sr, si), unroll=min(steps, 8))
                if carry:
                    st_ref[:, re0:re0 + REC_COLS] = sr
                    st_ref[:, im0:im0 + REC_COLS] = si
                    so_ref[0, :, re0:re0 + REC_COLS] = sr
                    so_ref[0, :, im0:im0 + REC_COLS] = si
                else:
                    so_ref[sg, :, re0:re0 + REC_COLS] = sr
                    so_ref[sg, :, im0:im0 + REC_COLS] = si

    ys = [jnp.dot(bu_ref[:, j * SLAB:(j + 1) * SLAB].astype(BF16), cm_ref[j], preferred_element_type=F32)
          for j in range(N_SLABS)]
    y = jnp.concatenate(ys, axis=-1) + d_ref[...] * u
    y = jax.nn.gelu(y).astype(BF16)
    a = jnp.dot(y, wout_ref[...], preferred_element_type=F32)
    xo = x + a[:, :D_MODEL] * jax.nn.sigmoid(a[:, D_MODEL:])
    _emit_rows(xo, rows_ref, cnt_in_ref, cnt_ref, gf_ref, wr_ref)


def _const_spec(*shape):
    return pl.BlockSpec(shape, lambda i: (0,) * len(shape))


def _rows_alias(rows_prev, input_index):
    if rows_prev is None:
        return {}, jnp.zeros((SUBLANES, LANES), F32)
    return {input_index: 0}, rows_prev


def _rows_out(n_total_rows, block_off):
    return (pl.BlockSpec((MIX_ROWS, ROW_W), lambda i: (i + block_off, 0)),
            jax.ShapeDtypeStruct((n_total_rows, ROW_W), F32))


def _s5_layer(x_rows, s0, g, win, bmat, a_re, a_im, cmat, dvec, wout, g_ffn, w_router, cnt_in, rows_prev,
              *, n_total_rows, block_off, n_seq_groups, steps, carry):
    assert n_seq_groups * steps * SUBLANES == MIX_ROWS
    n_blocks = x_rows.shape[0] // MIX_ROWS
    s_idx = (lambda i: (0, 0, 0)) if carry else (lambda i: (i, 0, 0))
    kern = functools.partial(_s5_kernel, n_seq_groups=n_seq_groups, steps=steps, carry=carry)
    n_state_groups = 1 if carry else n_blocks * n_seq_groups
    rows_spec, rows_shape = _rows_out(n_total_rows, block_off)
    aliases, rows_prev = _rows_alias(rows_prev, 13)
    inputs = [x_rows, g, win, bmat, a_re, a_im, cmat, dvec, wout, s0, g_ffn, w_router, cnt_in, rows_prev]
    return pl.pallas_call(
        kern,
        grid=(n_blocks,),
        in_specs=[
            pl.BlockSpec((MIX_ROWS, D_MODEL), lambda i: (i, 0)),
            _const_spec(1, D_MODEL), _const_spec(D_MODEL, SSM_WIDTH), _const_spec(N_SLABS, SLAB_IN, SLAB),
            _const_spec(1, STATE_COLS), _const_spec(1, STATE_COLS), _const_spec(N_SLABS, SLAB, SLAB_IN),
            _const_spec(1, SSM_WIDTH), _const_spec(SSM_WIDTH, 2 * D_MODEL),
            pl.BlockSpec((n_seq_groups, SUBLANES, STATE_COLS), s_idx),
            _const_spec(1, D_MODEL), _const_spec(D_MODEL, ROUTER_LANES), _const_spec(1, LANES),
            pl.BlockSpec(memory_space=pl.ANY),
        ],
        out_specs=[
            rows_spec,
            pl.BlockSpec((n_seq_groups, SUBLANES, STATE_COLS), s_idx),
            _const_spec(1, LANES),
        ],
        out_shape=[
            rows_shape,
            jax.ShapeDtypeStruct((n_state_groups, SUBLANES, STATE_COLS), F32),
            jax.ShapeDtypeStruct((1, LANES), F32),
        ],
        input_output_aliases=aliases,
        scratch_shapes=[pltpu.VMEM((MIX_ROWS, STATE_COLS), F32), pltpu.VMEM((SUBLANES, STATE_COLS), F32)],
        compiler_params=pltpu.CompilerParams(
            dimension_semantics=("arbitrary",), vmem_limit_bytes=VMEM_LIMIT_BYTES),
        name="s5_carry" if carry else "s5_step",
    )(*inputs)


def _sgu_kernel(x_ref, g_ref, win_ref, ng_ref, nb_ref, ws_ref, bs_ref, wout_ref,
                gf_ref, wr_ref, cnt_in_ref, rows_alias_ref, rows_ref, cnt_ref, *maybe_v_ref, block_diag):
    del rows_alias_ref
    x = x_ref[...]
    rows = x.shape[0]
    h = _rms(x, g_ref[...]).astype(BF16)
    hh = jax.nn.gelu(jnp.dot(h, win_ref[...], preferred_element_type=F32))
    u = hh[:, :SGU_WIDTH]
    v = hh[:, SGU_WIDTH:]
    vc = v - jnp.mean(v, axis=-1, keepdims=True)
    v = vc * lax.rsqrt(jnp.mean(vc * vc, axis=-1, keepdims=True) + EPS) * ng_ref[...] + nb_ref[...]
    if maybe_v_ref:
        maybe_v_ref[0][...] = v
    r = lax.broadcasted_iota(jnp.int32, (CHUNK, CHUNK), 0)
    c = lax.broadcasted_iota(jnp.int32, (CHUNK, CHUNK), 1)
    if block_diag:
        keep = (r // SUBLANES == c // SUBLANES) & (r >= c)
    else:
        keep = r >= c
    vb = v.astype(BF16)
    n_chunks = rows // CHUNK
    zs = []
    for hd in range(SGU_HEADS):
        wm = jnp.where(keep, ws_ref[hd], 0.0).astype(BF16)
        cols = slice(hd * SGU_HEAD_DIM, (hd + 1) * SGU_HEAD_DIM)
        vcat = jnp.concatenate([vb[ck * CHUNK:(ck + 1) * CHUNK, cols] for ck in range(n_chunks)], axis=-1)
        zcat = jnp.dot(wm, vcat, preferred_element_type=F32)
        zs.append([zcat[:, ck * SGU_HEAD_DIM:(ck + 1) * SGU_HEAD_DIM] for ck in range(n_chunks)])
    z = jnp.concatenate(
        [jnp.concatenate([zs[hd][ck] for hd in range(SGU_HEADS)], axis=-1) + bs_ref[...]
         for ck in range(n_chunks)], axis=0)
    o = jnp.dot((u * z).astype(BF16), wout_ref[...], preferred_element_type=F32)
    _emit_rows(x + o, rows_ref, cnt_in_ref, cnt_ref, gf_ref, wr_ref)


def _sgu_layer(x_rows, g, win, ng, nb, ws, bs, wout, g_ffn, w_router, cnt_in, rows_prev,
               *, n_total_rows, n_blocks, block_off, block_diag, emit_v):
    row_spec = pl.BlockSpec((MIX_ROWS, D_MODEL), lambda i: (i + block_off, 0))
    rows_spec, rows_shape = _rows_out(n_total_rows, block_off)
    out_shape = [rows_shape, jax.ShapeDtypeStruct((1, LANES), F32)]
    out_specs = [rows_spec, _const_spec(1, LANES)]
    if emit_v:
        out_shape.append(jax.ShapeDtypeStruct((n_blocks * MIX_ROWS, SGU_WIDTH), F32))
        out_specs.append(pl.BlockSpec((MIX_ROWS, SGU_WIDTH), lambda i: (i, 0)))
    aliases, rows_prev = _rows_alias(rows_prev, 11)
    inputs = [x_rows, g, win, ng, nb, ws, bs, wout, g_ffn, w_router, cnt_in, rows_prev]
    return pl.pallas_call(
        functools.partial(_sgu_kernel, block_diag=block_diag),
        grid=(n_blocks,),
        in_specs=[row_spec, _const_spec(1, D_MODEL), _const_spec(D_MODEL, 2 * SGU_WIDTH),
                  _const_spec(1, SGU_WIDTH), _const_spec(1, SGU_WIDTH), _const_spec(SGU_HEADS, CHUNK, CHUNK),
                  _const_spec(CHUNK, SGU_WIDTH), _const_spec(SGU_WIDTH, D_MODEL),
                  _const_spec(1, D_MODEL), _const_spec(D_MODEL, ROUTER_LANES), _const_spec(1, LANES),
                  pl.BlockSpec(memory_space=pl.ANY)],
        out_specs=out_specs,
        out_shape=out_shape,
        input_output_aliases=aliases,
        compiler_params=pltpu.CompilerParams(
            dimension_semantics=("arbitrary",), vmem_limit_bytes=VMEM_LIMIT_BYTES),
        name="sgu_block_diag" if block_diag else "sgu_chunked",
    )(*inputs)


def _moe_plan(rows, cnt, n_tiles_max):
    i32 = jnp.int32
    bucket = rows[:, D_MODEL + META_BUCKET].astype(i32)
    rank = rows[:, D_MODEL + META_RANK].astype(i32)
    n = cnt[0, :N_BUCKETS].astype(i32)
    nt = (n + MOE_TILE - 1) // MOE_TILE
    tile_end = jnp.cumsum(nt)
    tile_start = tile_end - nt
    pos = tile_start[bucket] * MOE_TILE + rank
    n_tiles = tile_end[-1]
    j = jnp.minimum(jnp.arange(n_tiles_max, dtype=i32), n_tiles - 1)
    tb = jnp.minimum(jnp.sum((j[:, None] >= tile_end[None, :]).astype(i32), axis=1), N_BUCKETS - 1)
    grp = tb // N_PAIRS
    pair = tb % N_PAIRS
    e_lo = grp * EXPERTS_PER_GROUP + jnp.asarray(PAIR_LO, i32)[pair]
    e_hi = grp * EXPERTS_PER_GROUP + jnp.asarray(PAIR_HI, i32)[pair]
    zrow = jnp.where(nt > 0, (tile_end - 1) * MOE_TILE, -1).astype(i32)
    return pos, j, e_lo, e_hi, n_tiles.reshape(1), zrow


def _dispatch_kernel(pos_ref, zrow_ref, rows_ref, xs_ref, zbuf_ref, sem, zsem):
    i = pl.program_id(0)
    n = rows_ref.shape[0]

    @pl.when(i == 0)
    def _():
        zbuf_ref[...] = jnp.zeros_like(zbuf_ref)
        zero_copy = lambda b: pltpu.make_async_copy(
            zbuf_ref, xs_ref.at[pl.ds(pl.multiple_of(zrow_ref[b], MOE_TILE), MOE_TILE)], zsem)
        for b in range(N_BUCKETS):
            @pl.when(zrow_ref[b] >= 0)
            def _():
                zero_copy(b).start()
        for b in range(N_BUCKETS):
            @pl.when(zrow_ref[b] >= 0)
            def _():
                zero_copy(b).wait()

    def issue(r, c):
        p = pos_ref[i * n + r]
        pltpu.make_async_copy(rows_ref.at[pl.ds(r, 1)], xs_ref.at[pl.ds(p, 1)], sem).start()
        return c

    lax.fori_loop(0, n, issue, 0, unroll=8)

    def drain(r, c):
        pltpu.make_async_copy(rows_ref.at[pl.ds(r, 1)], xs_ref.at[pl.ds(0, 1)], sem).wait()
        return c

    lax.fori_loop(0, n, drain, 0, unroll=8)


def _dispatch(rows, pos, zrow, n_sorted_rows):
    n_blocks = rows.shape[0] // MIX_ROWS
    return pl.pallas_call(
        _dispatch_kernel,
        grid_spec=pltpu.PrefetchScalarGridSpec(
            num_scalar_prefetch=2,
            grid=(n_blocks,),
            in_specs=[pl.BlockSpec((MIX_ROWS, ROW_W), lambda i, pos, zrow: (i, 0))],
            out_specs=pl.BlockSpec(memory_space=pl.ANY),
            scratch_shapes=[pltpu.VMEM((MOE_TILE, ROW_W), F32), pltpu.SemaphoreType.DMA,
                            pltpu.SemaphoreType.DMA],
        ),
        out_shape=jax.ShapeDtypeStruct((n_sorted_rows, ROW_W), F32),
        compiler_params=pltpu.CompilerParams(
            dimension_semantics=("arbitrary",), vmem_limit_bytes=VMEM_LIMIT_BYTES),
        name="moe_dispatch",
    )(pos, zrow, rows)


def _moe_kernel(tile_ref, elo_ref, ehi_ref, nt_ref, xs_ref, g_ref, wg_lo, wu_lo, wd_lo, wg_hi, wu_hi, wd_hi,
                gf_ref, ys_ref, wgu_ref, wd_ref, *, final_norm):
    j = pl.program_id(0)

    @pl.when(j < nt_ref[0])
    def _():
        prev = jnp.maximum(j - 1, 0)
        for s, (e_ref, wg, wu, wd) in enumerate(((elo_ref, wg_lo, wu_lo, wd_lo), (ehi_ref, wg_hi, wu_hi, wd_hi))):
            @pl.when((j == 0) | (e_ref[j] != e_ref[prev]))
            def _():
                wgu_ref[s, :, :D_EXPERT] = wg[0].astype(BF16)
                wgu_ref[s, :, D_EXPERT:] = wu[0].astype(BF16)
                wd_ref[s] = wd[0].astype(BF16)

        x = xs_ref[:, :D_MODEL]
        xn = _rms(x, g_ref[...]).astype(BF16)
        out = None
        for s, lane in enumerate((META_W_LO, META_W_HI)):
            gu = jnp.dot(xn, wgu_ref[s], preferred_element_type=F32)
            w = xs_ref[:, D_MODEL + lane:D_MODEL + lane + 1]
            hcur = (jax.nn.silu(gu[:, :D_EXPERT]) * gu[:, D_EXPERT:] * w).astype(BF16)
            o = jnp.dot(hcur, wd_ref[s], preferred_element_type=F32)
            out = o if out is None else out + o
        y = x + out
        if final_norm:
            y = _rms(y, gf_ref[...])
        ys_ref[...] = y


def _moe_experts(xs, tile, e_lo, e_hi, n_tiles, g, w_gate, w_up, w_down, g_final, *, final_norm):
    n_tiles_max = tile.shape[0]
    const = lambda *shape: pl.BlockSpec(shape, lambda j, *_: (0,) * len(shape))
    w_in = lambda which: pl.BlockSpec(
        (1, D_MODEL, D_EXPERT), lambda j, tile, elo, ehi, nt: ((elo, ehi)[which][j], 0, 0))
    w_out = lambda which: pl.BlockSpec(
        (1, D_EXPERT, D_MODEL), lambda j, tile, elo, ehi, nt: ((elo, ehi)[which][j], 0, 0))
    return pl.pallas_call(
        functools.partial(_moe_kernel, final_norm=final_norm),
        grid_spec=pltpu.PrefetchScalarGridSpec(
            num_scalar_prefetch=4,
            grid=(n_tiles_max,),
            in_specs=[pl.BlockSpec((MOE_TILE, ROW_W), lambda j, tile, elo, ehi, nt: (tile[j], 0)),
                      const(1, D_MODEL), w_in(0), w_in(0), w_out(0), w_in(1), w_in(1), w_out(1),
                      const(1, D_MODEL)],
            out_specs=pl.BlockSpec((MOE_TILE, D_MODEL), lambda j, tile, elo, ehi, nt: (tile[j], 0)),
            scratch_shapes=[pltpu.VMEM((2, D_MODEL, 2 * D_EXPERT), BF16), pltpu.VMEM((2, D_EXPERT, D_MODEL), BF16)],
        ),
        out_shape=jax.ShapeDtypeStruct((xs.shape[0], D_MODEL), F32),
        compiler_params=pltpu.CompilerParams(
            dimension_semantics=("arbitrary",), vmem_limit_bytes=VMEM_LIMIT_BYTES),
        name="moe_experts_final" if final_norm else "moe_experts",
    )(tile, e_lo, e_hi, n_tiles, xs, g, w_gate, w_up, w_down, w_gate, w_up, w_down, g_final)


def _gather_kernel(pos_ref, ys_ref, *refs, n_first_blocks):
    *o_refs, sem = refs
    i = pl.program_id(0)

    def fetch(o_ref):
        n = o_ref.shape[0]

        def issue(r, c):
            p = pos_ref[i * n + r]
            pltpu.make_async_copy(ys_ref.at[pl.ds(p, 1)], o_ref.at[pl.ds(r, 1)], sem).start()
            return c

        lax.fori_loop(0, n, issue, 0, unroll=8)

        def drain(r, c):
            pltpu.make_async_copy(ys_ref.at[pl.ds(0, 1)], o_ref.at[pl.ds(r, 1)], sem).wait()
            return c

        lax.fori_loop(0, n, drain, 0, unroll=8)

    if len(o_refs) == 1:
        fetch(o_refs[0])
    else:
        pl.when(i < n_first_blocks)(lambda: fetch(o_refs[0]))
        pl.when(i >= n_first_blocks)(lambda: fetch(o_refs[1]))


def _gather(ys, pos, n_first_rows=None):
    n_blocks = pos.shape[0] // MIX_ROWS
    spec = lambda f: pl.BlockSpec((MIX_ROWS, D_MODEL), f)
    if n_first_rows is None:
        n_first = n_blocks
        out_specs = [spec(lambda i, pos: (i, 0))]
        out_shape = [jax.ShapeDtypeStruct((pos.shape[0], D_MODEL), F32)]
    else:
        n_first = n_first_rows // MIX_ROWS
        out_specs = [spec(lambda i, pos: (jnp.minimum(i, n_first - 1), 0)),
                     spec(lambda i, pos: (jnp.maximum(i - n_first, 0), 0))]
        out_shape = [jax.ShapeDtypeStruct((n_first_rows, D_MODEL), F32),
                     jax.ShapeDtypeStruct((pos.shape[0] - n_first_rows, D_MODEL), F32)]
    return pl.pallas_call(
        functools.partial(_gather_kernel, n_first_blocks=n_first),
        grid_spec=pltpu.PrefetchScalarGridSpec(
            num_scalar_prefetch=1,
            grid=(n_blocks,),
            in_specs=[pl.BlockSpec(memory_space=pl.ANY)],
            out_specs=out_specs,
            scratch_shapes=[pltpu.SemaphoreType.DMA],
        ),
        out_shape=out_shape,
        compiler_params=pltpu.CompilerParams(
            dimension_semantics=("arbitrary",), vmem_limit_bytes=VMEM_LIMIT_BYTES),
        name="moe_gather",
    )(pos, ys)


def _moe_layer(rows, cnt, order, g, w_gate, w_up, w_down, g_final, *, final_norm, n_first_rows=None):
    n_tiles_max = rows.shape[0] // MOE_TILE + N_BUCKETS
    pos, tile, e_lo, e_hi, n_tiles, zrow = _moe_plan(rows, cnt, n_tiles_max)
    xs = _dispatch(rows, pos, zrow, n_tiles_max * MOE_TILE)
    ys = _moe_experts(xs, tile, e_lo, e_hi, n_tiles, g, w_gate, w_up, w_down, g_final, final_norm=final_norm)
    return _gather(ys, pos if order is None else pos[order], n_first_rows)


def _router_weights(w_rg, w_re):
    pad = ROUTER_LANES - N_EXPERT_GROUPS - N_EXPERTS
    return jnp.concatenate([w_rg.astype(F32), w_re.astype(F32), jnp.zeros((D_MODEL, pad), F32)], axis=-1)


def kernel(x_prompt, x_sample, state_ssm_re, state_ssm_im, norm_mix, norm_ffn, norm_final, ssm_w_in, ssm_lambda_re, ssm_lambda_im, ssm_log_step, ssm_b_re, ssm_b_im, ssm_c_re, ssm_c_im, ssm_d, ssm_w_out, sgu_w_in, sgu_norm_g, sgu_norm_b, sgu_w_s, sgu_b_s, sgu_w_out, moe_router_group, moe_router_expert, moe_w_gate, moe_w_up, moe_w_down):
    nb, ns, dm = x_prompt.shape
    db, ds, _ = x_sample.shape
    n_p, n_s = nb * ns, db * ds
    n_tok = n_p + n_s
    n_sg = db // SUBLANES
    steps_p = MIX_ROWS // SUBLANES
    sg_per_block = MIX_ROWS // (ds * SUBLANES)
    assert nb == SUBLANES and dm == D_MODEL and ns % steps_p == 0 and n_sg % sg_per_block == 0
    assert n_p % MIX_ROWS == 0 and n_s % MIX_ROWS == 0 and MIX_ROWS % MOE_TILE == 0 and CHUNK % ds == 0
    row = lambda a: a.astype(F32).reshape(1, -1)
    zero_cnt = jnp.zeros((1, LANES), F32)

    lb_re, lb_im, bb_re, bb_im = _s5_prep(ssm_lambda_re, ssm_lambda_im, ssm_log_step, ssm_b_re, ssm_b_im)
    bmat, cmat, a_re, a_im, dvec = _s5_matrices(lb_re, lb_im, bb_re, bb_im, ssm_c_re, ssm_c_im, ssm_d)
    xp = jnp.transpose(x_prompt, (1, 0, 2)).reshape(n_p, dm)
    xs = jnp.transpose(x_sample.reshape(n_sg, SUBLANES, ds, dm), (0, 2, 1, 3)).reshape(n_s, dm)
    zero_state = jnp.zeros((1, SUBLANES, STATE_COLS), F32)
    s0 = _pack_state(state_ssm_re.astype(F32), state_ssm_im.astype(F32)).reshape(n_sg, SUBLANES, STATE_COLS)
    wr0 = _router_weights(moe_router_group[0], moe_router_expert[0])
    s5 = functools.partial(_s5_layer, g=row(norm_mix[0]), win=ssm_w_in.astype(BF16), bmat=bmat, a_re=a_re,
                           a_im=a_im, cmat=cmat, dvec=dvec, wout=ssm_w_out.astype(BF16),
                           g_ffn=row(norm_ffn[0]), w_router=wr0, n_total_rows=n_tok)
    rows0, st_p, cnt0 = s5(xp, zero_state, cnt_in=zero_cnt, rows_prev=None, block_off=0,
                           n_seq_groups=1, steps=steps_p, carry=True)
    rows0, st_s, cnt0 = s5(xs, s0, cnt_in=cnt0, rows_prev=rows0, block_off=n_p // MIX_ROWS,
                           n_seq_groups=sg_per_block, steps=ds, carry=False)
    ssm_re_p, ssm_im_p = _unpack_state(st_p.reshape(nb, STATE_COLS))
    ssm_re_s, ssm_im_s = _unpack_state(st_s.reshape(db, STATE_COLS))

    idx = jnp.arange(n_tok, dtype=jnp.int32)
    order = jnp.concatenate([
        idx[:n_p].reshape(ns, nb).T.reshape(-1),
        jnp.transpose(idx[n_p:].reshape(n_sg, ds, SUBLANES), (0, 2, 1)).reshape(-1)])
    x1, = _moe_layer(rows0, cnt0, order, row(norm_ffn[0]), moe_w_gate[0], moe_w_up[0], moe_w_down[0],
                     row(norm_final), final_norm=False)

    ws_p = sgu_w_s.astype(F32)[:, :CHUNK, :CHUNK]
    bs_p = jnp.repeat(sgu_b_s.astype(F32)[:, :CHUNK].T, SGU_HEAD_DIM, axis=1)
    reps = CHUNK // ds
    ws_s = jnp.tile(sgu_w_s.astype(F32)[:, :ds, :ds], (1, reps, reps))
    bs_s = jnp.tile(jnp.repeat(sgu_b_s.astype(F32)[:, :ds].T, SGU_HEAD_DIM, axis=1), (reps, 1))
    wr1 = _router_weights(moe_router_group[1], moe_router_expert[1])
    sgu = functools.partial(_sgu_layer, g=row(norm_mix[1]), win=sgu_w_in.astype(BF16), ng=row(sgu_norm_g),
                            nb=row(sgu_norm_b), wout=sgu_w_out.astype(BF16), g_ffn=row(norm_ffn[1]),
                            w_router=wr1, n_total_rows=n_tok)
    rows1, cnt1 = sgu(x1, ws=ws_p, bs=bs_p, cnt_in=zero_cnt, rows_prev=None, n_blocks=n_p // MIX_ROWS,
                      block_off=0, block_diag=False, emit_v=False)
    rows1, cnt1, v_s = sgu(x1, ws=ws_s, bs=bs_s, cnt_in=cnt1, rows_prev=rows1, n_blocks=n_s // MIX_ROWS,
                           block_off=n_p // MIX_ROWS, block_diag=True, emit_v=True)
    y_p, y_s = _moe_layer(rows1, cnt1, None, row(norm_ffn[1]), moe_w_gate[1], moe_w_up[1], moe_w_down[1],
                          row(norm_final), final_norm=True, n_first_rows=n_p)
    return (y_p.reshape(nb, ns, dm), y_s.reshape(db, ds, dm), ssm_re_p, ssm_im_p, ssm_re_s, ssm_im_s,
            v_s.reshape(db, ds, SGU_WIDTH))
```

```python
import functools

import jax
import jax.numpy as jnp
from jax import lax
from jax.experimental import pallas as pl
from jax.experimental.pallas import tpu as pltpu

D_MODEL = 1024
SSM_WIDTH = D_MODEL // 2
SSM_GROUP = 16
SSM_GROUPS = SSM_WIDTH // SSM_GROUP
SSM_STATE = 64
SGU_WIDTH = D_MODEL
SGU_HEADS = 8
SGU_HEAD_DIM = SGU_WIDTH // SGU_HEADS
CHUNK = 128
N_EXPERT_GROUPS = 4
EXPERTS_PER_GROUP = 4
N_EXPERTS = N_EXPERT_GROUPS * EXPERTS_PER_GROUP
D_EXPERT = D_MODEL // 4
EPS = 1e-6

SUBLANES = 8
LANES = 128
VMEM_LIMIT_BYTES = 56 * 1024 * 1024

SLAB_GROUPS = 16
N_SLABS = SSM_GROUPS // SLAB_GROUPS
SLAB_HALF = SLAB_GROUPS * SSM_STATE
SLAB = 2 * SLAB_HALF
STATE_COLS = N_SLABS * SLAB
SLAB_IN = SLAB_GROUPS * SSM_GROUP
REC_COLS = 512

PAIR_LO = (0, 0, 0, 1, 1, 2)
PAIR_HI = (1, 2, 3, 2, 3, 3)
N_PAIRS = len(PAIR_LO)
N_BUCKETS = N_EXPERT_GROUPS * N_PAIRS
MOE_TILE = 256
MIX_ROWS = 512
META_LANES = LANES
ROW_W = D_MODEL + META_LANES
META_BUCKET, META_W_LO, META_W_HI, META_RANK = 0, 1, 2, 3
ROUTER_LANES = LANES
BF16 = jnp.bfloat16
F32 = jnp.float32


def _rms(x, g):
    return x * lax.rsqrt(jnp.mean(x * x, axis=-1, keepdims=True) + EPS) * g


def _s5_prep_kernel(lr_ref, li_ref, ls_ref, br_ref, bi_ref, lbr_ref, lbi_ref, bbr_ref, bbi_ref):
    lr = lr_ref[...]
    li = li_ref[...]
    dt = jnp.exp(ls_ref[...])
    mag = jnp.exp(lr * dt)
    ang = li * dt
    lb_re = mag * jnp.cos(ang)
    lb_im = mag * jnp.sin(ang)
    den = lr * lr + li * li
    nr = lb_re - 1.0
    coef_re = (nr * lr + lb_im * li) / den
    coef_im = (lb_im * lr - nr * li) / den
    br = br_ref[...]
    bi = bi_ref[...]
    lbr_ref[...] = lb_re
    lbi_ref[...] = lb_im
    bbr_ref[...] = coef_re * br - coef_im * bi
    bbi_ref[...] = coef_re * bi + coef_im * br


def _s5_prep(lam_re, lam_im, log_step, b_re, b_im):
    rows = SSM_GROUPS * SSM_GROUP
    rep = lambda a: jnp.repeat(a.astype(F32), SSM_GROUP, axis=0)
    lr = rep(lam_re)
    li = rep(lam_im)
    ls = rep(jnp.broadcast_to(log_step.astype(F32)[:, None], (SSM_GROUPS, SSM_STATE)))
    br = jnp.transpose(b_re.astype(F32), (0, 2, 1)).reshape(rows, SSM_STATE)
    bi = jnp.transpose(b_im.astype(F32), (0, 2, 1)).reshape(rows, SSM_STATE)
    sds = jax.ShapeDtypeStruct((rows, SSM_STATE), F32)
    lb_re, lb_im, bb_re, bb_im = pl.pallas_call(
        _s5_prep_kernel, out_shape=(sds, sds, sds, sds), name="s5_prep")(lr, li, ls, br, bi)
    return lb_re, lb_im, bb_re, bb_im


def _to_state_cols(a):
    return a.reshape(a.shape[:-2] + (N_SLABS, SLAB_HALF))


def _pack_state(re, im):
    return jnp.concatenate([_to_state_cols(re), _to_state_cols(im)], axis=-1).reshape(re.shape[0], STATE_COLS)


def _unpack_state(s):
    s = s.reshape(s.shape[0], N_SLABS, 2, SLAB_GROUPS, SSM_STATE)
    re = s[:, :, 0].reshape(s.shape[0], SSM_GROUPS, SSM_STATE)
    im = s[:, :, 1].reshape(s.shape[0], SSM_GROUPS, SSM_STATE)
    return re, im


def _s5_matrices(lb_re, lb_im, bb_re, bb_im, c_re, c_im, d):
    eye = jnp.eye(SLAB_GROUPS, dtype=F32)
    def in_side(bb):
        bb = bb.reshape(N_SLABS, SLAB_GROUPS, SSM_GROUP, SSM_STATE)
        m = bb[:, :, :, None, :] * eye[None, :, None, :, None]
        return m.reshape(N_SLABS, SLAB_IN, SLAB_HALF)
    bmat = jnp.concatenate([in_side(bb_re), in_side(bb_im)], axis=-1).astype(BF16)
    def out_side(c):
        c = jnp.transpose(c.astype(F32), (0, 2, 1)).reshape(N_SLABS, SLAB_GROUPS, SSM_STATE, SSM_GROUP)
        m = c[:, :, :, None, :] * eye[None, :, None, :, None]
        return m.reshape(N_SLABS, SLAB_HALF, SLAB_IN)
    cmat = jnp.concatenate([out_side(c_re), -out_side(c_im)], axis=1).astype(BF16)
    lam = lambda lb: _to_state_cols(lb.reshape(SSM_GROUPS, SSM_GROUP, SSM_STATE)[:, 0])
    a_re = jnp.concatenate([lam(lb_re), lam(lb_re)], axis=-1).reshape(1, STATE_COLS)
    a_im = jnp.concatenate([lam(lb_im), lam(lb_im)], axis=-1).reshape(1, STATE_COLS)
    return bmat, cmat, a_re, a_im, d.astype(F32).reshape(1, SSM_WIDTH)


def _route_meta(x, g, wr, cnt_ref):
    xn = _rms(x, g)
    xh = xn.astype(BF16)
    xl = (xn - xh.astype(F32)).astype(BF16)
    p = jnp.dot(xh, wr, preferred_element_type=F32)
    q = jnp.dot(xl, wr[:, :ROUTER_LANES], preferred_element_type=F32)
    logits = p[:, :ROUTER_LANES] + (p[:, ROUTER_LANES:] + q)
    rows = logits.shape[0]
    lane = lax.broadcasted_iota(jnp.int32, logits.shape, 1)
    neg = jnp.float32(-jnp.inf)
    big = jnp.int32(ROUTER_LANES)
    is_g = lane < N_EXPERT_GROUPS
    lg = jnp.where(is_g, logits, neg)
    mg = jnp.max(lg, axis=-1, keepdims=True)
    g_idx = jnp.min(jnp.where(lg == mg, lane, big), axis=-1, keepdims=True)
    g_w = 1.0 / jnp.sum(jnp.where(is_g, jnp.exp(lg - mg), 0.0), axis=-1, keepdims=True)
    first = N_EXPERT_GROUPS + g_idx * EXPERTS_PER_GROUP
    in_grp = (lane >= first) & (lane < first + EXPERTS_PER_GROUP)
    le = jnp.where(in_grp, logits, neg)
    m1 = jnp.max(le, axis=-1, keepdims=True)
    i1 = jnp.min(jnp.where(le == m1, lane, big), axis=-1, keepdims=True)
    le2 = jnp.where(lane == i1, neg, le)
    m2 = jnp.max(le2, axis=-1, keepdims=True)
    i2 = jnp.min(jnp.where(le2 == m2, lane, big), axis=-1, keepdims=True)
    e2 = jnp.exp(m2 - m1)
    w1 = g_w / (1.0 + e2)
    w2 = g_w * e2 / (1.0 + e2)
    lo = jnp.minimum(i1, i2) - first
    hi = jnp.maximum(i1, i2) - first
    pair = jnp.where(lo == 0, 0, jnp.where(lo == 1, 3, 5)) + (hi - lo - 1)
    bucket = g_idx * N_PAIRS + pair
    w_lo = jnp.where(i1 < i2, w1, w2)
    w_hi = jnp.where(i1 < i2, w2, w1)
    onehot = lane == bucket
    oh = jnp.where(onehot, 1.0, 0.0)
    r_i = lax.broadcasted_iota(jnp.int32, (rows, rows), 0)
    c_i = lax.broadcasted_iota(jnp.int32, (rows, rows), 1)
    before = jnp.where(r_i > c_i, 1.0, 0.0).astype(BF16)
    prefix = jnp.dot(before, oh.astype(BF16), preferred_element_type=F32)
    cnt = cnt_ref[...]
    rank = jnp.sum(jnp.where(onehot, prefix + cnt, 0.0), axis=-1, keepdims=True)
    cnt_ref[...] = cnt + jnp.sum(oh, axis=0, keepdims=True)
    return jnp.where(lane == META_BUCKET, bucket.astype(F32),
                     jnp.where(lane == META_W_LO, w_lo,
                               jnp.where(lane == META_W_HI, w_hi,
                                         jnp.where(lane == META_RANK, rank, 0.0))))


def _emit_rows(x, rows_ref, meta_ref, cnt_in_ref, cnt_ref, gf_ref, wr_ref):
    @pl.when(pl.program_id(0) == 0)
    def _():
        cnt_ref[...] = cnt_in_ref[...]
    meta = _route_meta(x, gf_ref[...], wr_ref[...], cnt_ref)
    rows_ref[:, :D_MODEL] = x
    rows_ref[:, D_MODEL:] = meta
    meta_ref[...] = meta


def _s5_kernel(x_ref, g_ref, win_ref, bm_ref, are_ref, aim_ref, cm_ref, d_ref, wout_ref, s0_ref,
               gf_ref, wr_ref, cnt_in_ref, rows_alias_ref,
               rows_ref, meta_ref, so_ref, cnt_ref, bu_ref, st_ref, *, n_seq_groups, steps, carry):
    del rows_alias_ref
    x = x_ref[...]
    h = _rms(x, g_ref[...]).astype(BF16)
    u = jnp.dot(h, win_ref[...], preferred_element_type=F32)
    ub = u.astype(BF16)
    for j in range(N_SLABS):
        bu_ref[:, j * SLAB:(j + 1) * SLAB] = jnp.dot(
            ub[:, j * SLAB_IN:(j + 1) * SLAB_IN], bm_ref[j], preferred_element_type=F32)

    if carry:
        @pl.when(pl.program_id(0) == 0)
        def _():
            st_ref[...] = s0_ref[0]

    for sg in range(n_seq_groups):
        for j in range(N_SLABS):
            for k in range(SLAB_HALF // REC_COLS):
                re0 = j * SLAB + k * REC_COLS
                im0 = re0 + SLAB_HALF
                ar = jnp.broadcast_to(are_ref[:, re0:re0 + REC_COLS], (SUBLANES, REC_COLS))
                ai = jnp.broadcast_to(aim_ref[:, re0:re0 + REC_COLS], (SUBLANES, REC_COLS))
                if carry:
                    sr = st_ref[:, re0:re0 + REC_COLS]
                    si = st_ref[:, im0:im0 + REC_COLS]
                else:
                    sr = s0_ref[sg, :, re0:re0 + REC_COLS]
                    si = s0_ref[sg, :, im0:im0 + REC_COLS]

                def step(t, c, re0=re0, im0=im0, ar=ar, ai=ai, sg=sg):
                    sr, si = c
                    row = pl.multiple_of((sg * steps + t) * SUBLANES, SUBLANES)
                    br = bu_ref[pl.ds(row, SUBLANES), re0:re0 + REC_COLS]
                    bi = bu_ref[pl.ds(row, SUBLANES), im0:im0 + REC_COLS]
                    nr = ar * sr - ai * si + br
                    ni = ar * si + ai * sr + bi
                    bu_ref[pl.ds(row, SUBLANES), re0:re0 + REC_COLS] = nr
                    bu_ref[pl.ds(row, SUBLANES), im0:im0 + REC_COLS] = ni
                    return nr, ni

                sr, si = lax.fori_loop(0, steps, step, (sr, si), unroll=min(steps, 8))
                if carry:
                    st_ref[:, re0:re0 + REC_COLS] = sr
                    st_ref[:, im0:im0 + REC_COLS] = si
                    so_ref[0, :, re0:re0 + REC_COLS] = sr
                    so_ref[0, :, im0:im0 + REC_COLS] = si
                else:
                    so_ref[sg, :, re0:re0 + REC_COLS] = sr
                    so_ref[sg, :, im0:im0 + REC_COLS] = si

    ys = [jnp.dot(bu_ref[:, j * SLAB:(j + 1) * SLAB].astype(BF16), cm_ref[j], preferred_element_type=F32)
          for j in range(N_SLABS)]
    y = jnp.concatenate(ys, axis=-1) + d_ref[...] * u
    y = jax.nn.gelu(y).astype(BF16)
    a = jnp.dot(y, wout_ref[...], preferred_element_type=F32)
    xo = x + a[:, :D_MODEL] * jax.nn.sigmoid(a[:, D_MODEL:])
    _emit_rows(xo, rows_ref, meta_ref, cnt_in_ref, cnt_ref, gf_ref, wr_ref)


def _const_spec(*shape):
    return pl.BlockSpec(shape, lambda i: (0,) * len(shape))


_META_SPEC = pl.BlockSpec((MIX_ROWS, META_LANES), lambda i: (i, 0))


def _rows_alias(rows_prev, input_index):
    if rows_prev is None:
        return {}, jnp.zeros((SUBLANES, LANES), F32)
    return {input_index: 0}, rows_prev


def _rows_out(n_total_rows, block_off):
    return (pl.BlockSpec((MIX_ROWS, ROW_W), lambda i: (i + block_off, 0)),
            jax.ShapeDtypeStruct((n_total_rows, ROW_W), F32))


def _s5_layer(x_rows, s0, g, win, bmat, a_re, a_im, cmat, dvec, wout, g_ffn, w_router, cnt_in, rows_prev,
              *, n_total_rows, block_off, n_seq_groups, steps, carry):
    assert n_seq_groups * steps * SUBLANES == MIX_ROWS
    n_blocks = x_rows.shape[0] // MIX_ROWS
    s_idx = (lambda i: (0, 0, 0)) if carry else (lambda i: (i, 0, 0))
    kern = functools.partial(_s5_kernel, n_seq_groups=n_seq_groups, steps=steps, carry=carry)
    n_state_groups = 1 if carry else n_blocks * n_seq_groups
    rows_spec, rows_shape = _rows_out(n_total_rows, block_off)
    aliases, rows_prev = _rows_alias(rows_prev, 13)
    inputs = [x_rows, g, win, bmat, a_re, a_im, cmat, dvec, wout, s0, g_ffn, w_router, cnt_in, rows_prev]
    return pl.pallas_call(
        kern,
        grid=(n_blocks,),
        in_specs=[
            pl.BlockSpec((MIX_ROWS, D_MODEL), lambda i: (i, 0)),
            _const_spec(1, D_MODEL), _const_spec(D_MODEL, SSM_WIDTH), _const_spec(N_SLABS, SLAB_IN, SLAB),
            _const_spec(1, STATE_COLS), _const_spec(1, STATE_COLS), _const_spec(N_SLABS, SLAB, SLAB_IN),
            _const_spec(1, SSM_WIDTH), _const_spec(SSM_WIDTH, 2 * D_MODEL),
            pl.BlockSpec((n_seq_groups, SUBLANES, STATE_COLS), s_idx),
            _const_spec(1, D_MODEL), _const_spec(D_MODEL, 2 * ROUTER_LANES), _const_spec(1, LANES),
            pl.BlockSpec(memory_space=pl.ANY),
        ],
        out_specs=[
            rows_spec, _META_SPEC,
            pl.BlockSpec((n_seq_groups, SUBLANES, STATE_COLS), s_idx),
            _const_spec(1, LANES),
        ],
        out_shape=[
            rows_shape, jax.ShapeDtypeStruct((n_blocks * MIX_ROWS, META_LANES), F32),
            jax.ShapeDtypeStruct((n_state_groups, SUBLANES, STATE_COLS), F32),
            jax.ShapeDtypeStruct((1, LANES), F32),
        ],
        input_output_aliases=aliases,
        scratch_shapes=[pltpu.VMEM((MIX_ROWS, STATE_COLS), F32), pltpu.VMEM((SUBLANES, STATE_COLS), F32)],
        compiler_params=pltpu.CompilerParams(
            dimension_semantics=("arbitrary",), vmem_limit_bytes=VMEM_LIMIT_BYTES),
        name="s5_carry" if carry else "s5_step",
    )(*inputs)


def _sgu_kernel(x_ref, g_ref, win_ref, ng_ref, nb_ref, ws_ref, bs_ref, wout_ref,
                gf_ref, wr_ref, cnt_in_ref, rows_alias_ref, rows_ref, meta_ref, cnt_ref, *maybe_v_ref,
                block_diag):
    del rows_alias_ref
    x = x_ref[...]
    rows = x.shape[0]
    h = _rms(x, g_ref[...]).astype(BF16)
    hh = jax.nn.gelu(jnp.dot(h, win_ref[...], preferred_element_type=F32))
    u = hh[:, :SGU_WIDTH]
    v = hh[:, SGU_WIDTH:]
    vc = v - jnp.mean(v, axis=-1, keepdims=True)
    v = vc * lax.rsqrt(jnp.mean(vc * vc, axis=-1, keepdims=True) + EPS) * ng_ref[...] + nb_ref[...]
    if maybe_v_ref:
        maybe_v_ref[0][...] = v
    r = lax.broadcasted_iota(jnp.int32, (CHUNK, CHUNK), 0)
    c = lax.broadcasted_iota(jnp.int32, (CHUNK, CHUNK), 1)
    if block_diag:
        keep = (r // SUBLANES == c // SUBLANES) & (r >= c)
    else:
        keep = r >= c
    vb = v.astype(BF16)
    n_chunks = rows // CHUNK
    zs = []
    for hd in range(SGU_HEADS):
        wm = jnp.where(keep, ws_ref[hd], 0.0).astype(BF16)
        cols = slice(hd * SGU_HEAD_DIM, (hd + 1) * SGU_HEAD_DIM)
        vcat = jnp.concatenate([vb[ck * CHUNK:(ck + 1) * CHUNK, cols] for ck in range(n_chunks)], axis=-1)
        zcat = jnp.dot(wm, vcat, preferred_element_type=F32)
        zs.append([zcat[:, ck * SGU_HEAD_DIM:(ck + 1) * SGU_HEAD_DIM] for ck in range(n_chunks)])
    z = jnp.concatenate(
        [jnp.concatenate([zs[hd][ck] for hd in range(SGU_HEADS)], axis=-1) + bs_ref[...]
         for ck in range(n_chunks)], axis=0)
    o = jnp.dot((u * z).astype(BF16), wout_ref[...], preferred_element_type=F32)
    _emit_rows(x + o, rows_ref, meta_ref, cnt_in_ref, cnt_ref, gf_ref, wr_ref)


def _sgu_layer(x_rows, g, win, ng, nb, ws, bs, wout, g_ffn, w_router, cnt_in, rows_prev,
               *, n_total_rows, n_blocks, block_off, block_diag, emit_v):
    row_spec = pl.BlockSpec((MIX_ROWS, D_MODEL), lambda i: (i + block_off, 0))
    rows_spec, rows_shape = _rows_out(n_total_rows, block_off)
    out_shape = [rows_shape, jax.ShapeDtypeStruct((n_blocks * MIX_ROWS, META_LANES), F32),
                 jax.ShapeDtypeStruct((1, LANES), F32)]
    out_specs = [rows_spec, _META_SPEC, _const_spec(1, LANES)]
    if emit_v:
        out_shape.append(jax.ShapeDtypeStruct((n_blocks * MIX_ROWS, SGU_WIDTH), F32))
        out_specs.append(pl.BlockSpec((MIX_ROWS, SGU_WIDTH), lambda i: (i, 0)))
    aliases, rows_prev = _rows_alias(rows_prev, 11)
    inputs = [x_rows, g, win, ng, nb, ws, bs, wout, g_ffn, w_router, cnt_in, rows_prev]
    return pl.pallas_call(
        functools.partial(_sgu_kernel, block_diag=block_diag),
        grid=(n_blocks,),
        in_specs=[row_spec, _const_spec(1, D_MODEL), _const_spec(D_MODEL, 2 * SGU_WIDTH),
                  _const_spec(1, SGU_WIDTH), _const_spec(1, SGU_WIDTH), _const_spec(SGU_HEADS, CHUNK, CHUNK),
                  _const_spec(CHUNK, SGU_WIDTH), _const_spec(SGU_WIDTH, D_MODEL),
                  _const_spec(1, D_MODEL), _const_spec(D_MODEL, 2 * ROUTER_LANES), _const_spec(1, LANES),
                  pl.BlockSpec(memory_space=pl.ANY)],
        out_specs=out_specs,
        out_shape=out_shape,
        input_output_aliases=aliases,
        compiler_params=pltpu.CompilerParams(
            dimension_semantics=("arbitrary",), vmem_limit_bytes=VMEM_LIMIT_BYTES),
        name="sgu_block_diag" if block_diag else "sgu_chunked",
    )(*inputs)


def _moe_plan(meta, cnt, n_tiles_max):
    i32 = jnp.int32
    bucket = meta[:, META_BUCKET].astype(i32)
    rank = meta[:, META_RANK].astype(i32)
    n = cnt[0, :N_BUCKETS].astype(i32)
    nt = (n + MOE_TILE - 1) // MOE_TILE
    tile_end = jnp.cumsum(nt)
    tile_start = tile_end - nt
    pos = tile_start[bucket] * MOE_TILE + rank
    n_tiles = tile_end[-1]
    j = jnp.minimum(jnp.arange(n_tiles_max, dtype=i32), jnp.maximum(n_tiles - 1, 0))
    tb = jnp.minimum(jnp.sum((j[:, None] >= tile_end[None, :]).astype(i32), axis=1), N_BUCKETS - 1)
    grp = tb // N_PAIRS
    pair = tb % N_PAIRS
    e_lo = grp * EXPERTS_PER_GROUP + jnp.asarray(PAIR_LO, i32)[pair]
    e_hi = grp * EXPERTS_PER_GROUP + jnp.asarray(PAIR_HI, i32)[pair]
    zrow = jnp.where(nt > 0, (tile_end - 1) * MOE_TILE, -1).astype(i32)
    return pos, j, e_lo, e_hi, n_tiles.reshape(1), zrow


def _dispatch_kernel(pos_ref, zrow_ref, rows_ref, xs_ref, zbuf_ref, sem, zsem):
    i = pl.program_id(0)
    n = rows_ref.shape[0]

    @pl.when(i == 0)
    def _():
        zbuf_ref[...] = jnp.zeros_like(zbuf_ref)
        zero_copy = lambda b: pltpu.make_async_copy(
            zbuf_ref, xs_ref.at[pl.ds(pl.multiple_of(zrow_ref[b], MOE_TILE), MOE_TILE)], zsem)
        for b in range(N_BUCKETS):
            @pl.when(zrow_ref[b] >= 0)
            def _():
                zero_copy(b).start()
        for b in range(N_BUCKETS):
            @pl.when(zrow_ref[b] >= 0)
            def _():
                zero_copy(b).wait()

    def issue(r, c):
        p = pos_ref[i * n + r]
        pltpu.make_async_copy(rows_ref.at[pl.ds(r, 1)], xs_ref.at[pl.ds(p, 1)], sem).start()
        return c

    lax.fori_loop(0, n, issue, 0, unroll=8)

    pltpu.make_async_copy(rows_ref, xs_ref.at[pl.ds(0, n)], sem).wait()


def _dispatch(rows, pos, zrow, n_sorted_rows):
    n_blocks = rows.shape[0] // MIX_ROWS
    return pl.pallas_call(
        _dispatch_kernel,
        grid_spec=pltpu.PrefetchScalarGridSpec(
            num_scalar_prefetch=2,
            grid=(n_blocks,),
            in_specs=[pl.BlockSpec((MIX_ROWS, ROW_W), lambda i, pos, zrow: (i, 0))],
            out_specs=pl.BlockSpec(memory_space=pl.ANY),
            scratch_shapes=[pltpu.VMEM((MOE_TILE, ROW_W), F32), pltpu.SemaphoreType.DMA,
                            pltpu.SemaphoreType.DMA],
        ),
        out_shape=jax.ShapeDtypeStruct((n_sorted_rows, ROW_W), F32),
        compiler_params=pltpu.CompilerParams(
            dimension_semantics=("arbitrary",), vmem_limit_bytes=VMEM_LIMIT_BYTES),
        name="moe_dispatch",
    )(pos, zrow, rows)


def _moe_kernel(tile_ref, elo_ref, ehi_ref, nt_ref, xs_ref, g_ref, wg_lo, wu_lo, wd_lo, wg_hi, wu_hi, wd_hi,
                gf_ref, ys_ref, wgu_ref, wd_ref, *, final_norm):
    j = pl.program_id(0)

    @pl.when(j < nt_ref[0])
    def _():
        prev = jnp.maximum(j - 1, 0)
        for s, (e_ref, wg, wu, wd) in enumerate(((elo_ref, wg_lo, wu_lo, wd_lo), (ehi_ref, wg_hi, wu_hi, wd_hi))):
            @pl.when((j == 0) | (e_ref[j] != e_ref[prev]))
            def _():
                wgu_ref[s, :, :D_EXPERT] = wg[0].astype(BF16)
                wgu_ref[s, :, D_EXPERT:] = wu[0].astype(BF16)
                wd_ref[s] = wd[0].astype(BF16)

        x = xs_ref[:, :D_MODEL]
        xn = _rms(x, g_ref[...]).astype(BF16)
        out = None
        for s, lane in enumerate((META_W_LO, META_W_HI)):
            gu = jnp.dot(xn, wgu_ref[s], preferred_element_type=F32)
            w = xs_ref[:, D_MODEL + lane:D_MODEL + lane + 1]
            hcur = (jax.nn.silu(gu[:, :D_EXPERT]) * gu[:, D_EXPERT:] * w).astype(BF16)
            o = jnp.dot(hcur, wd_ref[s], preferred_element_type=F32)
            out = o if out is None else out + o
        y = x + out
        if final_norm:
            y = _rms(y, gf_ref[...])
        ys_ref[...] = y


def _moe_experts(xs, tile, e_lo, e_hi, n_tiles, g, w_gate, w_up, w_down, g_final, *, final_norm):
    n_tiles_max = tile.shape[0]
    const = lambda *shape: pl.BlockSpec(shape, lambda j, *_: (0,) * len(shape))
    w_in = lambda which: pl.BlockSpec(
        (1, D_MODEL, D_EXPERT), lambda j, tile, elo, ehi, nt: ((elo, ehi)[which][j], 0, 0))
    w_out = lambda which: pl.BlockSpec(
        (1, D_EXPERT, D_MODEL), lambda j, tile, elo, ehi, nt: ((elo, ehi)[which][j], 0, 0))
    return pl.pallas_call(
        functools.partial(_moe_kernel, final_norm=final_norm),
        grid_spec=pltpu.PrefetchScalarGridSpec(
            num_scalar_prefetch=4,
            grid=(n_tiles_max,),
            in_specs=[pl.BlockSpec((MOE_TILE, ROW_W), lambda j, tile, elo, ehi, nt: (tile[j], 0)),
                      const(1, D_MODEL), w_in(0), w_in(0), w_out(0), w_in(1), w_in(1), w_out(1),
                      const(1, D_MODEL)],
            out_specs=pl.BlockSpec((MOE_TILE, D_MODEL), lambda j, tile, elo, ehi, nt: (tile[j], 0)),
            scratch_shapes=[pltpu.VMEM((2, D_MODEL, 2 * D_EXPERT), BF16), pltpu.VMEM((2, D_EXPERT, D_MODEL), BF16)],
        ),
        out_shape=jax.ShapeDtypeStruct((xs.shape[0], D_MODEL), F32),
        compiler_params=pltpu.CompilerParams(
            dimension_semantics=("arbitrary",), vmem_limit_bytes=VMEM_LIMIT_BYTES),
        name="moe_experts_final" if final_norm else "moe_experts",
    )(tile, e_lo, e_hi, n_tiles, xs, g, w_gate, w_up, w_down, w_gate, w_up, w_down, g_final)


def _gather_kernel(pos_ref, ys_ref, *refs, n_first_blocks):
    *o_refs, sem = refs
    i = pl.program_id(0)

    def fetch(o_ref):
        n = o_ref.shape[0]

        def issue(r, c):
            p = pos_ref[i * n + r]
            pltpu.make_async_copy(ys_ref.at[pl.ds(p, 1)], o_ref.at[pl.ds(r, 1)], sem).start()
            return c

        lax.fori_loop(0, n, issue, 0, unroll=8)

        pltpu.make_async_copy(ys_ref.at[pl.ds(0, n)], o_ref, sem).wait()

    if len(o_refs) == 1:
        fetch(o_refs[0])
    else:
        pl.when(i < n_first_blocks)(lambda: fetch(o_refs[0]))
        pl.when(i >= n_first_blocks)(lambda: fetch(o_refs[1]))


def _gather(ys, pos, n_first_rows=None):
    n_blocks = pos.shape[0] // MIX_ROWS
    spec = lambda f: pl.BlockSpec((MIX_ROWS, D_MODEL), f)
    if n_first_rows is None:
        n_first = n_blocks
        out_specs = [spec(lambda i, pos: (i, 0))]
        out_shape = [jax.ShapeDtypeStruct((pos.shape[0], D_MODEL), F32)]
    else:
        n_first = n_first_rows // MIX_ROWS
        out_specs = [spec(lambda i, pos: (jnp.minimum(i, n_first - 1), 0)),
                     spec(lambda i, pos: (jnp.maximum(i - n_first, 0), 0))]
        out_shape = [jax.ShapeDtypeStruct((n_first_rows, D_MODEL), F32),
                     jax.ShapeDtypeStruct((pos.shape[0] - n_first_rows, D_MODEL), F32)]
    return pl.pallas_call(
        functools.partial(_gather_kernel, n_first_blocks=n_first),
        grid_spec=pltpu.PrefetchScalarGridSpec(
            num_scalar_prefetch=1,
            grid=(n_blocks,),
            in_specs=[pl.BlockSpec(memory_space=pl.ANY)],
            out_specs=out_specs,
            scratch_shapes=[pltpu.SemaphoreType.DMA],
        ),
        out_shape=out_shape,
        compiler_params=pltpu.CompilerParams(
            dimension_semantics=("arbitrary",), vmem_limit_bytes=VMEM_LIMIT_BYTES),
        name="moe_gather",
    )(pos, ys)


def _moe_layer(rows, meta, cnt, order, g, w_gate, w_up, w_down, g_final, *, final_norm, n_first_rows=None):
    n_tiles_max = rows.shape[0] // MOE_TILE + N_BUCKETS
    pos, tile, e_lo, e_hi, n_tiles, zrow = _moe_plan(meta, cnt, n_tiles_max)
    xs = _dispatch(rows, pos, zrow, n_tiles_max * MOE_TILE)
    ys = _moe_experts(xs, tile, e_lo, e_hi, n_tiles, g, w_gate, w_up, w_down, g_final, final_norm=final_norm)
    return _gather(ys, pos if order is None else pos[order], n_first_rows)


def _router_weights(w_rg, w_re):
    pad = ROUTER_LANES - N_EXPERT_GROUPS - N_EXPERTS
    w = jnp.concatenate([w_rg.astype(F32), w_re.astype(F32), jnp.zeros((D_MODEL, pad), F32)], axis=-1)
    hi = w.astype(BF16)
    lo = (w - hi.astype(F32)).astype(BF16)
    return jnp.concatenate([hi, lo], axis=-1)


def kernel(x_prompt, x_sample, state_ssm_re, state_ssm_im, norm_mix, norm_ffn, norm_final, ssm_w_in, ssm_lambda_re, ssm_lambda_im, ssm_log_step, ssm_b_re, ssm_b_im, ssm_c_re, ssm_c_im, ssm_d, ssm_w_out, sgu_w_in, sgu_norm_g, sgu_norm_b, sgu_w_s, sgu_b_s, sgu_w_out, moe_router_group, moe_router_expert, moe_w_gate, moe_w_up, moe_w_down):
    nb, ns, dm = x_prompt.shape
    db, ds, _ = x_sample.shape
    n_p, n_s = nb * ns, db * ds
    n_tok = n_p + n_s
    n_sg = db // SUBLANES
    steps_p = MIX_ROWS // SUBLANES
    sg_per_block = MIX_ROWS // (ds * SUBLANES)
    assert nb == SUBLANES and dm == D_MODEL and ns % steps_p == 0 and n_sg % sg_per_block == 0
    assert n_p % MIX_ROWS == 0 and n_s % MIX_ROWS == 0 and MIX_ROWS % MOE_TILE == 0 and CHUNK % ds == 0
    row = lambda a: a.astype(F32).reshape(1, -1)
    zero_cnt = jnp.zeros((1, LANES), F32)

    lb_re, lb_im, bb_re, bb_im = _s5_prep(ssm_lambda_re, ssm_lambda_im, ssm_log_step, ssm_b_re, ssm_b_im)
    bmat, cmat, a_re, a_im, dvec = _s5_matrices(lb_re, lb_im, bb_re, bb_im, ssm_c_re, ssm_c_im, ssm_d)
    xp = jnp.transpose(x_prompt, (1, 0, 2)).reshape(n_p, dm)
    xs = jnp.transpose(x_sample.reshape(n_sg, SUBLANES, ds, dm), (0, 2, 1, 3)).reshape(n_s, dm)
    zero_state = jnp.zeros((1, SUBLANES, STATE_COLS), F32)
    s0 = _pack_state(state_ssm_re.astype(F32), state_ssm_im.astype(F32)).reshape(n_sg, SUBLANES, STATE_COLS)
    wr0 = _router_weights(moe_router_group[0], moe_router_expert[0])
    s5 = functools.partial(_s5_layer, g=row(norm_mix[0]), win=ssm_w_in.astype(BF16), bmat=bmat, a_re=a_re,
                           a_im=a_im, cmat=cmat, dvec=dvec, wout=ssm_w_out.astype(BF16),
                           g_ffn=row(norm_ffn[0]), w_router=wr0, n_total_rows=n_tok)
    rows0, meta_p, st_p, cnt0 = s5(xp, zero_state, cnt_in=zero_cnt, rows_prev=None, block_off=0,
                           n_seq_groups=1, steps=steps_p, carry=True)
    rows0, meta_s, st_s, cnt0 = s5(xs, s0, cnt_in=cnt0, rows_prev=rows0, block_off=n_p // MIX_ROWS,
                           n_seq_groups=sg_per_block, steps=ds, carry=False)
    ssm_re_p, ssm_im_p = _unpack_state(st_p.reshape(nb, STATE_COLS))
    ssm_re_s, ssm_im_s = _unpack_state(st_s.reshape(db, STATE_COLS))

    idx = jnp.arange(n_tok, dtype=jnp.int32)
    order = jnp.concatenate([
        idx[:n_p].reshape(ns, nb).T.reshape(-1),
        jnp.transpose(idx[n_p:].reshape(n_sg, ds, SUBLANES), (0, 2, 1)).reshape(-1)])
    x1, = _moe_layer(rows0, jnp.concatenate([meta_p, meta_s]), cnt0, order, row(norm_ffn[0]),
                     moe_w_gate[0], moe_w_up[0], moe_w_down[0], row(norm_final), final_norm=False)

    ws_p = sgu_w_s.astype(F32)[:, :CHUNK, :CHUNK]
    bs_p = jnp.repeat(sgu_b_s.astype(F32)[:, :CHUNK].T, SGU_HEAD_DIM, axis=1)
    reps = CHUNK // ds
    ws_s = jnp.tile(sgu_w_s.astype(F32)[:, :ds, :ds], (1, reps, reps))
    bs_s = jnp.tile(jnp.repeat(sgu_b_s.astype(F32)[:, :ds].T, SGU_HEAD_DIM, axis=1), (reps, 1))
    wr1 = _router_weights(moe_router_group[1], moe_router_expert[1])
    sgu = functools.partial(_sgu_layer, g=row(norm_mix[1]), win=sgu_w_in.astype(BF16), ng=row(sgu_norm_g),
                            nb=row(sgu_norm_b), wout=sgu_w_out.astype(BF16), g_ffn=row(norm_ffn[1]),
                            w_router=wr1, n_total_rows=n_tok)
    rows1, meta_p, cnt1 = sgu(x1, ws=ws_p, bs=bs_p, cnt_in=zero_cnt, rows_prev=None, n_blocks=n_p // MIX_ROWS,
                      block_off=0, block_diag=False, emit_v=False)
    rows1, meta_s, cnt1, v_s = sgu(x1, ws=ws_s, bs=bs_s, cnt_in=cnt1, rows_prev=rows1, n_blocks=n_s // MIX_ROWS,
                           block_off=n_p // MIX_ROWS, block_diag=True, emit_v=True)
    y_p, y_s = _moe_layer(rows1, jnp.concatenate([meta_p, meta_s]), cnt1, None, row(norm_ffn[1]),
                          moe_w_gate[1], moe_w_up[1], moe_w_down[1], row(norm_final), final_norm=True,
                          n_first_rows=n_p)
    return (y_p.reshape(nb, ns, dm), y_s.reshape(db, ds, dm), ssm_re_p, ssm_im_p, ssm_re_s, ssm_im_s,
            v_s.reshape(db, ds, SGU_WIDTH))
```

```python
import functools

import jax
import jax.numpy as jnp
from jax import lax
from jax.experimental import pallas as pl
from jax.experimental.pallas import tpu as pltpu

D_MODEL = 1024
SSM_WIDTH = D_MODEL // 2
SSM_GROUP = 16
SSM_GROUPS = SSM_WIDTH // SSM_GROUP
SSM_STATE = 64
SGU_WIDTH = D_MODEL
SGU_HEADS = 8
SGU_HEAD_DIM = SGU_WIDTH // SGU_HEADS
CHUNK = 128
N_EXPERT_GROUPS = 4
EXPERTS_PER_GROUP = 4
N_EXPERTS = N_EXPERT_GROUPS * EXPERTS_PER_GROUP
D_EXPERT = D_MODEL // 4
EPS = 1e-6

SUBLANES = 8
LANES = 128
VMEM_LIMIT_BYTES = 56 * 1024 * 1024
N_DMA_PRIORITIES = 2

SLAB_GROUPS = 16
N_SLABS = SSM_GROUPS // SLAB_GROUPS
SLAB_HALF = SLAB_GROUPS * SSM_STATE
SLAB = 2 * SLAB_HALF
STATE_COLS = N_SLABS * SLAB
SLAB_IN = SLAB_GROUPS * SSM_GROUP
REC_COLS = 512

TOKEN_CHUNKS = D_MODEL // LANES
assert TOKEN_CHUNKS == SUBLANES

PAIR_LO = (0, 0, 0, 1, 1, 2)
PAIR_HI = (1, 2, 3, 2, 3, 3)
N_PAIRS = len(PAIR_LO)
N_BUCKETS = N_EXPERT_GROUPS * N_PAIRS
MOE_TILE = 256
MIX_ROWS = 512
META_ROWS = SUBLANES
META_BUCKET, META_W_LO, META_W_HI, META_RANK = 0, 1, 2, 3
ROUTER_LANES = LANES
BF16 = jnp.bfloat16
F32 = jnp.float32


def _rms(x, g):
    return x * lax.rsqrt(jnp.mean(x * x, axis=-1, keepdims=True) + EPS) * g


def _s5_prep_kernel(lr_ref, li_ref, ls_ref, br_ref, bi_ref, lbr_ref, lbi_ref, bbr_ref, bbi_ref):
    lr = lr_ref[...]
    li = li_ref[...]
    dt = jnp.exp(ls_ref[...])
    mag = jnp.exp(lr * dt)
    ang = li * dt
    lb_re = mag * jnp.cos(ang)
    lb_im = mag * jnp.sin(ang)
    den = lr * lr + li * li
    nr = lb_re - 1.0
    coef_re = (nr * lr + lb_im * li) / den
    coef_im = (lb_im * lr - nr * li) / den
    br = br_ref[...]
    bi = bi_ref[...]
    lbr_ref[...] = lb_re
    lbi_ref[...] = lb_im
    bbr_ref[...] = coef_re * br - coef_im * bi
    bbi_ref[...] = coef_re * bi + coef_im * br


def _s5_prep(lam_re, lam_im, log_step, b_re, b_im):
    rows = SSM_GROUPS * SSM_GROUP
    rep = lambda a: jnp.repeat(a.astype(F32), SSM_GROUP, axis=0)
    lr = rep(lam_re)
    li = rep(lam_im)
    ls = rep(jnp.broadcast_to(log_step.astype(F32)[:, None], (SSM_GROUPS, SSM_STATE)))
    br = jnp.transpose(b_re.astype(F32), (0, 2, 1)).reshape(rows, SSM_STATE)
    bi = jnp.transpose(b_im.astype(F32), (0, 2, 1)).reshape(rows, SSM_STATE)
    sds = jax.ShapeDtypeStruct((rows, SSM_STATE), F32)
    lb_re, lb_im, bb_re, bb_im = pl.pallas_call(
        _s5_prep_kernel, out_shape=(sds, sds, sds, sds), name="s5_prep")(lr, li, ls, br, bi)
    return lb_re, lb_im, bb_re, bb_im


def _to_state_cols(a):
    return a.reshape(a.shape[:-2] + (N_SLABS, SLAB_HALF))


def _pack_state(re, im):
    return jnp.concatenate([_to_state_cols(re), _to_state_cols(im)], axis=-1).reshape(re.shape[0], STATE_COLS)


def _unpack_state(s):
    s = s.reshape(s.shape[0], N_SLABS, 2, SLAB_GROUPS, SSM_STATE)
    re = s[:, :, 0].reshape(s.shape[0], SSM_GROUPS, SSM_STATE)
    im = s[:, :, 1].reshape(s.shape[0], SSM_GROUPS, SSM_STATE)
    return re, im


def _s5_matrices(lb_re, lb_im, bb_re, bb_im, c_re, c_im, d):
    eye = jnp.eye(SLAB_GROUPS, dtype=F32)
    def in_side(bb):
        bb = bb.reshape(N_SLABS, SLAB_GROUPS, SSM_GROUP, SSM_STATE)
        m = bb[:, :, :, None, :] * eye[None, :, None, :, None]
        return m.reshape(N_SLABS, SLAB_IN, SLAB_HALF)
    bmat = jnp.concatenate([in_side(bb_re), in_side(bb_im)], axis=-1).astype(BF16)
    def out_side(c):
        c = jnp.transpose(c.astype(F32), (0, 2, 1)).reshape(N_SLABS, SLAB_GROUPS, SSM_STATE, SSM_GROUP)
        m = c[:, :, :, None, :] * eye[None, :, None, :, None]
        return m.reshape(N_SLABS, SLAB_HALF, SLAB_IN)
    cmat = jnp.concatenate([out_side(c_re), -out_side(c_im)], axis=1).astype(BF16)
    lam = lambda lb: _to_state_cols(lb.reshape(SSM_GROUPS, SSM_GROUP, SSM_STATE)[:, 0])
    a_re = jnp.concatenate([lam(lb_re), lam(lb_re)], axis=-1).reshape(1, STATE_COLS)
    a_im = jnp.concatenate([lam(lb_im), lam(lb_im)], axis=-1).reshape(1, STATE_COLS)
    return bmat, cmat, a_re, a_im, d.astype(F32).reshape(1, SSM_WIDTH)


def _rows_to_tokens(x, tok_ref):
    n = x.shape[0]
    for c in range(TOKEN_CHUNKS):
        tok_ref[pl.ds(c, n, stride=TOKEN_CHUNKS), :] = x[:, c * LANES:(c + 1) * LANES]


def _tokens_to_rows(tok_ref, n):
    return jnp.concatenate([tok_ref[pl.ds(c, n, stride=TOKEN_CHUNKS), :] for c in range(TOKEN_CHUNKS)], axis=-1)


def _start_token_gather(idx_ref, base, src_hbm, buf_ref, sem, n):
    for r in range(n):
        tok = idx_ref[base + r]
        pltpu.make_async_copy(
            src_hbm.at[pl.ds(pl.multiple_of(tok * TOKEN_CHUNKS, TOKEN_CHUNKS), TOKEN_CHUNKS), :],
            buf_ref.at[pl.ds(r * TOKEN_CHUNKS, TOKEN_CHUNKS), :], sem).start(priority=r % N_DMA_PRIORITIES)


def _wait_token_gather(src_hbm, buf_ref, sem):
    pltpu.make_async_copy(src_hbm.at[pl.ds(0, buf_ref.shape[0]), :], buf_ref, sem).wait()


def _gathered_rows(idx_ref, first_block, n_blocks, src_hbm, buf_ref, sems, n):
    i = pl.program_id(0)
    slot = lax.rem(i, 2)

    @pl.when(i == 0)
    def _():
        _start_token_gather(idx_ref, first_block * n, src_hbm, buf_ref.at[0], sems.at[0], n)

    _wait_token_gather(src_hbm, buf_ref.at[slot], sems.at[slot])
    nxt = jnp.minimum(i + 1, n_blocks - 1)
    _start_token_gather(idx_ref, (first_block + nxt) * n, src_hbm, buf_ref.at[1 - slot], sems.at[1 - slot], n)
    x = _tokens_to_rows(buf_ref.at[slot], n)

    @pl.when(i == n_blocks - 1)
    def _():
        _wait_token_gather(src_hbm, buf_ref.at[1 - slot], sems.at[1 - slot])

    return x


def _route_meta(x, g, wr, cnt_ref):
    xn = _rms(x, g)
    xh = xn.astype(BF16)
    xl = (xn - xh.astype(F32)).astype(BF16)
    p = jnp.dot(xh, wr, preferred_element_type=F32)
    q = jnp.dot(xl, wr[:, :ROUTER_LANES], preferred_element_type=F32)
    logits = p[:, :ROUTER_LANES] + (p[:, ROUTER_LANES:] + q)
    rows = logits.shape[0]
    lane = lax.broadcasted_iota(jnp.int32, logits.shape, 1)
    neg = jnp.float32(-jnp.inf)
    big = jnp.int32(ROUTER_LANES)
    is_g = lane < N_EXPERT_GROUPS
    lg = jnp.where(is_g, logits, neg)
    mg = jnp.max(lg, axis=-1, keepdims=True)
    g_idx = jnp.min(jnp.where(lg == mg, lane, big), axis=-1, keepdims=True)
    g_w = 1.0 / jnp.sum(jnp.where(is_g, jnp.exp(lg - mg), 0.0), axis=-1, keepdims=True)
    first = N_EXPERT_GROUPS + g_idx * EXPERTS_PER_GROUP
    in_grp = (lane >= first) & (lane < first + EXPERTS_PER_GROUP)
    le = jnp.where(in_grp, logits, neg)
    m1 = jnp.max(le, axis=-1, keepdims=True)
    i1 = jnp.min(jnp.where(le == m1, lane, big), axis=-1, keepdims=True)
    le2 = jnp.where(lane == i1, neg, le)
    m2 = jnp.max(le2, axis=-1, keepdims=True)
    i2 = jnp.min(jnp.where(le2 == m2, lane, big), axis=-1, keepdims=True)
    e2 = jnp.exp(m2 - m1)
    w1 = g_w / (1.0 + e2)
    w2 = g_w * e2 / (1.0 + e2)
    lo = jnp.minimum(i1, i2) - first
    hi = jnp.maximum(i1, i2) - first
    pair = jnp.where(lo == 0, 0, jnp.where(lo == 1, 3, 5)) + (hi - lo - 1)
    bucket = g_idx * N_PAIRS + pair
    w_lo = jnp.where(i1 < i2, w1, w2)
    w_hi = jnp.where(i1 < i2, w2, w1)
    onehot = lane == bucket
    oh = jnp.where(onehot, 1.0, 0.0)
    r_i = lax.broadcasted_iota(jnp.int32, (rows, rows), 0)
    c_i = lax.broadcasted_iota(jnp.int32, (rows, rows), 1)
    before = jnp.where(r_i > c_i, 1.0, 0.0).astype(BF16)
    prefix = jnp.dot(before, oh.astype(BF16), preferred_element_type=F32)
    cnt = cnt_ref[...]
    rank = jnp.sum(jnp.where(onehot, prefix + cnt, 0.0), axis=-1, keepdims=True)
    cnt_ref[...] = cnt + jnp.sum(oh, axis=0, keepdims=True)
    return jnp.where(lane == META_BUCKET, bucket.astype(F32),
                     jnp.where(lane == META_W_LO, w_lo,
                               jnp.where(lane == META_W_HI, w_hi,
                                         jnp.where(lane == META_RANK, rank, 0.0))))


def _emit_tokens(x, tok_ref, meta_ref, cnt_in_ref, cnt_ref, gf_ref, wr_ref):
    @pl.when(pl.program_id(0) == 0)
    def _():
        cnt_ref[...] = cnt_in_ref[...]
    meta = _route_meta(x, gf_ref[...], wr_ref[...], cnt_ref)
    _rows_to_tokens(x, tok_ref)
    meta_ref[...] = meta.T[:META_ROWS, :]


def _const_spec(*shape):
    return pl.BlockSpec(shape, lambda i, *_: (0,) * len(shape))


_META_SPEC = pl.BlockSpec((META_ROWS, MIX_ROWS), lambda i, *_: (0, i))


def _tokens_alias(tok_prev, input_index):
    if tok_prev is None:
        return {}, jnp.zeros((SUBLANES, LANES), F32)
    return {input_index: 0}, tok_prev


def _tokens_out(n_total_rows, block_off):
    return (pl.BlockSpec((MIX_ROWS * TOKEN_CHUNKS, LANES), lambda i, *_: (i + block_off, 0)),
            jax.ShapeDtypeStruct((n_total_rows * TOKEN_CHUNKS, LANES), F32))


def _s5_kernel(x_ref, g_ref, win_ref, bm_ref, are_ref, aim_ref, cm_ref, d_ref, wout_ref, s0_ref,
               gf_ref, wr_ref, cnt_in_ref, tok_alias_ref,
               tok_ref, meta_ref, so_ref, cnt_ref, bu_ref, st_ref, *, n_seq_groups, steps, carry):
    del tok_alias_ref
    x = x_ref[...]
    h = _rms(x, g_ref[...]).astype(BF16)
    u = jnp.dot(h, win_ref[...], preferred_element_type=F32)
    ub = u.astype(BF16)
    for j in range(N_SLABS):
        bu_ref[:, j * SLAB:(j + 1) * SLAB] = jnp.dot(
            ub[:, j * SLAB_IN:(j + 1) * SLAB_IN], bm_ref[j], preferred_element_type=F32)

    if carry:
        @pl.when(pl.program_id(0) == 0)
        def _():
            st_ref[...] = s0_ref[0]

    for sg in range(n_seq_groups):
        for j in range(N_SLABS):
            for k in range(SLAB_HALF // REC_COLS):
                re0 = j * SLAB + k * REC_COLS
                im0 = re0 + SLAB_HALF
                ar = jnp.broadcast_to(are_ref[:, re0:re0 + REC_COLS], (SUBLANES, REC_COLS))
                ai = jnp.broadcast_to(aim_ref[:, re0:re0 + REC_COLS], (SUBLANES, REC_COLS))
                if carry:
                    sr = st_ref[:, re0:re0 + REC_COLS]
                    si = st_ref[:, im0:im0 + REC_COLS]
                else:
                    sr = s0_ref[sg, :, re0:re0 + REC_COLS]
                    si = s0_ref[sg, :, im0:im0 + REC_COLS]

                def step(t, c, re0=re0, im0=im0, ar=ar, ai=ai, sg=sg):
                    sr, si = c
                    row = pl.multiple_of((sg * steps + t) * SUBLANES, SUBLANES)
                    br = bu_ref[pl.ds(row, SUBLANES), re0:re0 + REC_COLS]
                    bi = bu_ref[pl.ds(row, SUBLANES), im0:im0 + REC_COLS]
                    nr = ar * sr - ai * si + br
                    ni = ar * si + ai * sr + bi
                    bu_ref[pl.ds(row, SUBLANES), re0:re0 + REC_COLS] = nr
                    bu_ref[pl.ds(row, SUBLANES), im0:im0 + REC_COLS] = ni
                    return nr, ni

                sr, si = lax.fori_loop(0, steps, step, (sr, si), unroll=min(steps, 8))
                if carry:
                    st_ref[:, re0:re0 + REC_COLS] = sr
                    st_ref[:, im0:im0 + REC_COLS] = si
                    so_ref[0, :, re0:re0 + REC_COLS] = sr
                    so_ref[0, :, im0:im0 + REC_COLS] = si
                else:
                    so_ref[sg, :, re0:re0 + REC_COLS] = sr
                    so_ref[sg, :, im0:im0 + REC_COLS] = si

    ys = [jnp.dot(bu_ref[:, j * SLAB:(j + 1) * SLAB].astype(BF16), cm_ref[j], preferred_element_type=F32)
          for j in range(N_SLABS)]
    y = jnp.concatenate(ys, axis=-1) + d_ref[...] * u
    y = jax.nn.gelu(y).astype(BF16)
    a = jnp.dot(y, wout_ref[...], preferred_element_type=F32)
    xo = x + a[:, :D_MODEL] * jax.nn.sigmoid(a[:, D_MODEL:])
    _emit_tokens(xo, tok_ref, meta_ref, cnt_in_ref, cnt_ref, gf_ref, wr_ref)


def _s5_layer(x_rows, s0, g, win, bmat, a_re, a_im, cmat, dvec, wout, g_ffn, w_router, cnt_in, tok_prev,
              *, n_total_rows, block_off, n_seq_groups, steps, carry):
    assert n_seq_groups * steps * SUBLANES == MIX_ROWS
    n_blocks = x_rows.shape[0] // MIX_ROWS
    s_idx = (lambda i: (0, 0, 0)) if carry else (lambda i: (i, 0, 0))
    kern = functools.partial(_s5_kernel, n_seq_groups=n_seq_groups, steps=steps, carry=carry)
    n_state_groups = 1 if carry else n_blocks * n_seq_groups
    tok_spec, tok_shape = _tokens_out(n_total_rows, block_off)
    aliases, tok_prev = _tokens_alias(tok_prev, 13)
    inputs = [x_rows, g, win, bmat, a_re, a_im, cmat, dvec, wout, s0, g_ffn, w_router, cnt_in, tok_prev]
    return pl.pallas_call(
        kern,
        grid=(n_blocks,),
        in_specs=[
            pl.BlockSpec((MIX_ROWS, D_MODEL), lambda i: (i, 0)),
            _const_spec(1, D_MODEL), _const_spec(D_MODEL, SSM_WIDTH), _const_spec(N_SLABS, SLAB_IN, SLAB),
            _const_spec(1, STATE_COLS), _const_spec(1, STATE_COLS), _const_spec(N_SLABS, SLAB, SLAB_IN),
            _const_spec(1, SSM_WIDTH), _const_spec(SSM_WIDTH, 2 * D_MODEL),
            pl.BlockSpec((n_seq_groups, SUBLANES, STATE_COLS), s_idx),
            _const_spec(1, D_MODEL), _const_spec(D_MODEL, 2 * ROUTER_LANES), _const_spec(1, LANES),
            pl.BlockSpec(memory_space=pl.ANY),
        ],
        out_specs=[
            tok_spec, _META_SPEC,
            pl.BlockSpec((n_seq_groups, SUBLANES, STATE_COLS), s_idx),
            _const_spec(1, LANES),
        ],
        out_shape=[
            tok_shape, jax.ShapeDtypeStruct((META_ROWS, n_blocks * MIX_ROWS), F32),
            jax.ShapeDtypeStruct((n_state_groups, SUBLANES, STATE_COLS), F32),
            jax.ShapeDtypeStruct((1, LANES), F32),
        ],
        input_output_aliases=aliases,
        scratch_shapes=[pltpu.VMEM((MIX_ROWS, STATE_COLS), F32), pltpu.VMEM((SUBLANES, STATE_COLS), F32)],
        compiler_params=pltpu.CompilerParams(
            dimension_semantics=("arbitrary",), vmem_limit_bytes=VMEM_LIMIT_BYTES),
        name="s5_carry" if carry else "s5_step",
    )(*inputs)


def _sgu_kernel(pos_ref, src_hbm, g_ref, win_ref, ng_ref, nb_ref, ws_ref, bs_ref, wout_ref,
                gf_ref, wr_ref, cnt_in_ref, tok_alias_ref, *refs, n_blocks, block_off, block_diag, emit_v):
    del tok_alias_ref
    tok_ref, meta_ref, cnt_ref = refs[:3]
    v_ref = refs[3] if emit_v else None
    xbuf, sems = refs[-2:]
    x = _gathered_rows(pos_ref, block_off, n_blocks, src_hbm, xbuf, sems, MIX_ROWS)
    rows = x.shape[0]
    h = _rms(x, g_ref[...]).astype(BF16)
    hh = jax.nn.gelu(jnp.dot(h, win_ref[...], preferred_element_type=F32))
    u = hh[:, :SGU_WIDTH]
    v = hh[:, SGU_WIDTH:]
    vc = v - jnp.mean(v, axis=-1, keepdims=True)
    v = vc * lax.rsqrt(jnp.mean(vc * vc, axis=-1, keepdims=True) + EPS) * ng_ref[...] + nb_ref[...]
    if emit_v:
        v_ref[...] = v
    r = lax.broadcasted_iota(jnp.int32, (CHUNK, CHUNK), 0)
    c = lax.broadcasted_iota(jnp.int32, (CHUNK, CHUNK), 1)
    if block_diag:
        keep = (r // SUBLANES == c // SUBLANES) & (r >= c)
    else:
        keep = r >= c
    vb = v.astype(BF16)
    n_chunks = rows // CHUNK
    zs = []
    for hd in range(SGU_HEADS):
        wm = jnp.where(keep, ws_ref[hd], 0.0).astype(BF16)
        cols = slice(hd * SGU_HEAD_DIM, (hd + 1) * SGU_HEAD_DIM)
        vcat = jnp.concatenate([vb[ck * CHUNK:(ck + 1) * CHUNK, cols] for ck in range(n_chunks)], axis=-1)
        zcat = jnp.dot(wm, vcat, preferred_element_type=F32)
        zs.append([zcat[:, ck * SGU_HEAD_DIM:(ck + 1) * SGU_HEAD_DIM] for ck in range(n_chunks)])
    z = jnp.concatenate(
        [jnp.concatenate([zs[hd][ck] for hd in range(SGU_HEADS)], axis=-1) + bs_ref[...]
         for ck in range(n_chunks)], axis=0)
    o = jnp.dot((u * z).astype(BF16), wout_ref[...], preferred_element_type=F32)
    _emit_tokens(x + o, tok_ref, meta_ref, cnt_in_ref, cnt_ref, gf_ref, wr_ref)


def _sgu_layer(pos, src_tokens, g, win, ng, nb, ws, bs, wout, g_ffn, w_router, cnt_in, tok_prev,
               *, n_total_rows, n_blocks, block_off, block_diag, emit_v):
    tok_spec, tok_shape = _tokens_out(n_total_rows, block_off)
    out_shape = [tok_shape, jax.ShapeDtypeStruct((META_ROWS, n_blocks * MIX_ROWS), F32),
                 jax.ShapeDtypeStruct((1, LANES), F32)]
    out_specs = [tok_spec, _META_SPEC, _const_spec(1, LANES)]
    if emit_v:
        out_shape.append(jax.ShapeDtypeStruct((n_blocks * MIX_ROWS, SGU_WIDTH), F32))
        out_specs.append(pl.BlockSpec((MIX_ROWS, SGU_WIDTH), lambda i, *_: (i, 0)))
    aliases, tok_prev = _tokens_alias(tok_prev, 12)
    inputs = [src_tokens, g, win, ng, nb, ws, bs, wout, g_ffn, w_router, cnt_in, tok_prev]
    return pl.pallas_call(
        functools.partial(_sgu_kernel, n_blocks=n_blocks, block_off=block_off, block_diag=block_diag,
                          emit_v=emit_v),
        grid_spec=pltpu.PrefetchScalarGridSpec(
            num_scalar_prefetch=1,
            grid=(n_blocks,),
            in_specs=[pl.BlockSpec(memory_space=pl.ANY), _const_spec(1, D_MODEL),
                      _const_spec(D_MODEL, 2 * SGU_WIDTH), _const_spec(1, SGU_WIDTH), _const_spec(1, SGU_WIDTH),
                      _const_spec(SGU_HEADS, CHUNK, CHUNK), _const_spec(CHUNK, SGU_WIDTH),
                      _const_spec(SGU_WIDTH, D_MODEL), _const_spec(1, D_MODEL),
                      _const_spec(D_MODEL, 2 * ROUTER_LANES), _const_spec(1, LANES),
                      pl.BlockSpec(memory_space=pl.ANY)],
            out_specs=out_specs,
            scratch_shapes=[pltpu.VMEM((2, MIX_ROWS * TOKEN_CHUNKS, LANES), F32), pltpu.SemaphoreType.DMA((2,))],
        ),
        out_shape=out_shape,
        input_output_aliases=aliases,
        compiler_params=pltpu.CompilerParams(
            dimension_semantics=("arbitrary",), vmem_limit_bytes=VMEM_LIMIT_BYTES),
        name="sgu_block_diag" if block_diag else "sgu_chunked",
    )(pos, *inputs)


def _moe_plan(meta, cnt, n_tiles_max):
    i32 = jnp.int32
    n_tok = meta.shape[1]
    bucket = meta[META_BUCKET].astype(i32)
    rank = meta[META_RANK].astype(i32)
    n = cnt[0, :N_BUCKETS].astype(i32)
    nt = (n + MOE_TILE - 1) // MOE_TILE
    tile_end = jnp.cumsum(nt)
    tile_start = tile_end - nt
    pos = tile_start[bucket] * MOE_TILE + rank
    n_tiles = tile_end[-1]
    token_of_slot = jnp.zeros((n_tiles_max * MOE_TILE,), i32).at[pos].set(jnp.arange(n_tok, dtype=i32))
    gates = jnp.zeros((META_ROWS, n_tiles_max * MOE_TILE), F32).at[:2].set(
        meta[META_W_LO:META_W_HI + 1][:, token_of_slot])
    j = jnp.minimum(jnp.arange(n_tiles_max, dtype=i32), jnp.maximum(n_tiles - 1, 0))
    tb = jnp.minimum(jnp.sum((j[:, None] >= tile_end[None, :]).astype(i32), axis=1), N_BUCKETS - 1)
    grp = tb // N_PAIRS
    pair = tb % N_PAIRS
    e_lo = grp * EXPERTS_PER_GROUP + jnp.asarray(PAIR_LO, i32)[pair]
    e_hi = grp * EXPERTS_PER_GROUP + jnp.asarray(PAIR_HI, i32)[pair]
    return pos, token_of_slot, gates, e_lo, e_hi, n_tiles.reshape(1)


def _moe_kernel(tok_of_slot_ref, elo_ref, ehi_ref, nt_ref, src_hbm, gates_ref, g_ref,
                wg_lo, wu_lo, wd_lo, wg_hi, wu_hi, wd_hi, gf_ref, ys_ref, xbuf, wgu_ref, wd_ref, sems,
                *, final_norm):
    j = pl.program_id(0)
    n_tiles = nt_ref[0]
    slot = lax.rem(j, 2)

    @pl.when(j == 0)
    def _():
        _start_token_gather(tok_of_slot_ref, 0, src_hbm, xbuf.at[0], sems.at[0], MOE_TILE)

    @pl.when(j < n_tiles)
    def _():
        prev = jnp.maximum(j - 1, 0)
        for s, (e_ref, wg, wu, wd) in enumerate(((elo_ref, wg_lo, wu_lo, wd_lo), (ehi_ref, wg_hi, wu_hi, wd_hi))):
            @pl.when((j == 0) | (e_ref[j] != e_ref[prev]))
            def _():
                wgu_ref[s, :, :D_EXPERT] = wg[0].astype(BF16)
                wgu_ref[s, :, D_EXPERT:] = wu[0].astype(BF16)
                wd_ref[s] = wd[0].astype(BF16)

        _wait_token_gather(src_hbm, xbuf.at[slot], sems.at[slot])
        nxt = jnp.minimum(j + 1, n_tiles - 1)
        _start_token_gather(tok_of_slot_ref, nxt * MOE_TILE, src_hbm, xbuf.at[1 - slot], sems.at[1 - slot],
                            MOE_TILE)
        x = _tokens_to_rows(xbuf.at[slot], MOE_TILE)
        xn = _rms(x, g_ref[...]).astype(BF16)
        gates = jnp.concatenate(
            [gates_ref[...], jnp.zeros((LANES - META_ROWS, MOE_TILE), F32)], axis=0).T
        out = None
        for s in range(2):
            gu = jnp.dot(xn, wgu_ref[s], preferred_element_type=F32)
            hcur = (jax.nn.silu(gu[:, :D_EXPERT]) * gu[:, D_EXPERT:] * gates[:, s:s + 1]).astype(BF16)
            o = jnp.dot(hcur, wd_ref[s], preferred_element_type=F32)
            out = o if out is None else out + o
        y = x + out
        if final_norm:
            y = _rms(y, gf_ref[...])
        _rows_to_tokens(y, ys_ref)

        @pl.when(j == n_tiles - 1)
        def _():
            _wait_token_gather(src_hbm, xbuf.at[1 - slot], sems.at[1 - slot])


def _moe_experts(tokens, token_of_slot, gates, e_lo, e_hi, n_tiles, g, w_gate, w_up, w_down, g_final,
                 *, final_norm):
    n_tiles_max = e_lo.shape[0]
    last = lambda j, nt: jnp.minimum(j, jnp.maximum(nt[0] - 1, 0))
    w_in = lambda which: pl.BlockSpec(
        (1, D_MODEL, D_EXPERT), lambda j, tos, elo, ehi, nt: ((elo, ehi)[which][j], 0, 0))
    w_out = lambda which: pl.BlockSpec(
        (1, D_EXPERT, D_MODEL), lambda j, tos, elo, ehi, nt: ((elo, ehi)[which][j], 0, 0))
    return pl.pallas_call(
        functools.partial(_moe_kernel, final_norm=final_norm),
        grid_spec=pltpu.PrefetchScalarGridSpec(
            num_scalar_prefetch=4,
            grid=(n_tiles_max,),
            in_specs=[pl.BlockSpec(memory_space=pl.ANY),
                      pl.BlockSpec((META_ROWS, MOE_TILE), lambda j, tos, elo, ehi, nt: (0, last(j, nt))),
                      _const_spec(1, D_MODEL), w_in(0), w_in(0), w_out(0), w_in(1), w_in(1), w_out(1),
                      _const_spec(1, D_MODEL)],
            out_specs=pl.BlockSpec((MOE_TILE * TOKEN_CHUNKS, LANES),
                                   lambda j, tos, elo, ehi, nt: (last(j, nt), 0)),
            scratch_shapes=[pltpu.VMEM((2, MOE_TILE * TOKEN_CHUNKS, LANES), F32),
                            pltpu.VMEM((2, D_MODEL, 2 * D_EXPERT), BF16), pltpu.VMEM((2, D_EXPERT, D_MODEL), BF16),
                            pltpu.SemaphoreType.DMA((2,))],
        ),
        out_shape=jax.ShapeDtypeStruct((n_tiles_max * MOE_TILE * TOKEN_CHUNKS, LANES), F32),
        compiler_params=pltpu.CompilerParams(
            dimension_semantics=("arbitrary",), vmem_limit_bytes=VMEM_LIMIT_BYTES),
        name="moe_experts_final" if final_norm else "moe_experts",
    )(token_of_slot, e_lo, e_hi, n_tiles, tokens, gates, g, w_gate, w_up, w_down, w_gate, w_up, w_down, g_final)


def _moe_layer(tokens, meta, cnt, g, w_gate, w_up, w_down, g_final, *, final_norm):
    n_tiles_max = meta.shape[1] // MOE_TILE + N_BUCKETS
    pos, token_of_slot, gates, e_lo, e_hi, n_tiles = _moe_plan(meta, cnt, n_tiles_max)
    ys = _moe_experts(tokens, token_of_slot, gates, e_lo, e_hi, n_tiles, g, w_gate, w_up, w_down, g_final,
                      final_norm=final_norm)
    return ys, pos


def _ungather_kernel(pos_ref, src_hbm, *refs, n_blocks, n_first_blocks):
    o_refs, (xbuf, sems) = refs[:-2], refs[-2:]
    i = pl.program_id(0)
    x = _gathered_rows(pos_ref, 0, n_blocks, src_hbm, xbuf, sems, MIX_ROWS)

    @pl.when(i < n_first_blocks)
    def _():
        o_refs[0][...] = x

    @pl.when(i >= n_first_blocks)
    def _():
        o_refs[1][...] = x


def _ungather(ys, pos, n_first_rows):
    n_blocks = pos.shape[0] // MIX_ROWS
    n_first = n_first_rows // MIX_ROWS
    spec = lambda f: pl.BlockSpec((MIX_ROWS, D_MODEL), f)
    return pl.pallas_call(
        functools.partial(_ungather_kernel, n_blocks=n_blocks, n_first_blocks=n_first),
        grid_spec=pltpu.PrefetchScalarGridSpec(
            num_scalar_prefetch=1,
            grid=(n_blocks,),
            in_specs=[pl.BlockSpec(memory_space=pl.ANY)],
            out_specs=[spec(lambda i, pos: (jnp.minimum(i, n_first - 1), 0)),
                       spec(lambda i, pos: (jnp.maximum(i - n_first, 0), 0))],
            scratch_shapes=[pltpu.VMEM((2, MIX_ROWS * TOKEN_CHUNKS, LANES), F32), pltpu.SemaphoreType.DMA((2,))],
        ),
        out_shape=[jax.ShapeDtypeStruct((n_first_rows, D_MODEL), F32),
                   jax.ShapeDtypeStruct((pos.shape[0] - n_first_rows, D_MODEL), F32)],
        compiler_params=pltpu.CompilerParams(
            dimension_semantics=("arbitrary",), vmem_limit_bytes=VMEM_LIMIT_BYTES),
        name="moe_ungather",
    )(pos, ys)


def _router_weights(w_rg, w_re):
    pad = ROUTER_LANES - N_EXPERT_GROUPS - N_EXPERTS
    w = jnp.concatenate([w_rg.astype(F32), w_re.astype(F32), jnp.zeros((D_MODEL, pad), F32)], axis=-1)
    hi = w.astype(BF16)
    lo = (w - hi.astype(F32)).astype(BF16)
    return jnp.concatenate([hi, lo], axis=-1)


def kernel(x_prompt, x_sample, state_ssm_re, state_ssm_im, norm_mix, norm_ffn, norm_final, ssm_w_in, ssm_lambda_re, ssm_lambda_im, ssm_log_step, ssm_b_re, ssm_b_im, ssm_c_re, ssm_c_im, ssm_d, ssm_w_out, sgu_w_in, sgu_norm_g, sgu_norm_b, sgu_w_s, sgu_b_s, sgu_w_out, moe_router_group, moe_router_expert, moe_w_gate, moe_w_up, moe_w_down):
    nb, ns, dm = x_prompt.shape
    db, ds, _ = x_sample.shape
    n_p, n_s = nb * ns, db * ds
    n_tok = n_p + n_s
    n_sg = db // SUBLANES
    steps_p = MIX_ROWS // SUBLANES
    sg_per_block = MIX_ROWS // (ds * SUBLANES)
    assert nb == SUBLANES and dm == D_MODEL and ns % steps_p == 0 and n_sg % sg_per_block == 0
    assert n_p % MIX_ROWS == 0 and n_s % MIX_ROWS == 0 and MIX_ROWS % MOE_TILE == 0 and CHUNK % ds == 0
    row = lambda a: a.astype(F32).reshape(1, -1)
    zero_cnt = jnp.zeros((1, LANES), F32)

    lb_re, lb_im, bb_re, bb_im = _s5_prep(ssm_lambda_re, ssm_lambda_im, ssm_log_step, ssm_b_re, ssm_b_im)
    bmat, cmat, a_re, a_im, dvec = _s5_matrices(lb_re, lb_im, bb_re, bb_im, ssm_c_re, ssm_c_im, ssm_d)
    xp = jnp.transpose(x_prompt, (1, 0, 2)).reshape(n_p, dm)
    xs = jnp.transpose(x_sample.reshape(n_sg, SUBLANES, ds, dm), (0, 2, 1, 3)).reshape(n_s, dm)
    zero_state = jnp.zeros((1, SUBLANES, STATE_COLS), F32)
    s0 = _pack_state(state_ssm_re.astype(F32), state_ssm_im.astype(F32)).reshape(n_sg, SUBLANES, STATE_COLS)
    wr0 = _router_weights(moe_router_group[0], moe_router_expert[0])
    s5 = functools.partial(_s5_layer, g=row(norm_mix[0]), win=ssm_w_in.astype(BF16), bmat=bmat, a_re=a_re,
                           a_im=a_im, cmat=cmat, dvec=dvec, wout=ssm_w_out.astype(BF16),
                           g_ffn=row(norm_ffn[0]), w_router=wr0, n_total_rows=n_tok)
    tok0, meta_p, st_p, cnt0 = s5(xp, zero_state, cnt_in=zero_cnt, tok_prev=None, block_off=0,
                                  n_seq_groups=1, steps=steps_p, carry=True)
    tok0, meta_s, st_s, cnt0 = s5(xs, s0, cnt_in=cnt0, tok_prev=tok0, block_off=n_p // MIX_ROWS,
                                  n_seq_groups=sg_per_block, steps=ds, carry=False)
    ssm_re_p, ssm_im_p = _unpack_state(st_p.reshape(nb, STATE_COLS))
    ssm_re_s, ssm_im_s = _unpack_state(st_s.reshape(db, STATE_COLS))
    ys0, pos0 = _moe_layer(tok0, jnp.concatenate([meta_p, meta_s], axis=1), cnt0, row(norm_ffn[0]),
                           moe_w_gate[0], moe_w_up[0], moe_w_down[0], row(norm_final), final_norm=False)

    idx = jnp.arange(n_tok, dtype=jnp.int32)
    order = jnp.concatenate([
        idx[:n_p].reshape(ns, nb).T.reshape(-1),
        jnp.transpose(idx[n_p:].reshape(n_sg, ds, SUBLANES), (0, 2, 1)).reshape(-1)])
    pos0 = pos0[order]
    ws_p = sgu_w_s.astype(F32)[:, :CHUNK, :CHUNK]
    bs_p = jnp.repeat(sgu_b_s.astype(F32)[:, :CHUNK].T, SGU_HEAD_DIM, axis=1)
    reps = CHUNK // ds
    ws_s = jnp.tile(sgu_w_s.astype(F32)[:, :ds, :ds], (1, reps, reps))
    bs_s = jnp.tile(jnp.repeat(sgu_b_s.astype(F32)[:, :ds].T, SGU_HEAD_DIM, axis=1), (reps, 1))
    wr1 = _router_weights(moe_router_group[1], moe_router_expert[1])
    sgu = functools.partial(_sgu_layer, pos0, ys0, g=row(norm_mix[1]), win=sgu_w_in.astype(BF16),
                            ng=row(sgu_norm_g), nb=row(sgu_norm_b), wout=sgu_w_out.astype(BF16),
                            g_ffn=row(norm_ffn[1]), w_router=wr1, n_total_rows=n_tok)
    tok1, meta_p, cnt1 = sgu(ws=ws_p, bs=bs_p, cnt_in=zero_cnt, tok_prev=None, n_blocks=n_p // MIX_ROWS,
                             block_off=0, block_diag=False, emit_v=False)
    tok1, meta_s, cnt1, v_s = sgu(ws=ws_s, bs=bs_s, cnt_in=cnt1, tok_prev=tok1, n_blocks=n_s // MIX_ROWS,
                                  block_off=n_p // MIX_ROWS, block_diag=True, emit_v=True)
    ys1, pos1 = _moe_layer(tok1, jnp.concatenate([meta_p, meta_s], axis=1), cnt1, row(norm_ffn[1]),
                           moe_w_gate[1], moe_w_up[1], moe_w_down[1], row(norm_final), final_norm=True)
    y_p, y_s = _ungather(ys1, pos1, n_p)
    return (y_p.reshape(nb, ns, dm), y_s.reshape(db, ds, dm), ssm_re_p, ssm_im_p, ssm_re_s, ssm_im_s,
            v_s.reshape(db, ds, SGU_WIDTH))
```

```python
import functools

import jax
import jax.numpy as jnp
from jax import lax
from jax.experimental import pallas as pl
from jax.experimental.pallas import tpu as pltpu

D_MODEL = 1024
SSM_WIDTH = D_MODEL // 2
SSM_GROUP = 16
SSM_GROUPS = SSM_WIDTH // SSM_GROUP
SSM_STATE = 64
SGU_WIDTH = D_MODEL
SGU_HEADS = 8
SGU_HEAD_DIM = SGU_WIDTH // SGU_HEADS
CHUNK = 128
N_EXPERT_GROUPS = 4
EXPERTS_PER_GROUP = 4
N_EXPERTS = N_EXPERT_GROUPS * EXPERTS_PER_GROUP
D_EXPERT = D_MODEL // 4
EPS = 1e-6

SUBLANES = 8
LANES = 128
VMEM_LIMIT_BYTES = 56 * 1024 * 1024
N_DMA_PRIORITIES = 2

SLAB_GROUPS = 16
N_SLABS = SSM_GROUPS // SLAB_GROUPS
SLAB_HALF = SLAB_GROUPS * SSM_STATE
SLAB = 2 * SLAB_HALF
STATE_COLS = N_SLABS * SLAB
SLAB_IN = SLAB_GROUPS * SSM_GROUP
REC_COLS = 512

TOKEN_CHUNKS = D_MODEL // LANES
assert TOKEN_CHUNKS == SUBLANES

PAIR_A = (0, 0, 0, 1, 1, 3)
PAIR_B = (1, 2, 3, 3, 2, 2)
N_PAIRS = len(PAIR_A)
GATHER_DEPTH = 3
N_BUCKETS = N_EXPERT_GROUPS * N_PAIRS
MOE_TILE = 256
MIX_ROWS = 512
META_ROWS = SUBLANES
META_BUCKET, META_W_A, META_W_B, META_RANK = 0, 1, 2, 3
ROUTER_LANES = LANES
BF16 = jnp.bfloat16
F32 = jnp.float32


def _rms(x, g):
    return x * lax.rsqrt(jnp.mean(x * x, axis=-1, keepdims=True) + EPS) * g


def _s5_prep_kernel(lr_ref, li_ref, ls_ref, br_ref, bi_ref, lbr_ref, lbi_ref, bbr_ref, bbi_ref):
    lr = lr_ref[...]
    li = li_ref[...]
    dt = jnp.exp(ls_ref[...])
    mag = jnp.exp(lr * dt)
    ang = li * dt
    lb_re = mag * jnp.cos(ang)
    lb_im = mag * jnp.sin(ang)
    den = lr * lr + li * li
    nr = lb_re - 1.0
    coef_re = (nr * lr + lb_im * li) / den
    coef_im = (lb_im * lr - nr * li) / den
    br = br_ref[...]
    bi = bi_ref[...]
    lbr_ref[...] = lb_re
    lbi_ref[...] = lb_im
    bbr_ref[...] = coef_re * br - coef_im * bi
    bbi_ref[...] = coef_re * bi + coef_im * br


def _s5_prep(lam_re, lam_im, log_step, b_re, b_im):
    rows = SSM_GROUPS * SSM_GROUP
    rep = lambda a: jnp.repeat(a.astype(F32), SSM_GROUP, axis=0)
    lr = rep(lam_re)
    li = rep(lam_im)
    ls = rep(jnp.broadcast_to(log_step.astype(F32)[:, None], (SSM_GROUPS, SSM_STATE)))
    br = jnp.transpose(b_re.astype(F32), (0, 2, 1)).reshape(rows, SSM_STATE)
    bi = jnp.transpose(b_im.astype(F32), (0, 2, 1)).reshape(rows, SSM_STATE)
    sds = jax.ShapeDtypeStruct((rows, SSM_STATE), F32)
    lb_re, lb_im, bb_re, bb_im = pl.pallas_call(
        _s5_prep_kernel, out_shape=(sds, sds, sds, sds), name="s5_prep")(lr, li, ls, br, bi)
    return lb_re, lb_im, bb_re, bb_im


def _to_state_cols(a):
    return a.reshape(a.shape[:-2] + (N_SLABS, SLAB_HALF))


def _pack_state(re, im):
    return jnp.concatenate([_to_state_cols(re), _to_state_cols(im)], axis=-1).reshape(re.shape[0], STATE_COLS)


def _unpack_state(s):
    s = s.reshape(s.shape[0], N_SLABS, 2, SLAB_GROUPS, SSM_STATE)
    re = s[:, :, 0].reshape(s.shape[0], SSM_GROUPS, SSM_STATE)
    im = s[:, :, 1].reshape(s.shape[0], SSM_GROUPS, SSM_STATE)
    return re, im


def _s5_matrices(lb_re, lb_im, bb_re, bb_im, c_re, c_im, d):
    eye = jnp.eye(SLAB_GROUPS, dtype=F32)
    def in_side(bb):
        bb = bb.reshape(N_SLABS, SLAB_GROUPS, SSM_GROUP, SSM_STATE)
        m = bb[:, :, :, None, :] * eye[None, :, None, :, None]
        return m.reshape(N_SLABS, SLAB_IN, SLAB_HALF)
    bmat = jnp.concatenate([in_side(bb_re), in_side(bb_im)], axis=-1).astype(BF16)
    def out_side(c):
        c = jnp.transpose(c.astype(F32), (0, 2, 1)).reshape(N_SLABS, SLAB_GROUPS, SSM_STATE, SSM_GROUP)
        m = c[:, :, :, None, :] * eye[None, :, None, :, None]
        return m.reshape(N_SLABS, SLAB_HALF, SLAB_IN)
    cmat = jnp.concatenate([out_side(c_re), -out_side(c_im)], axis=1).astype(BF16)
    lam = lambda lb: _to_state_cols(lb.reshape(SSM_GROUPS, SSM_GROUP, SSM_STATE)[:, 0])
    a_re = jnp.concatenate([lam(lb_re), lam(lb_re)], axis=-1).reshape(1, STATE_COLS)
    a_im = jnp.concatenate([lam(lb_im), lam(lb_im)], axis=-1).reshape(1, STATE_COLS)
    return bmat, cmat, a_re, a_im, d.astype(F32).reshape(1, SSM_WIDTH)


def _rows_to_tokens(x, tok_ref):
    n = x.shape[0]
    for c in range(TOKEN_CHUNKS):
        tok_ref[pl.ds(c, n, stride=TOKEN_CHUNKS), :] = x[:, c * LANES:(c + 1) * LANES]


def _tokens_to_rows(tok_ref, n):
    return jnp.concatenate([tok_ref[pl.ds(c, n, stride=TOKEN_CHUNKS), :] for c in range(TOKEN_CHUNKS)], axis=-1)


def _start_token_gather(idx_ref, base, src_hbm, buf_ref, sem, n):
    for r in range(n):
        tok = idx_ref[base + r]
        pltpu.make_async_copy(
            src_hbm.at[pl.ds(pl.multiple_of(tok * TOKEN_CHUNKS, TOKEN_CHUNKS), TOKEN_CHUNKS), :],
            buf_ref.at[pl.ds(r * TOKEN_CHUNKS, TOKEN_CHUNKS), :], sem).start(priority=r % N_DMA_PRIORITIES)


def _wait_token_gather(src_hbm, buf_ref, sem):
    pltpu.make_async_copy(src_hbm.at[pl.ds(0, buf_ref.shape[0]), :], buf_ref, sem).wait()


def _gather_prologue(idx_ref, first_block, n_blocks, src_hbm, buf_ref, sems, n):
    for k in range(GATHER_DEPTH - 1):
        blk = first_block + jnp.minimum(k, n_blocks - 1)
        _start_token_gather(idx_ref, blk * n, src_hbm, buf_ref.at[k], sems.at[k], n)


def _gather_step(i, idx_ref, first_block, n_blocks, src_hbm, buf_ref, sems, n):
    slot = lax.rem(i, GATHER_DEPTH)
    ahead = lax.rem(i + GATHER_DEPTH - 1, GATHER_DEPTH)
    _wait_token_gather(src_hbm, buf_ref.at[slot], sems.at[slot])
    blk = first_block + jnp.minimum(i + GATHER_DEPTH - 1, n_blocks - 1)
    _start_token_gather(idx_ref, blk * n, src_hbm, buf_ref.at[ahead], sems.at[ahead], n)
    return _tokens_to_rows(buf_ref.at[slot], n)


def _gather_epilogue(i, src_hbm, buf_ref, sems):
    for k in range(1, GATHER_DEPTH):
        slot = lax.rem(i + k, GATHER_DEPTH)
        _wait_token_gather(src_hbm, buf_ref.at[slot], sems.at[slot])


def _gathered_rows(idx_ref, first_block, n_blocks, src_hbm, buf_ref, sems, n):
    i = pl.program_id(0)

    @pl.when(i == 0)
    def _():
        _gather_prologue(idx_ref, first_block, n_blocks, src_hbm, buf_ref, sems, n)

    x = _gather_step(i, idx_ref, first_block, n_blocks, src_hbm, buf_ref, sems, n)

    @pl.when(i == n_blocks - 1)
    def _():
        _gather_epilogue(i, src_hbm, buf_ref, sems)

    return x


def _route_meta(x, g, wr, cnt_ref):
    xn = _rms(x, g)
    xh = xn.astype(BF16)
    xl = (xn - xh.astype(F32)).astype(BF16)
    p = jnp.dot(xh, wr, preferred_element_type=F32)
    q = jnp.dot(xl, wr[:, :ROUTER_LANES], preferred_element_type=F32)
    logits = p[:, :ROUTER_LANES] + (p[:, ROUTER_LANES:] + q)
    rows = logits.shape[0]
    lane = lax.broadcasted_iota(jnp.int32, logits.shape, 1)
    neg = jnp.float32(-jnp.inf)
    big = jnp.int32(ROUTER_LANES)
    is_g = lane < N_EXPERT_GROUPS
    lg = jnp.where(is_g, logits, neg)
    mg = jnp.max(lg, axis=-1, keepdims=True)
    g_idx = jnp.min(jnp.where(lg == mg, lane, big), axis=-1, keepdims=True)
    g_w = 1.0 / jnp.sum(jnp.where(is_g, jnp.exp(lg - mg), 0.0), axis=-1, keepdims=True)
    first = N_EXPERT_GROUPS + g_idx * EXPERTS_PER_GROUP
    in_grp = (lane >= first) & (lane < first + EXPERTS_PER_GROUP)
    le = jnp.where(in_grp, logits, neg)
    m1 = jnp.max(le, axis=-1, keepdims=True)
    i1 = jnp.min(jnp.where(le == m1, lane, big), axis=-1, keepdims=True)
    le2 = jnp.where(lane == i1, neg, le)
    m2 = jnp.max(le2, axis=-1, keepdims=True)
    i2 = jnp.min(jnp.where(le2 == m2, lane, big), axis=-1, keepdims=True)
    e2 = jnp.exp(m2 - m1)
    w1 = g_w / (1.0 + e2)
    w2 = g_w * e2 / (1.0 + e2)
    lo = jnp.minimum(i1, i2) - first
    hi = jnp.maximum(i1, i2) - first
    pair = jnp.where(lo == 0, hi - 1, jnp.where(lo == 1, 6 - hi, 5))
    bucket = g_idx * N_PAIRS + pair
    w_lo = jnp.where(i1 < i2, w1, w2)
    w_hi = jnp.where(i1 < i2, w2, w1)
    w_a = jnp.where(pair == 5, w_hi, w_lo)
    w_b = jnp.where(pair == 5, w_lo, w_hi)
    onehot = lane == bucket
    oh = jnp.where(onehot, 1.0, 0.0)
    r_i = lax.broadcasted_iota(jnp.int32, (rows, rows), 0)
    c_i = lax.broadcasted_iota(jnp.int32, (rows, rows), 1)
    before = jnp.where(r_i > c_i, 1.0, 0.0).astype(BF16)
    prefix = jnp.dot(before, oh.astype(BF16), preferred_element_type=F32)
    cnt = cnt_ref[...]
    rank = jnp.sum(jnp.where(onehot, prefix + cnt, 0.0), axis=-1, keepdims=True)
    cnt_ref[...] = cnt + jnp.sum(oh, axis=0, keepdims=True)
    return jnp.where(lane == META_BUCKET, bucket.astype(F32),
                     jnp.where(lane == META_W_A, w_a,
                               jnp.where(lane == META_W_B, w_b,
                                         jnp.where(lane == META_RANK, rank, 0.0))))


def _emit_tokens(x, tok_ref, meta_ref, cnt_in_ref, cnt_ref, gf_ref, wr_ref):
    @pl.when(pl.program_id(0) == 0)
    def _():
        cnt_ref[...] = cnt_in_ref[...]
    meta = _route_meta(x, gf_ref[...], wr_ref[...], cnt_ref)
    _rows_to_tokens(x, tok_ref)
    meta_ref[...] = meta.T[:META_ROWS, :]


def _const_spec(*shape):
    return pl.BlockSpec(shape, lambda i, *_: (0,) * len(shape))


_META_SPEC = pl.BlockSpec((META_ROWS, MIX_ROWS), lambda i, *_: (0, i))


def _tokens_alias(tok_prev, input_index):
    if tok_prev is None:
        return {}, jnp.zeros((SUBLANES, LANES), F32)
    return {input_index: 0}, tok_prev


def _tokens_out(n_total_rows, block_off):
    return (pl.BlockSpec((MIX_ROWS * TOKEN_CHUNKS, LANES), lambda i, *_: (i + block_off, 0)),
            jax.ShapeDtypeStruct((n_total_rows * TOKEN_CHUNKS, LANES), F32))


def _s5_kernel(x_ref, g_ref, win_ref, bm_ref, are_ref, aim_ref, cm_ref, d_ref, wout_ref, s0_ref,
               gf_ref, wr_ref, cnt_in_ref, tok_alias_ref,
               tok_ref, meta_ref, so_ref, cnt_ref, bu_ref, st_ref, *, n_seq_groups, steps, carry):
    del tok_alias_ref
    x = x_ref[...]
    h = _rms(x, g_ref[...]).astype(BF16)
    u = jnp.dot(h, win_ref[...], preferred_element_type=F32)
    ub = u.astype(BF16)
    for j in range(N_SLABS):
        bu_ref[:, j * SLAB:(j + 1) * SLAB] = jnp.dot(
            ub[:, j * SLAB_IN:(j + 1) * SLAB_IN], bm_ref[j], preferred_element_type=F32)

    if carry:
        @pl.when(pl.program_id(0) == 0)
        def _():
            st_ref[...] = s0_ref[0]

    for sg in range(n_seq_groups):
        for j in range(N_SLABS):
            for k in range(SLAB_HALF // REC_COLS):
                re0 = j * SLAB + k * REC_COLS
                im0 = re0 + SLAB_HALF
                ar = jnp.broadcast_to(are_ref[:, re0:re0 + REC_COLS], (SUBLANES, REC_COLS))
                ai = jnp.broadcast_to(aim_ref[:, re0:re0 + REC_COLS], (SUBLANES, REC_COLS))
                if carry:
                    sr = st_ref[:, re0:re0 + REC_COLS]
                    si = st_ref[:, im0:im0 + REC_COLS]
                else:
                    sr = s0_ref[sg, :, re0:re0 + REC_COLS]
                    si = s0_ref[sg, :, im0:im0 + REC_COLS]

                def step(t, c, re0=re0, im0=im0, ar=ar, ai=ai, sg=sg):
                    sr, si = c
                    row = pl.multiple_of((sg * steps + t) * SUBLANES, SUBLANES)
                    br = bu_ref[pl.ds(row, SUBLANES), re0:re0 + REC_COLS]
                    bi = bu_ref[pl.ds(row, SUBLANES), im0:im0 + REC_COLS]
                    nr = ar * sr - ai * si + br
                    ni = ar * si + ai * sr + bi
                    bu_ref[pl.ds(row, SUBLANES), re0:re0 + REC_COLS] = nr
                    bu_ref[pl.ds(row, SUBLANES), im0:im0 + REC_COLS] = ni
                    return nr, ni

                sr, si = lax.fori_loop(0, steps, step, (sr, si), unroll=min(steps, 8))
                if carry:
                    st_ref[:, re0:re0 + REC_COLS] = sr
                    st_ref[:, im0:im0 + REC_COLS] = si
                    so_ref[0, :, re0:re0 + REC_COLS] = sr
                    so_ref[0, :, im0:im0 + REC_COLS] = si
                else:
                    so_ref[sg, :, re0:re0 + REC_COLS] = sr
                    so_ref[sg, :, im0:im0 + REC_COLS] = si

    ys = [jnp.dot(bu_ref[:, j * SLAB:(j + 1) * SLAB].astype(BF16), cm_ref[j], preferred_element_type=F32)
          for j in range(N_SLABS)]
    y = jnp.concatenate(ys, axis=-1) + d_ref[...] * u
    y = jax.nn.gelu(y).astype(BF16)
    a = jnp.dot(y, wout_ref[...], preferred_element_type=F32)
    xo = x + a[:, :D_MODEL] * jax.nn.sigmoid(a[:, D_MODEL:])
    _emit_tokens(xo, tok_ref, meta_ref, cnt_in_ref, cnt_ref, gf_ref, wr_ref)


def _s5_layer(x_rows, s0, g, win, bmat, a_re, a_im, cmat, dvec, wout, g_ffn, w_router, cnt_in, tok_prev,
              *, n_total_rows, block_off, n_seq_groups, steps, carry):
    assert n_seq_groups * steps * SUBLANES == MIX_ROWS
    n_blocks = x_rows.shape[0] // MIX_ROWS
    s_idx = (lambda i: (0, 0, 0)) if carry else (lambda i: (i, 0, 0))
    kern = functools.partial(_s5_kernel, n_seq_groups=n_seq_groups, steps=steps, carry=carry)
    n_state_groups = 1 if carry else n_blocks * n_seq_groups
    tok_spec, tok_shape = _tokens_out(n_total_rows, block_off)
    aliases, tok_prev = _tokens_alias(tok_prev, 13)
    inputs = [x_rows, g, win, bmat, a_re, a_im, cmat, dvec, wout, s0, g_ffn, w_router, cnt_in, tok_prev]
    return pl.pallas_call(
        kern,
        grid=(n_blocks,),
        in_specs=[
            pl.BlockSpec((MIX_ROWS, D_MODEL), lambda i: (i, 0)),
            _const_spec(1, D_MODEL), _const_spec(D_MODEL, SSM_WIDTH), _const_spec(N_SLABS, SLAB_IN, SLAB),
            _const_spec(1, STATE_COLS), _const_spec(1, STATE_COLS), _const_spec(N_SLABS, SLAB, SLAB_IN),
            _const_spec(1, SSM_WIDTH), _const_spec(SSM_WIDTH, 2 * D_MODEL),
            pl.BlockSpec((n_seq_groups, SUBLANES, STATE_COLS), s_idx),
            _const_spec(1, D_MODEL), _const_spec(D_MODEL, 2 * ROUTER_LANES), _const_spec(1, LANES),
            pl.BlockSpec(memory_space=pl.ANY),
        ],
        out_specs=[
            tok_spec, _META_SPEC,
            pl.BlockSpec((n_seq_groups, SUBLANES, STATE_COLS), s_idx),
            _const_spec(1, LANES),
        ],
        out_shape=[
            tok_shape, jax.ShapeDtypeStruct((META_ROWS, n_blocks * MIX_ROWS), F32),
            jax.ShapeDtypeStruct((n_state_groups, SUBLANES, STATE_COLS), F32),
            jax.ShapeDtypeStruct((1, LANES), F32),
        ],
        input_output_aliases=aliases,
        scratch_shapes=[pltpu.VMEM((MIX_ROWS, STATE_COLS), F32), pltpu.VMEM((SUBLANES, STATE_COLS), F32)],
        compiler_params=pltpu.CompilerParams(
            dimension_semantics=("arbitrary",), vmem_limit_bytes=VMEM_LIMIT_BYTES),
        name="s5_carry" if carry else "s5_step",
    )(*inputs)


def _sgu_kernel(pos_ref, src_hbm, g_ref, win_ref, ng_ref, nb_ref, ws_ref, bs_ref, wout_ref,
                gf_ref, wr_ref, cnt_in_ref, tok_alias_ref, *refs, n_blocks, block_off, block_diag, emit_v):
    del tok_alias_ref
    tok_ref, meta_ref, cnt_ref = refs[:3]
    v_ref = refs[3] if emit_v else None
    xbuf, sems = refs[-2:]
    x = _gathered_rows(pos_ref, block_off, n_blocks, src_hbm, xbuf, sems, MIX_ROWS)
    rows = x.shape[0]
    h = _rms(x, g_ref[...]).astype(BF16)
    hh = jax.nn.gelu(jnp.dot(h, win_ref[...], preferred_element_type=F32))
    u = hh[:, :SGU_WIDTH]
    v = hh[:, SGU_WIDTH:]
    vc = v - jnp.mean(v, axis=-1, keepdims=True)
    v = vc * lax.rsqrt(jnp.mean(vc * vc, axis=-1, keepdims=True) + EPS) * ng_ref[...] + nb_ref[...]
    if emit_v:
        v_ref[...] = v
    r = lax.broadcasted_iota(jnp.int32, (CHUNK, CHUNK), 0)
    c = lax.broadcasted_iota(jnp.int32, (CHUNK, CHUNK), 1)
    if block_diag:
        keep = (r // SUBLANES == c // SUBLANES) & (r >= c)
    else:
        keep = r >= c
    vb = v.astype(BF16)
    n_chunks = rows // CHUNK
    zs = []
    for hd in range(SGU_HEADS):
        wm = jnp.where(keep, ws_ref[hd], 0.0).astype(BF16)
        cols = slice(hd * SGU_HEAD_DIM, (hd + 1) * SGU_HEAD_DIM)
        vcat = jnp.concatenate([vb[ck * CHUNK:(ck + 1) * CHUNK, cols] for ck in range(n_chunks)], axis=-1)
        zcat = jnp.dot(wm, vcat, preferred_element_type=F32)
        zs.append([zcat[:, ck * SGU_HEAD_DIM:(ck + 1) * SGU_HEAD_DIM] for ck in range(n_chunks)])
    z = jnp.concatenate(
        [jnp.concatenate([zs[hd][ck] for hd in range(SGU_HEADS)], axis=-1) + bs_ref[...]
         for ck in range(n_chunks)], axis=0)
    o = jnp.dot((u * z).astype(BF16), wout_ref[...], preferred_element_type=F32)
    _emit_tokens(x + o, tok_ref, meta_ref, cnt_in_ref, cnt_ref, gf_ref, wr_ref)


def _sgu_layer(pos, src_tokens, g, win, ng, nb, ws, bs, wout, g_ffn, w_router, cnt_in, tok_prev,
               *, n_total_rows, n_blocks, block_off, block_diag, emit_v):
    tok_spec, tok_shape = _tokens_out(n_total_rows, block_off)
    out_shape = [tok_shape, jax.ShapeDtypeStruct((META_ROWS, n_blocks * MIX_ROWS), F32),
                 jax.ShapeDtypeStruct((1, LANES), F32)]
    out_specs = [tok_spec, _META_SPEC, _const_spec(1, LANES)]
    if emit_v:
        out_shape.append(jax.ShapeDtypeStruct((n_blocks * MIX_ROWS, SGU_WIDTH), F32))
        out_specs.append(pl.BlockSpec((MIX_ROWS, SGU_WIDTH), lambda i, *_: (i, 0)))
    aliases, tok_prev = _tokens_alias(tok_prev, 12)
    inputs = [src_tokens, g, win, ng, nb, ws, bs, wout, g_ffn, w_router, cnt_in, tok_prev]
    return pl.pallas_call(
        functools.partial(_sgu_kernel, n_blocks=n_blocks, block_off=block_off, block_diag=block_diag,
                          emit_v=emit_v),
        grid_spec=pltpu.PrefetchScalarGridSpec(
            num_scalar_prefetch=1,
            grid=(n_blocks,),
            in_specs=[pl.BlockSpec(memory_space=pl.ANY), _const_spec(1, D_MODEL),
                      _const_spec(D_MODEL, 2 * SGU_WIDTH), _const_spec(1, SGU_WIDTH), _const_spec(1, SGU_WIDTH),
                      _const_spec(SGU_HEADS, CHUNK, CHUNK), _const_spec(CHUNK, SGU_WIDTH),
                      _const_spec(SGU_WIDTH, D_MODEL), _const_spec(1, D_MODEL),
                      _const_spec(D_MODEL, 2 * ROUTER_LANES), _const_spec(1, LANES),
                      pl.BlockSpec(memory_space=pl.ANY)],
            out_specs=out_specs,
            scratch_shapes=[pltpu.VMEM((GATHER_DEPTH, MIX_ROWS * TOKEN_CHUNKS, LANES), F32),
                            pltpu.SemaphoreType.DMA((GATHER_DEPTH,))],
        ),
        out_shape=out_shape,
        input_output_aliases=aliases,
        compiler_params=pltpu.CompilerParams(
            dimension_semantics=("arbitrary",), vmem_limit_bytes=VMEM_LIMIT_BYTES),
        name="sgu_block_diag" if block_diag else "sgu_chunked",
    )(pos, *inputs)


def _moe_plan(meta, cnt, n_tiles_max):
    i32 = jnp.int32
    n_tok = meta.shape[1]
    bucket = meta[META_BUCKET].astype(i32)
    rank = meta[META_RANK].astype(i32)
    n = cnt[0, :N_BUCKETS].astype(i32)
    nt = (n + MOE_TILE - 1) // MOE_TILE
    tile_end = jnp.cumsum(nt)
    tile_start = tile_end - nt
    pos = tile_start[bucket] * MOE_TILE + rank
    n_tiles = tile_end[-1]
    j = jnp.minimum(jnp.arange(n_tiles_max, dtype=i32), jnp.maximum(n_tiles - 1, 0))
    tb = jnp.minimum(jnp.sum((j[:, None] >= tile_end[None, :]).astype(i32), axis=1), N_BUCKETS - 1)
    grp = tb // N_PAIRS
    pair = tb % N_PAIRS
    e_a = grp * EXPERTS_PER_GROUP + jnp.asarray(PAIR_A, i32)[pair]
    e_b = grp * EXPERTS_PER_GROUP + jnp.asarray(PAIR_B, i32)[pair]
    pad_lo = jnp.minimum(tile_start * MOE_TILE + n, (n_tiles_max - 1) * MOE_TILE)
    bits = lambda w: lax.bitcast_convert_type(w, i32)
    token_of_slot, gate_a, gate_b = _invert_plan(
        pos, bits(meta[META_W_A]), bits(meta[META_W_B]), pad_lo, n_tiles_max * MOE_TILE)
    as_f32 = lambda w: lax.bitcast_convert_type(w, F32)[None, :]
    gates = jnp.concatenate(
        [as_f32(gate_a), as_f32(gate_b), jnp.zeros((META_ROWS - 2, n_tiles_max * MOE_TILE), F32)], axis=0)
    return pos, token_of_slot, gates, e_a, e_b, n_tiles.reshape(1)


def _invert_kernel(pos_ref, wa_ref, wb_ref, pad_lo_ref, tok_ref, ga_ref, gb_ref):
    def pad_bucket(b, c):
        base = pad_lo_ref[b]

        def pad(k, c):
            tok_ref[base + k] = 0
            ga_ref[base + k] = 0
            gb_ref[base + k] = 0
            return c

        return lax.fori_loop(0, MOE_TILE, pad, c, unroll=8)

    lax.fori_loop(0, N_BUCKETS, pad_bucket, 0)

    def place(t, c):
        p = pos_ref[t]
        tok_ref[p] = t
        ga_ref[p] = wa_ref[t]
        gb_ref[p] = wb_ref[t]
        return c

    lax.fori_loop(0, pos_ref.shape[0], place, 0, unroll=8)


def _invert_plan(pos, wa_bits, wb_bits, pad_lo, n_slots):
    smem = pl.BlockSpec(memory_space=pltpu.SMEM)
    sds = jax.ShapeDtypeStruct((n_slots,), jnp.int32)
    return pl.pallas_call(
        _invert_kernel, in_specs=[smem] * 4, out_specs=[smem] * 3, out_shape=[sds] * 3, name="moe_invert",
    )(pos, wa_bits, wb_bits, pad_lo)


def _moe_kernel(tok_of_slot_ref, ea_ref, eb_ref, nt_ref, src_hbm, gates_ref, g_ref,
                wg_a, wu_a, wd_a, wg_b, wu_b, wd_b, gf_ref, ys_ref, xbuf, wgu_ref, wd_ref, sems,
                *, final_norm):
    j = pl.program_id(0)
    n_tiles = nt_ref[0]

    @pl.when(j == 0)
    def _():
        _gather_prologue(tok_of_slot_ref, 0, n_tiles, src_hbm, xbuf, sems, MOE_TILE)

    @pl.when(j < n_tiles)
    def _():
        prev = jnp.maximum(j - 1, 0)
        for s, (e_ref, wg, wu, wd) in enumerate(((ea_ref, wg_a, wu_a, wd_a), (eb_ref, wg_b, wu_b, wd_b))):
            @pl.when((j == 0) | (e_ref[j] != e_ref[prev]))
            def _():
                wgu_ref[s, :, :D_EXPERT] = wg[0].astype(BF16)
                wgu_ref[s, :, D_EXPERT:] = wu[0].astype(BF16)
                wd_ref[s] = wd[0].astype(BF16)

        x = _gather_step(j, tok_of_slot_ref, 0, n_tiles, src_hbm, xbuf, sems, MOE_TILE)
        xn = _rms(x, g_ref[...]).astype(BF16)
        gates = jnp.concatenate(
            [gates_ref[...], jnp.zeros((LANES - META_ROWS, MOE_TILE), F32)], axis=0).T
        out = None
        for s in range(2):
            gu = jnp.dot(xn, wgu_ref[s], preferred_element_type=F32)
            hcur = (jax.nn.silu(gu[:, :D_EXPERT]) * gu[:, D_EXPERT:] * gates[:, s:s + 1]).astype(BF16)
            o = jnp.dot(hcur, wd_ref[s], preferred_element_type=F32)
            out = o if out is None else out + o
        y = x + out
        if final_norm:
            y = _rms(y, gf_ref[...])
        _rows_to_tokens(y, ys_ref)

        @pl.when(j == n_tiles - 1)
        def _():
            _gather_epilogue(j, src_hbm, xbuf, sems)


def _moe_experts(tokens, token_of_slot, gates, e_a, e_b, n_tiles, g, w_gate, w_up, w_down, g_final,
                 *, final_norm):
    n_tiles_max = e_a.shape[0]
    last = lambda j, nt: jnp.minimum(j, jnp.maximum(nt[0] - 1, 0))
    w_in = lambda which: pl.BlockSpec(
        (1, D_MODEL, D_EXPERT), lambda j, tos, ea, eb, nt: ((ea, eb)[which][j], 0, 0))
    w_out = lambda which: pl.BlockSpec(
        (1, D_EXPERT, D_MODEL), lambda j, tos, ea, eb, nt: ((ea, eb)[which][j], 0, 0))
    return pl.pallas_call(
        functools.partial(_moe_kernel, final_norm=final_norm),
        grid_spec=pltpu.PrefetchScalarGridSpec(
            num_scalar_prefetch=4,
            grid=(n_tiles_max,),
            in_specs=[pl.BlockSpec(memory_space=pl.ANY),
                      pl.BlockSpec((META_ROWS, MOE_TILE), lambda j, tos, ea, eb, nt: (0, last(j, nt))),
                      _const_spec(1, D_MODEL), w_in(0), w_in(0), w_out(0), w_in(1), w_in(1), w_out(1),
                      _const_spec(1, D_MODEL)],
            out_specs=pl.BlockSpec((MOE_TILE * TOKEN_CHUNKS, LANES),
                                   lambda j, tos, ea, eb, nt: (last(j, nt), 0)),
            scratch_shapes=[pltpu.VMEM((GATHER_DEPTH, MOE_TILE * TOKEN_CHUNKS, LANES), F32),
                            pltpu.VMEM((2, D_MODEL, 2 * D_EXPERT), BF16), pltpu.VMEM((2, D_EXPERT, D_MODEL), BF16),
                            pltpu.SemaphoreType.DMA((GATHER_DEPTH,))],
        ),
        out_shape=jax.ShapeDtypeStruct((n_tiles_max * MOE_TILE * TOKEN_CHUNKS, LANES), F32),
        compiler_params=pltpu.CompilerParams(
            dimension_semantics=("arbitrary",), vmem_limit_bytes=VMEM_LIMIT_BYTES),
        name="moe_experts_final" if final_norm else "moe_experts",
    )(token_of_slot, e_a, e_b, n_tiles, tokens, gates, g, w_gate, w_up, w_down, w_gate, w_up, w_down, g_final)


def _moe_layer(tokens, meta, cnt, g, w_gate, w_up, w_down, g_final, *, final_norm):
    n_tiles_max = meta.shape[1] // MOE_TILE + N_BUCKETS
    pos, token_of_slot, gates, e_a, e_b, n_tiles = _moe_plan(meta, cnt, n_tiles_max)
    ys = _moe_experts(tokens, token_of_slot, gates, e_a, e_b, n_tiles, g, w_gate, w_up, w_down, g_final,
                      final_norm=final_norm)
    return ys, pos


def _ungather_kernel(pos_ref, src_hbm, *refs, n_blocks, n_first_blocks):
    o_refs, (xbuf, sems) = refs[:-2], refs[-2:]
    i = pl.program_id(0)
    x = _gathered_rows(pos_ref, 0, n_blocks, src_hbm, xbuf, sems, MIX_ROWS)

    @pl.when(i < n_first_blocks)
    def _():
        o_refs[0][...] = x

    @pl.when(i >= n_first_blocks)
    def _():
        o_refs[1][...] = x


def _ungather(ys, pos, n_first_rows):
    n_blocks = pos.shape[0] // MIX_ROWS
    n_first = n_first_rows // MIX_ROWS
    spec = lambda f: pl.BlockSpec((MIX_ROWS, D_MODEL), f)
    return pl.pallas_call(
        functools.partial(_ungather_kernel, n_blocks=n_blocks, n_first_blocks=n_first),
        grid_spec=pltpu.PrefetchScalarGridSpec(
            num_scalar_prefetch=1,
            grid=(n_blocks,),
            in_specs=[pl.BlockSpec(memory_space=pl.ANY)],
            out_specs=[spec(lambda i, pos: (jnp.minimum(i, n_first - 1), 0)),
                       spec(lambda i, pos: (jnp.maximum(i - n_first, 0), 0))],
            scratch_shapes=[pltpu.VMEM((GATHER_DEPTH, MIX_ROWS * TOKEN_CHUNKS, LANES), F32),
                            pltpu.SemaphoreType.DMA((GATHER_DEPTH,))],
        ),
        out_shape=[jax.ShapeDtypeStruct((n_first_rows, D_MODEL), F32),
                   jax.ShapeDtypeStruct((pos.shape[0] - n_first_rows, D_MODEL), F32)],
        compiler_params=pltpu.CompilerParams(
            dimension_semantics=("arbitrary",), vmem_limit_bytes=VMEM_LIMIT_BYTES),
        name="moe_ungather",
    )(pos, ys)


def _router_weights(w_rg, w_re):
    pad = ROUTER_LANES - N_EXPERT_GROUPS - N_EXPERTS
    w = jnp.concatenate([w_rg.astype(F32), w_re.astype(F32), jnp.zeros((D_MODEL, pad), F32)], axis=-1)
    hi = w.astype(BF16)
    lo = (w - hi.astype(F32)).astype(BF16)
    return jnp.concatenate([hi, lo], axis=-1)


def kernel(x_prompt, x_sample, state_ssm_re, state_ssm_im, norm_mix, norm_ffn, norm_final, ssm_w_in, ssm_lambda_re, ssm_lambda_im, ssm_log_step, ssm_b_re, ssm_b_im, ssm_c_re, ssm_c_im, ssm_d, ssm_w_out, sgu_w_in, sgu_norm_g, sgu_norm_b, sgu_w_s, sgu_b_s, sgu_w_out, moe_router_group, moe_router_expert, moe_w_gate, moe_w_up, moe_w_down):
    nb, ns, dm = x_prompt.shape
    db, ds, _ = x_sample.shape
    n_p, n_s = nb * ns, db * ds
    n_tok = n_p + n_s
    n_sg = db // SUBLANES
    steps_p = MIX_ROWS // SUBLANES
    sg_per_block = MIX_ROWS // (ds * SUBLANES)
    assert nb == SUBLANES and dm == D_MODEL and ns % steps_p == 0 and n_sg % sg_per_block == 0
    assert n_p % MIX_ROWS == 0 and n_s % MIX_ROWS == 0 and MIX_ROWS % MOE_TILE == 0 and CHUNK % ds == 0
    row = lambda a: a.astype(F32).reshape(1, -1)
    zero_cnt = jnp.zeros((1, LANES), F32)

    lb_re, lb_im, bb_re, bb_im = _s5_prep(ssm_lambda_re, ssm_lambda_im, ssm_log_step, ssm_b_re, ssm_b_im)
    bmat, cmat, a_re, a_im, dvec = _s5_matrices(lb_re, lb_im, bb_re, bb_im, ssm_c_re, ssm_c_im, ssm_d)
    xp = jnp.transpose(x_prompt, (1, 0, 2)).reshape(n_p, dm)
    xs = jnp.transpose(x_sample.reshape(n_sg, SUBLANES, ds, dm), (0, 2, 1, 3)).reshape(n_s, dm)
    zero_state = jnp.zeros((1, SUBLANES, STATE_COLS), F32)
    s0 = _pack_state(state_ssm_re.astype(F32), state_ssm_im.astype(F32)).reshape(n_sg, SUBLANES, STATE_COLS)
    wr0 = _router_weights(moe_router_group[0], moe_router_expert[0])
    s5 = functools.partial(_s5_layer, g=row(norm_mix[0]), win=ssm_w_in.astype(BF16), bmat=bmat, a_re=a_re,
                           a_im=a_im, cmat=cmat, dvec=dvec, wout=ssm_w_out.astype(BF16),
                           g_ffn=row(norm_ffn[0]), w_router=wr0, n_total_rows=n_tok)
    tok0, meta_p, st_p, cnt0 = s5(xp, zero_state, cnt_in=zero_cnt, tok_prev=None, block_off=0,
                                  n_seq_groups=1, steps=steps_p, carry=True)
    tok0, meta_s, st_s, cnt0 = s5(xs, s0, cnt_in=cnt0, tok_prev=tok0, block_off=n_p // MIX_ROWS,
                                  n_seq_groups=sg_per_block, steps=ds, carry=False)
    ssm_re_p, ssm_im_p = _unpack_state(st_p.reshape(nb, STATE_COLS))
    ssm_re_s, ssm_im_s = _unpack_state(st_s.reshape(db, STATE_COLS))
    ys0, pos0 = _moe_layer(tok0, jnp.concatenate([meta_p, meta_s], axis=1), cnt0, row(norm_ffn[0]),
                           moe_w_gate[0], moe_w_up[0], moe_w_down[0], row(norm_final), final_norm=False)

    pos0 = jnp.concatenate([
        pos0[:n_p].reshape(ns, nb).T.reshape(-1),
        jnp.transpose(pos0[n_p:].reshape(n_sg, ds, SUBLANES), (0, 2, 1)).reshape(-1)])
    ws_p = sgu_w_s.astype(F32)[:, :CHUNK, :CHUNK]
    bs_p = jnp.repeat(sgu_b_s.astype(F32)[:, :CHUNK].T, SGU_HEAD_DIM, axis=1)
    reps = CHUNK // ds
    ws_s = jnp.tile(sgu_w_s.astype(F32)[:, :ds, :ds], (1, reps, reps))
    bs_s = jnp.tile(jnp.repeat(sgu_b_s.astype(F32)[:, :ds].T, SGU_HEAD_DIM, axis=1), (reps, 1))
    wr1 = _router_weights(moe_router_group[1], moe_router_expert[1])
    sgu = functools.partial(_sgu_layer, pos0, ys0, g=row(norm_mix[1]), win=sgu_w_in.astype(BF16),
                            ng=row(sgu_norm_g), nb=row(sgu_norm_b), wout=sgu_w_out.astype(BF16),
                            g_ffn=row(norm_ffn[1]), w_router=wr1, n_total_rows=n_tok)
    tok1, meta_p, cnt1 = sgu(ws=ws_p, bs=bs_p, cnt_in=zero_cnt, tok_prev=None, n_blocks=n_p // MIX_ROWS,
                             block_off=0, block_diag=False, emit_v=False)
    tok1, meta_s, cnt1, v_s = sgu(ws=ws_s, bs=bs_s, cnt_in=cnt1, tok_prev=tok1, n_blocks=n_s // MIX_ROWS,
                                  block_off=n_p // MIX_ROWS, block_diag=True, emit_v=True)
    ys1, pos1 = _moe_layer(tok1, jnp.concatenate([meta_p, meta_s], axis=1), cnt1, row(norm_ffn[1]),
                           moe_w_gate[1], moe_w_up[1], moe_w_down[1], row(norm_final), final_norm=True)
    y_p, y_s = _ungather(ys1, pos1, n_p)
    return (y_p.reshape(nb, ns, dm), y_s.reshape(db, ds, dm), ssm_re_p, ssm_im_p, ssm_re_s, ssm_im_s,
            v_s.reshape(db, ds, SGU_WIDTH))
```

```python
import functools

import jax
import jax.numpy as jnp
from jax import lax
from jax.experimental import pallas as pl
from jax.experimental.pallas import tpu as pltpu

D_MODEL = 1024
SSM_WIDTH = D_MODEL // 2
SSM_GROUP = 16
SSM_GROUPS = SSM_WIDTH // SSM_GROUP
SSM_STATE = 64
SGU_WIDTH = D_MODEL
SGU_HEADS = 8
SGU_HEAD_DIM = SGU_WIDTH // SGU_HEADS
CHUNK = 128
N_EXPERT_GROUPS = 4
EXPERTS_PER_GROUP = 4
N_EXPERTS = N_EXPERT_GROUPS * EXPERTS_PER_GROUP
D_EXPERT = D_MODEL // 4
EPS = 1e-6

SUBLANES = 8
LANES = 128
VMEM_LIMIT_BYTES = 56 * 1024 * 1024
N_DMA_PRIORITIES = 2

SLAB_GROUPS = 16
N_SLABS = SSM_GROUPS // SLAB_GROUPS
SLAB_HALF = SLAB_GROUPS * SSM_STATE
SLAB = 2 * SLAB_HALF
STATE_COLS = N_SLABS * SLAB
SLAB_IN = SLAB_GROUPS * SSM_GROUP
REC_COLS = 512

TOKEN_CHUNKS = D_MODEL // LANES
assert TOKEN_CHUNKS == SUBLANES

PAIR_A = (0, 0, 0, 1, 1, 3)
PAIR_B = (1, 2, 3, 3, 2, 2)
N_PAIRS = len(PAIR_A)
GATHER_DEPTH = 3
GATHER_UNROLL = 16
CAST_ROWS = 256
N_BUCKETS = N_EXPERT_GROUPS * N_PAIRS
MOE_TILE = 256
MIX_ROWS = 512
META_ROWS = SUBLANES
META_BUCKET, META_W_A, META_W_B, META_RANK = 0, 1, 2, 3
ROUTER_LANES = LANES
BF16 = jnp.bfloat16
F32 = jnp.float32


def _rms(x, g):
    return x * lax.rsqrt(jnp.mean(x * x, axis=-1, keepdims=True) + EPS) * g


def _s5_prep_kernel(lr_ref, li_ref, ls_ref, br_ref, bi_ref, lbr_ref, lbi_ref, bbr_ref, bbi_ref):
    lr = lr_ref[...]
    li = li_ref[...]
    dt = jnp.exp(ls_ref[...])
    mag = jnp.exp(lr * dt)
    ang = li * dt
    lb_re = mag * jnp.cos(ang)
    lb_im = mag * jnp.sin(ang)
    den = lr * lr + li * li
    nr = lb_re - 1.0
    coef_re = (nr * lr + lb_im * li) / den
    coef_im = (lb_im * lr - nr * li) / den
    br = br_ref[...]
    bi = bi_ref[...]
    lbr_ref[...] = lb_re
    lbi_ref[...] = lb_im
    bbr_ref[...] = coef_re * br - coef_im * bi
    bbi_ref[...] = coef_re * bi + coef_im * br


def _s5_prep(lam_re, lam_im, log_step, b_re, b_im):
    rows = SSM_GROUPS * SSM_GROUP
    rep = lambda a: jnp.repeat(a.astype(F32), SSM_GROUP, axis=0)
    lr = rep(lam_re)
    li = rep(lam_im)
    ls = rep(jnp.broadcast_to(log_step.astype(F32)[:, None], (SSM_GROUPS, SSM_STATE)))
    br = jnp.transpose(b_re.astype(F32), (0, 2, 1)).reshape(rows, SSM_STATE)
    bi = jnp.transpose(b_im.astype(F32), (0, 2, 1)).reshape(rows, SSM_STATE)
    sds = jax.ShapeDtypeStruct((rows, SSM_STATE), F32)
    lb_re, lb_im, bb_re, bb_im = pl.pallas_call(
        _s5_prep_kernel, out_shape=(sds, sds, sds, sds), name="s5_prep")(lr, li, ls, br, bi)
    return lb_re, lb_im, bb_re, bb_im


def _to_state_cols(a):
    return a.reshape(a.shape[:-2] + (N_SLABS, SLAB_HALF))


def _pack_state(re, im):
    return jnp.concatenate([_to_state_cols(re), _to_state_cols(im)], axis=-1).reshape(re.shape[0], STATE_COLS)


def _unpack_state(s):
    s = s.reshape(s.shape[0], N_SLABS, 2, SLAB_GROUPS, SSM_STATE)
    re = s[:, :, 0].reshape(s.shape[0], SSM_GROUPS, SSM_STATE)
    im = s[:, :, 1].reshape(s.shape[0], SSM_GROUPS, SSM_STATE)
    return re, im


def _s5_matrices(lb_re, lb_im, bb_re, bb_im, c_re, c_im, d):
    eye = jnp.eye(SLAB_GROUPS, dtype=F32)
    def in_side(bb):
        bb = bb.reshape(N_SLABS, SLAB_GROUPS, SSM_GROUP, SSM_STATE)
        m = bb[:, :, :, None, :] * eye[None, :, None, :, None]
        return m.reshape(N_SLABS, SLAB_IN, SLAB_HALF)
    bmat = jnp.concatenate([in_side(bb_re), in_side(bb_im)], axis=-1).astype(BF16)
    def out_side(c):
        c = jnp.transpose(c.astype(F32), (0, 2, 1)).reshape(N_SLABS, SLAB_GROUPS, SSM_STATE, SSM_GROUP)
        m = c[:, :, :, None, :] * eye[None, :, None, :, None]
        return m.reshape(N_SLABS, SLAB_HALF, SLAB_IN)
    cmat = jnp.concatenate([out_side(c_re), -out_side(c_im)], axis=1).astype(BF16)
    lam = lambda lb: _to_state_cols(lb.reshape(SSM_GROUPS, SSM_GROUP, SSM_STATE)[:, 0])
    a_re = jnp.concatenate([lam(lb_re), lam(lb_re)], axis=-1).reshape(1, STATE_COLS)
    a_im = jnp.concatenate([lam(lb_im), lam(lb_im)], axis=-1).reshape(1, STATE_COLS)
    return bmat, cmat, a_re, a_im, d.astype(F32).reshape(1, SSM_WIDTH)


def _rows_to_tokens(x, tok_ref):
    n = x.shape[0]
    for c in range(TOKEN_CHUNKS):
        tok_ref[pl.ds(c, n, stride=TOKEN_CHUNKS), :] = x[:, c * LANES:(c + 1) * LANES]


def _tokens_to_rows(tok_ref, n):
    return jnp.concatenate([tok_ref[pl.ds(c, n, stride=TOKEN_CHUNKS), :] for c in range(TOKEN_CHUNKS)], axis=-1)


def _start_token_gather(idx_ref, base, src_hbm, buf_ref, sem, n):
    def issue(blk, c):
        for k in range(GATHER_UNROLL):
            r = blk * GATHER_UNROLL + k
            tok = idx_ref[base + r]
            pltpu.make_async_copy(
                src_hbm.at[pl.ds(pl.multiple_of(tok * TOKEN_CHUNKS, TOKEN_CHUNKS), TOKEN_CHUNKS), :],
                buf_ref.at[pl.ds(pl.multiple_of(r * TOKEN_CHUNKS, TOKEN_CHUNKS), TOKEN_CHUNKS), :],
                sem).start(priority=k % N_DMA_PRIORITIES)
        return c

    lax.fori_loop(0, n // GATHER_UNROLL, issue, 0)


def _wait_token_gather(src_hbm, buf_ref, sem):
    pltpu.make_async_copy(src_hbm.at[pl.ds(0, buf_ref.shape[0]), :], buf_ref, sem).wait()


def _gather_prologue(idx_ref, first_block, n_blocks, src_hbm, buf_ref, sems, n):
    for k in range(GATHER_DEPTH - 1):
        blk = first_block + jnp.minimum(k, n_blocks - 1)
        _start_token_gather(idx_ref, blk * n, src_hbm, buf_ref.at[k], sems.at[k], n)


def _gather_step(i, idx_ref, first_block, n_blocks, src_hbm, buf_ref, sems, n):
    slot = lax.rem(i, GATHER_DEPTH)
    ahead = lax.rem(i + GATHER_DEPTH - 1, GATHER_DEPTH)
    _wait_token_gather(src_hbm, buf_ref.at[slot], sems.at[slot])
    blk = first_block + jnp.minimum(i + GATHER_DEPTH - 1, n_blocks - 1)
    _start_token_gather(idx_ref, blk * n, src_hbm, buf_ref.at[ahead], sems.at[ahead], n)
    return _tokens_to_rows(buf_ref.at[slot], n)


def _gather_epilogue(i, src_hbm, buf_ref, sems):
    for k in range(1, GATHER_DEPTH):
        slot = lax.rem(i + k, GATHER_DEPTH)
        _wait_token_gather(src_hbm, buf_ref.at[slot], sems.at[slot])


def _gathered_rows(idx_ref, first_block, n_blocks, src_hbm, buf_ref, sems, n):
    i = pl.program_id(0)

    @pl.when(i == 0)
    def _():
        _gather_prologue(idx_ref, first_block, n_blocks, src_hbm, buf_ref, sems, n)

    x = _gather_step(i, idx_ref, first_block, n_blocks, src_hbm, buf_ref, sems, n)

    @pl.when(i == n_blocks - 1)
    def _():
        _gather_epilogue(i, src_hbm, buf_ref, sems)

    return x


def _route_meta(x, g, wr, cnt_ref):
    xn = _rms(x, g)
    xh = xn.astype(BF16)
    xl = (xn - xh.astype(F32)).astype(BF16)
    p = jnp.dot(xh, wr, preferred_element_type=F32)
    q = jnp.dot(xl, wr[:, :ROUTER_LANES], preferred_element_type=F32)
    logits = p[:, :ROUTER_LANES] + (p[:, ROUTER_LANES:] + q)
    rows = logits.shape[0]
    lane = lax.broadcasted_iota(jnp.int32, logits.shape, 1)
    neg = jnp.float32(-jnp.inf)
    big = jnp.int32(ROUTER_LANES)
    is_g = lane < N_EXPERT_GROUPS
    lg = jnp.where(is_g, logits, neg)
    mg = jnp.max(lg, axis=-1, keepdims=True)
    g_idx = jnp.min(jnp.where(lg == mg, lane, big), axis=-1, keepdims=True)
    g_w = 1.0 / jnp.sum(jnp.where(is_g, jnp.exp(lg - mg), 0.0), axis=-1, keepdims=True)
    first = N_EXPERT_GROUPS + g_idx * EXPERTS_PER_GROUP
    in_grp = (lane >= first) & (lane < first + EXPERTS_PER_GROUP)
    le = jnp.where(in_grp, logits, neg)
    m1 = jnp.max(le, axis=-1, keepdims=True)
    i1 = jnp.min(jnp.where(le == m1, lane, big), axis=-1, keepdims=True)
    le2 = jnp.where(lane == i1, neg, le)
    m2 = jnp.max(le2, axis=-1, keepdims=True)
    i2 = jnp.min(jnp.where(le2 == m2, lane, big), axis=-1, keepdims=True)
    e2 = jnp.exp(m2 - m1)
    w1 = g_w / (1.0 + e2)
    w2 = g_w * e2 / (1.0 + e2)
    lo = jnp.minimum(i1, i2) - first
    hi = jnp.maximum(i1, i2) - first
    pair = jnp.where(lo == 0, hi - 1, jnp.where(lo == 1, 6 - hi, 5))
    bucket = g_idx * N_PAIRS + pair
    w_lo = jnp.where(i1 < i2, w1, w2)
    w_hi = jnp.where(i1 < i2, w2, w1)
    w_a = jnp.where(pair == 5, w_hi, w_lo)
    w_b = jnp.where(pair == 5, w_lo, w_hi)
    onehot = lane == bucket
    oh = jnp.where(onehot, 1.0, 0.0)
    r_i = lax.broadcasted_iota(jnp.int32, (rows, rows), 0)
    c_i = lax.broadcasted_iota(jnp.int32, (rows, rows), 1)
    before = jnp.where(r_i > c_i, 1.0, 0.0).astype(BF16)
    prefix = jnp.dot(before, oh.astype(BF16), preferred_element_type=F32)
    cnt = cnt_ref[...]
    rank = jnp.sum(jnp.where(onehot, prefix + cnt, 0.0), axis=-1, keepdims=True)
    cnt_ref[...] = cnt + jnp.sum(oh, axis=0, keepdims=True)
    return jnp.where(lane == META_BUCKET, bucket.astype(F32),
                     jnp.where(lane == META_W_A, w_a,
                               jnp.where(lane == META_W_B, w_b,
                                         jnp.where(lane == META_RANK, rank, 0.0))))


def _emit_tokens(x, tok_ref, meta_ref, cnt_in_ref, cnt_ref, gf_ref, wr_ref):
    @pl.when(pl.program_id(0) == 0)
    def _():
        cnt_ref[...] = cnt_in_ref[...]
    meta = _route_meta(x, gf_ref[...], wr_ref[...], cnt_ref)
    _rows_to_tokens(x, tok_ref)
    meta_ref[...] = meta.T[:META_ROWS, :]


def _const_spec(*shape):
    return pl.BlockSpec(shape, lambda i, *_: (0,) * len(shape))


_META_SPEC = pl.BlockSpec((META_ROWS, MIX_ROWS), lambda i, *_: (0, i))


def _tokens_alias(tok_prev, input_index):
    if tok_prev is None:
        return {}, jnp.zeros((SUBLANES, LANES), F32)
    return {input_index: 0}, tok_prev


def _tokens_out(n_total_rows, block_off):
    return (pl.BlockSpec((MIX_ROWS * TOKEN_CHUNKS, LANES), lambda i, *_: (i + block_off, 0)),
            jax.ShapeDtypeStruct((n_total_rows * TOKEN_CHUNKS, LANES), F32))


def _s5_kernel(x_ref, g_ref, win_ref, bm_ref, are_ref, aim_ref, cm_ref, d_ref, wout_ref, s0_ref,
               gf_ref, wr_ref, cnt_in_ref, tok_alias_ref,
               tok_ref, meta_ref, so_ref, cnt_ref, bu_ref, st_ref, *, n_seq_groups, steps, carry):
    del tok_alias_ref
    x = x_ref[...]
    h = _rms(x, g_ref[...]).astype(BF16)
    u = jnp.dot(h, win_ref[...], preferred_element_type=F32)
    ub = u.astype(BF16)
    for j in range(N_SLABS):
        bu_ref[:, j * SLAB:(j + 1) * SLAB] = jnp.dot(
            ub[:, j * SLAB_IN:(j + 1) * SLAB_IN], bm_ref[j], preferred_element_type=F32)

    if carry:
        @pl.when(pl.program_id(0) == 0)
        def _():
            st_ref[...] = s0_ref[0]

    for sg in range(n_seq_groups):
        for j in range(N_SLABS):
            for k in range(SLAB_HALF // REC_COLS):
                re0 = j * SLAB + k * REC_COLS
                im0 = re0 + SLAB_HALF
                ar = jnp.broadcast_to(are_ref[:, re0:re0 + REC_COLS], (SUBLANES, REC_COLS))
                ai = jnp.broadcast_to(aim_ref[:, re0:re0 + REC_COLS], (SUBLANES, REC_COLS))
                if carry:
                    sr = st_ref[:, re0:re0 + REC_COLS]
                    si = st_ref[:, im0:im0 + REC_COLS]
                else:
                    sr = s0_ref[sg, :, re0:re0 + REC_COLS]
                    si = s0_ref[sg, :, im0:im0 + REC_COLS]

                def step(t, c, re0=re0, im0=im0, ar=ar, ai=ai, sg=sg):
                    sr, si = c
                    row = pl.multiple_of((sg * steps + t) * SUBLANES, SUBLANES)
                    br = bu_ref[pl.ds(row, SUBLANES), re0:re0 + REC_COLS]
                    bi = bu_ref[pl.ds(row, SUBLANES), im0:im0 + REC_COLS]
                    nr = ar * sr - ai * si + br
                    ni = ar * si + ai * sr + bi
                    bu_ref[pl.ds(row, SUBLANES), re0:re0 + REC_COLS] = nr
                    bu_ref[pl.ds(row, SUBLANES), im0:im0 + REC_COLS] = ni
                    return nr, ni

                sr, si = lax.fori_loop(0, steps, step, (sr, si), unroll=min(steps, 8))
                if carry:
                    st_ref[:, re0:re0 + REC_COLS] = sr
                    st_ref[:, im0:im0 + REC_COLS] = si
                    so_ref[0, :, re0:re0 + REC_COLS] = sr
                    so_ref[0, :, im0:im0 + REC_COLS] = si
                else:
                    so_ref[sg, :, re0:re0 + REC_COLS] = sr
                    so_ref[sg, :, im0:im0 + REC_COLS] = si

    ys = [jnp.dot(bu_ref[:, j * SLAB:(j + 1) * SLAB].astype(BF16), cm_ref[j], preferred_element_type=F32)
          for j in range(N_SLABS)]
    y = jnp.concatenate(ys, axis=-1) + d_ref[...] * u
    y = jax.nn.gelu(y).astype(BF16)
    a = jnp.dot(y, wout_ref[...], preferred_element_type=F32)
    xo = x + a[:, :D_MODEL] * jax.nn.sigmoid(a[:, D_MODEL:])
    _emit_tokens(xo, tok_ref, meta_ref, cnt_in_ref, cnt_ref, gf_ref, wr_ref)


def _s5_layer(x_rows, s0, g, win, bmat, a_re, a_im, cmat, dvec, wout, g_ffn, w_router, cnt_in, tok_prev,
              *, n_total_rows, block_off, n_seq_groups, steps, carry):
    assert n_seq_groups * steps * SUBLANES == MIX_ROWS
    n_blocks = x_rows.shape[0] // MIX_ROWS
    s_idx = (lambda i: (0, 0, 0)) if carry else (lambda i: (i, 0, 0))
    kern = functools.partial(_s5_kernel, n_seq_groups=n_seq_groups, steps=steps, carry=carry)
    n_state_groups = 1 if carry else n_blocks * n_seq_groups
    tok_spec, tok_shape = _tokens_out(n_total_rows, block_off)
    aliases, tok_prev = _tokens_alias(tok_prev, 13)
    inputs = [x_rows, g, win, bmat, a_re, a_im, cmat, dvec, wout, s0, g_ffn, w_router, cnt_in, tok_prev]
    return pl.pallas_call(
        kern,
        grid=(n_blocks,),
        in_specs=[
            pl.BlockSpec((MIX_ROWS, D_MODEL), lambda i: (i, 0)),
            _const_spec(1, D_MODEL), _const_spec(D_MODEL, SSM_WIDTH), _const_spec(N_SLABS, SLAB_IN, SLAB),
            _const_spec(1, STATE_COLS), _const_spec(1, STATE_COLS), _const_spec(N_SLABS, SLAB, SLAB_IN),
            _const_spec(1, SSM_WIDTH), _const_spec(SSM_WIDTH, 2 * D_MODEL),
            pl.BlockSpec((n_seq_groups, SUBLANES, STATE_COLS), s_idx),
            _const_spec(1, D_MODEL), _const_spec(D_MODEL, 2 * ROUTER_LANES), _const_spec(1, LANES),
            pl.BlockSpec(memory_space=pl.ANY),
        ],
        out_specs=[
            tok_spec, _META_SPEC,
            pl.BlockSpec((n_seq_groups, SUBLANES, STATE_COLS), s_idx),
            _const_spec(1, LANES),
        ],
        out_shape=[
            tok_shape, jax.ShapeDtypeStruct((META_ROWS, n_blocks * MIX_ROWS), F32),
            jax.ShapeDtypeStruct((n_state_groups, SUBLANES, STATE_COLS), F32),
            jax.ShapeDtypeStruct((1, LANES), F32),
        ],
        input_output_aliases=aliases,
        scratch_shapes=[pltpu.VMEM((MIX_ROWS, STATE_COLS), F32), pltpu.VMEM((SUBLANES, STATE_COLS), F32)],
        compiler_params=pltpu.CompilerParams(
            dimension_semantics=("arbitrary",), vmem_limit_bytes=VMEM_LIMIT_BYTES),
        name="s5_carry" if carry else "s5_step",
    )(*inputs)


def _sgu_kernel(pos_ref, src_hbm, g_ref, win_ref, ng_ref, nb_ref, ws_ref, bs_ref, wout_ref,
                gf_ref, wr_ref, cnt_in_ref, tok_alias_ref, *refs, n_blocks, block_off, block_diag, emit_v):
    del tok_alias_ref
    tok_ref, meta_ref, cnt_ref = refs[:3]
    v_ref = refs[3] if emit_v else None
    xbuf, sems = refs[-2:]
    x = _gathered_rows(pos_ref, block_off, n_blocks, src_hbm, xbuf, sems, MIX_ROWS)
    rows = x.shape[0]
    h = _rms(x, g_ref[...]).astype(BF16)
    hh = jax.nn.gelu(jnp.dot(h, win_ref[...], preferred_element_type=F32))
    u = hh[:, :SGU_WIDTH]
    v = hh[:, SGU_WIDTH:]
    vc = v - jnp.mean(v, axis=-1, keepdims=True)
    v = vc * lax.rsqrt(jnp.mean(vc * vc, axis=-1, keepdims=True) + EPS) * ng_ref[...] + nb_ref[...]
    if emit_v:
        v_ref[...] = v
    r = lax.broadcasted_iota(jnp.int32, (CHUNK, CHUNK), 0)
    c = lax.broadcasted_iota(jnp.int32, (CHUNK, CHUNK), 1)
    if block_diag:
        keep = (r // SUBLANES == c // SUBLANES) & (r >= c)
    else:
        keep = r >= c
    vb = v.astype(BF16)
    n_chunks = rows // CHUNK
    zs = []
    for hd in range(SGU_HEADS):
        wm = jnp.where(keep, ws_ref[hd], 0.0).astype(BF16)
        cols = slice(hd * SGU_HEAD_DIM, (hd + 1) * SGU_HEAD_DIM)
        vcat = jnp.concatenate([vb[ck * CHUNK:(ck + 1) * CHUNK, cols] for ck in range(n_chunks)], axis=-1)
        zcat = jnp.dot(wm, vcat, preferred_element_type=F32)
        zs.append([zcat[:, ck * SGU_HEAD_DIM:(ck + 1) * SGU_HEAD_DIM] for ck in range(n_chunks)])
    z = jnp.concatenate(
        [jnp.concatenate([zs[hd][ck] for hd in range(SGU_HEADS)], axis=-1) + bs_ref[...]
         for ck in range(n_chunks)], axis=0)
    o = jnp.dot((u * z).astype(BF16), wout_ref[...], preferred_element_type=F32)
    _emit_tokens(x + o, tok_ref, meta_ref, cnt_in_ref, cnt_ref, gf_ref, wr_ref)


def _sgu_layer(pos, src_tokens, g, win, ng, nb, ws, bs, wout, g_ffn, w_router, cnt_in, tok_prev,
               *, n_total_rows, n_blocks, block_off, block_diag, emit_v):
    tok_spec, tok_shape = _tokens_out(n_total_rows, block_off)
    out_shape = [tok_shape, jax.ShapeDtypeStruct((META_ROWS, n_blocks * MIX_ROWS), F32),
                 jax.ShapeDtypeStruct((1, LANES), F32)]
    out_specs = [tok_spec, _META_SPEC, _const_spec(1, LANES)]
    if emit_v:
        out_shape.append(jax.ShapeDtypeStruct((n_blocks * MIX_ROWS, SGU_WIDTH), F32))
        out_specs.append(pl.BlockSpec((MIX_ROWS, SGU_WIDTH), lambda i, *_: (i, 0)))
    aliases, tok_prev = _tokens_alias(tok_prev, 12)
    inputs = [src_tokens, g, win, ng, nb, ws, bs, wout, g_ffn, w_router, cnt_in, tok_prev]
    return pl.pallas_call(
        functools.partial(_sgu_kernel, n_blocks=n_blocks, block_off=block_off, block_diag=block_diag,
                          emit_v=emit_v),
        grid_spec=pltpu.PrefetchScalarGridSpec(
            num_scalar_prefetch=1,
            grid=(n_blocks,),
            in_specs=[pl.BlockSpec(memory_space=pl.ANY), _const_spec(1, D_MODEL),
                      _const_spec(D_MODEL, 2 * SGU_WIDTH), _const_spec(1, SGU_WIDTH), _const_spec(1, SGU_WIDTH),
                      _const_spec(SGU_HEADS, CHUNK, CHUNK), _const_spec(CHUNK, SGU_WIDTH),
                      _const_spec(SGU_WIDTH, D_MODEL), _const_spec(1, D_MODEL),
                      _const_spec(D_MODEL, 2 * ROUTER_LANES), _const_spec(1, LANES),
                      pl.BlockSpec(memory_space=pl.ANY)],
            out_specs=out_specs,
            scratch_shapes=[pltpu.VMEM((GATHER_DEPTH, MIX_ROWS * TOKEN_CHUNKS, LANES), F32),
                            pltpu.SemaphoreType.DMA((GATHER_DEPTH,))],
        ),
        out_shape=out_shape,
        input_output_aliases=aliases,
        compiler_params=pltpu.CompilerParams(
            dimension_semantics=("arbitrary",), vmem_limit_bytes=VMEM_LIMIT_BYTES),
        name="sgu_block_diag" if block_diag else "sgu_chunked",
    )(pos, *inputs)


def _moe_plan(meta, cnt, n_tiles_max):
    i32 = jnp.int32
    n_tok = meta.shape[1]
    bucket = meta[META_BUCKET].astype(i32)
    rank = meta[META_RANK].astype(i32)
    n = cnt[0, :N_BUCKETS].astype(i32)
    nt = (n + MOE_TILE - 1) // MOE_TILE
    tile_end = jnp.cumsum(nt)
    tile_start = tile_end - nt
    pos = tile_start[bucket] * MOE_TILE + rank
    n_tiles = tile_end[-1]
    j = jnp.minimum(jnp.arange(n_tiles_max, dtype=i32), jnp.maximum(n_tiles - 1, 0))
    tb = jnp.minimum(jnp.sum((j[:, None] >= tile_end[None, :]).astype(i32), axis=1), N_BUCKETS - 1)
    grp = tb // N_PAIRS
    pair = tb % N_PAIRS
    e_a = grp * EXPERTS_PER_GROUP + jnp.asarray(PAIR_A, i32)[pair]
    e_b = grp * EXPERTS_PER_GROUP + jnp.asarray(PAIR_B, i32)[pair]
    pad_lo = jnp.minimum(tile_start * MOE_TILE + n, (n_tiles_max - 1) * MOE_TILE)
    bits = lambda w: lax.bitcast_convert_type(w, i32)
    token_of_slot, gate_a, gate_b = _invert_plan(
        pos, bits(meta[META_W_A]), bits(meta[META_W_B]), pad_lo, n_tiles_max * MOE_TILE)
    as_f32 = lambda w: lax.bitcast_convert_type(w, F32)[None, :]
    gates = jnp.concatenate(
        [as_f32(gate_a), as_f32(gate_b), jnp.zeros((META_ROWS - 2, n_tiles_max * MOE_TILE), F32)], axis=0)
    return pos, token_of_slot, gates, e_a, e_b, n_tiles.reshape(1)


def _invert_kernel(pos_ref, wa_ref, wb_ref, pad_lo_ref, tok_ref, ga_ref, gb_ref):
    def pad_bucket(b, c):
        base = pad_lo_ref[b]

        def pad(k, c):
            tok_ref[base + k] = 0
            ga_ref[base + k] = 0
            gb_ref[base + k] = 0
            return c

        return lax.fori_loop(0, MOE_TILE, pad, c, unroll=8)

    lax.fori_loop(0, N_BUCKETS, pad_bucket, 0)

    def place(t, c):
        p = pos_ref[t]
        tok_ref[p] = t
        ga_ref[p] = wa_ref[t]
        gb_ref[p] = wb_ref[t]
        return c

    lax.fori_loop(0, pos_ref.shape[0], place, 0, unroll=8)


def _invert_plan(pos, wa_bits, wb_bits, pad_lo, n_slots):
    smem = pl.BlockSpec(memory_space=pltpu.SMEM)
    sds = jax.ShapeDtypeStruct((n_slots,), jnp.int32)
    return pl.pallas_call(
        _invert_kernel, in_specs=[smem] * 4, out_specs=[smem] * 3, out_shape=[sds] * 3, name="moe_invert",
    )(pos, wa_bits, wb_bits, pad_lo)


def _moe_kernel(tok_of_slot_ref, ea_ref, eb_ref, nt_ref, src_hbm, gates_ref, g_ref,
                wg_a, wu_a, wd_a, wg_b, wu_b, wd_b, gf_ref, ys_ref, xbuf, wgu_ref, wd_ref, sems,
                *, final_norm):
    j = pl.program_id(0)
    n_tiles = nt_ref[0]

    @pl.when(j == 0)
    def _():
        _gather_prologue(tok_of_slot_ref, 0, n_tiles, src_hbm, xbuf, sems, MOE_TILE)

    @pl.when(j < n_tiles)
    def _():
        prev = jnp.maximum(j - 1, 0)
        for s, (e_ref, wg, wu, wd) in enumerate(((ea_ref, wg_a, wu_a, wd_a), (eb_ref, wg_b, wu_b, wd_b))):
            @pl.when((j == 0) | (e_ref[j] != e_ref[prev]))
            def _():
                def cast(c, carry, s=s, wg=wg, wu=wu):
                    rows = pl.ds(pl.multiple_of(c * CAST_ROWS, CAST_ROWS), CAST_ROWS)
                    wgu_ref[s, rows, :D_EXPERT] = wg[0, rows, :].astype(BF16)
                    wgu_ref[s, rows, D_EXPERT:] = wu[0, rows, :].astype(BF16)
                    return carry

                lax.fori_loop(0, D_MODEL // CAST_ROWS, cast, 0)
                wd_ref[s] = wd[0].astype(BF16)

        x = _gather_step(j, tok_of_slot_ref, 0, n_tiles, src_hbm, xbuf, sems, MOE_TILE)
        xn = _rms(x, g_ref[...]).astype(BF16)
        gates = jnp.concatenate(
            [gates_ref[...], jnp.zeros((LANES - META_ROWS, MOE_TILE), F32)], axis=0).T
        out = None
        for s in range(2):
            gu = jnp.dot(xn, wgu_ref[s], preferred_element_type=F32)
            hcur = (jax.nn.silu(gu[:, :D_EXPERT]) * gu[:, D_EXPERT:] * gates[:, s:s + 1]).astype(BF16)
            o = jnp.dot(hcur, wd_ref[s], preferred_element_type=F32)
            out = o if out is None else out + o
        y = x + out
        if final_norm:
            y = _rms(y, gf_ref[...])
        _rows_to_tokens(y, ys_ref)

        @pl.when(j == n_tiles - 1)
        def _():
            _gather_epilogue(j, src_hbm, xbuf, sems)


def _moe_experts(tokens, token_of_slot, gates, e_a, e_b, n_tiles, g, w_gate, w_up, w_down, g_final,
                 *, final_norm):
    n_tiles_max = e_a.shape[0]
    last = lambda j, nt: jnp.minimum(j, jnp.maximum(nt[0] - 1, 0))
    w_in = lambda which: pl.BlockSpec(
        (1, D_MODEL, D_EXPERT), lambda j, tos, ea, eb, nt: ((ea, eb)[which][j], 0, 0))
    w_out = lambda which: pl.BlockSpec(
        (1, D_EXPERT, D_MODEL), lambda j, tos, ea, eb, nt: ((ea, eb)[which][j], 0, 0))
    return pl.pallas_call(
        functools.partial(_moe_kernel, final_norm=final_norm),
        grid_spec=pltpu.PrefetchScalarGridSpec(
            num_scalar_prefetch=4,
            grid=(n_tiles_max,),
            in_specs=[pl.BlockSpec(memory_space=pl.ANY),
                      pl.BlockSpec((META_ROWS, MOE_TILE), lambda j, tos, ea, eb, nt: (0, last(j, nt))),
                      _const_spec(1, D_MODEL), w_in(0), w_in(0), w_out(0), w_in(1), w_in(1), w_out(1),
                      _const_spec(1, D_MODEL)],
            out_specs=pl.BlockSpec((MOE_TILE * TOKEN_CHUNKS, LANES),
                                   lambda j, tos, ea, eb, nt: (last(j, nt), 0)),
            scratch_shapes=[pltpu.VMEM((GATHER_DEPTH, MOE_TILE * TOKEN_CHUNKS, LANES), F32),
                            pltpu.VMEM((2, D_MODEL, 2 * D_EXPERT), BF16), pltpu.VMEM((2, D_EXPERT, D_MODEL), BF16),
                            pltpu.SemaphoreType.DMA((GATHER_DEPTH,))],
        ),
        out_shape=jax.ShapeDtypeStruct((n_tiles_max * MOE_TILE * TOKEN_CHUNKS, LANES), F32),
        compiler_params=pltpu.CompilerParams(
            dimension_semantics=("arbitrary",), vmem_limit_bytes=VMEM_LIMIT_BYTES),
        name="moe_experts_final" if final_norm else "moe_experts",
    )(token_of_slot, e_a, e_b, n_tiles, tokens, gates, g, w_gate, w_up, w_down, w_gate, w_up, w_down, g_final)


def _moe_layer(tokens, meta, cnt, g, w_gate, w_up, w_down, g_final, *, final_norm):
    n_tiles_max = meta.shape[1] // MOE_TILE + N_BUCKETS
    pos, token_of_slot, gates, e_a, e_b, n_tiles = _moe_plan(meta, cnt, n_tiles_max)
    ys = _moe_experts(tokens, token_of_slot, gates, e_a, e_b, n_tiles, g, w_gate, w_up, w_down, g_final,
                      final_norm=final_norm)
    return ys, pos


def _ungather_kernel(pos_ref, src_hbm, *refs, n_blocks, n_first_blocks):
    o_refs, (xbuf, sems) = refs[:-2], refs[-2:]
    i = pl.program_id(0)
    x = _gathered_rows(pos_ref, 0, n_blocks, src_hbm, xbuf, sems, MIX_ROWS)

    @pl.when(i < n_first_blocks)
    def _():
        o_refs[0][...] = x

    @pl.when(i >= n_first_blocks)
    def _():
        o_refs[1][...] = x


def _ungather(ys, pos, n_first_rows):
    n_blocks = pos.shape[0] // MIX_ROWS
    n_first = n_first_rows // MIX_ROWS
    spec = lambda f: pl.BlockSpec((MIX_ROWS, D_MODEL), f)
    return pl.pallas_call(
        functools.partial(_ungather_kernel, n_blocks=n_blocks, n_first_blocks=n_first),
        grid_spec=pltpu.PrefetchScalarGridSpec(
            num_scalar_prefetch=1,
            grid=(n_blocks,),
            in_specs=[pl.BlockSpec(memory_space=pl.ANY)],
            out_specs=[spec(lambda i, pos: (jnp.minimum(i, n_first - 1), 0)),
                       spec(lambda i, pos: (jnp.maximum(i - n_first, 0), 0))],
            scratch_shapes=[pltpu.VMEM((GATHER_DEPTH, MIX_ROWS * TOKEN_CHUNKS, LANES), F32),
                            pltpu.SemaphoreType.DMA((GATHER_DEPTH,))],
        ),
        out_shape=[jax.ShapeDtypeStruct((n_first_rows, D_MODEL), F32),
                   jax.ShapeDtypeStruct((pos.shape[0] - n_first_rows, D_MODEL), F32)],
        compiler_params=pltpu.CompilerParams(
            dimension_semantics=("arbitrary",), vmem_limit_bytes=VMEM_LIMIT_BYTES),
        name="moe_ungather",
    )(pos, ys)


def _router_weights(w_rg, w_re):
    pad = ROUTER_LANES - N_EXPERT_GROUPS - N_EXPERTS
    w = jnp.concatenate([w_rg.astype(F32), w_re.astype(F32), jnp.zeros((D_MODEL, pad), F32)], axis=-1)
    hi = w.astype(BF16)
    lo = (w - hi.astype(F32)).astype(BF16)
    return jnp.concatenate([hi, lo], axis=-1)


def kernel(x_prompt, x_sample, state_ssm_re, state_ssm_im, norm_mix, norm_ffn, norm_final, ssm_w_in, ssm_lambda_re, ssm_lambda_im, ssm_log_step, ssm_b_re, ssm_b_im, ssm_c_re, ssm_c_im, ssm_d, ssm_w_out, sgu_w_in, sgu_norm_g, sgu_norm_b, sgu_w_s, sgu_b_s, sgu_w_out, moe_router_group, moe_router_expert, moe_w_gate, moe_w_up, moe_w_down):
    nb, ns, dm = x_prompt.shape
    db, ds, _ = x_sample.shape
    n_p, n_s = nb * ns, db * ds
    n_tok = n_p + n_s
    n_sg = db // SUBLANES
    steps_p = MIX_ROWS // SUBLANES
    sg_per_block = MIX_ROWS // (ds * SUBLANES)
    assert nb == SUBLANES and dm == D_MODEL and ns % steps_p == 0 and n_sg % sg_per_block == 0
    assert n_p % MIX_ROWS == 0 and n_s % MIX_ROWS == 0 and MIX_ROWS % MOE_TILE == 0 and CHUNK % ds == 0
    row = lambda a: a.astype(F32).reshape(1, -1)
    zero_cnt = jnp.zeros((1, LANES), F32)

    lb_re, lb_im, bb_re, bb_im = _s5_prep(ssm_lambda_re, ssm_lambda_im, ssm_log_step, ssm_b_re, ssm_b_im)
    bmat, cmat, a_re, a_im, dvec = _s5_matrices(lb_re, lb_im, bb_re, bb_im, ssm_c_re, ssm_c_im, ssm_d)
    xp = jnp.transpose(x_prompt, (1, 0, 2)).reshape(n_p, dm)
    xs = jnp.transpose(x_sample.reshape(n_sg, SUBLANES, ds, dm), (0, 2, 1, 3)).reshape(n_s, dm)
    zero_state = jnp.zeros((1, SUBLANES, STATE_COLS), F32)
    s0 = _pack_state(state_ssm_re.astype(F32), state_ssm_im.astype(F32)).reshape(n_sg, SUBLANES, STATE_COLS)
    wr0 = _router_weights(moe_router_group[0], moe_router_expert[0])
    s5 = functools.partial(_s5_layer, g=row(norm_mix[0]), win=ssm_w_in.astype(BF16), bmat=bmat, a_re=a_re,
                           a_im=a_im, cmat=cmat, dvec=dvec, wout=ssm_w_out.astype(BF16),
                           g_ffn=row(norm_ffn[0]), w_router=wr0, n_total_rows=n_tok)
    tok0, meta_p, st_p, cnt0 = s5(xp, zero_state, cnt_in=zero_cnt, tok_prev=None, block_off=0,
                                  n_seq_groups=1, steps=steps_p, carry=True)
    tok0, meta_s, st_s, cnt0 = s5(xs, s0, cnt_in=cnt0, tok_prev=tok0, block_off=n_p // MIX_ROWS,
                                  n_seq_groups=sg_per_block, steps=ds, carry=False)
    ssm_re_p, ssm_im_p = _unpack_state(st_p.reshape(nb, STATE_COLS))
    ssm_re_s, ssm_im_s = _unpack_state(st_s.reshape(db, STATE_COLS))
    ys0, pos0 = _moe_layer(tok0, jnp.concatenate([meta_p, meta_s], axis=1), cnt0, row(norm_ffn[0]),
                           moe_w_gate[0], moe_w_up[0], moe_w_down[0], row(norm_final), final_norm=False)

    pos0 = jnp.concatenate([
        pos0[:n_p].reshape(ns, nb).T.reshape(-1),
        jnp.transpose(pos0[n_p:].reshape(n_sg, ds, SUBLANES), (0, 2, 1)).reshape(-1)])
    ws_p = sgu_w_s.astype(F32)[:, :CHUNK, :CHUNK]
    bs_p = jnp.repeat(sgu_b_s.astype(F32)[:, :CHUNK].T, SGU_HEAD_DIM, axis=1)
    reps = CHUNK // ds
    ws_s = jnp.tile(sgu_w_s.astype(F32)[:, :ds, :ds], (1, reps, reps))
    bs_s = jnp.tile(jnp.repeat(sgu_b_s.astype(F32)[:, :ds].T, SGU_HEAD_DIM, axis=1), (reps, 1))
    wr1 = _router_weights(moe_router_group[1], moe_router_expert[1])
    sgu = functools.partial(_sgu_layer, pos0, ys0, g=row(norm_mix[1]), win=sgu_w_in.astype(BF16),
                            ng=row(sgu_norm_g), nb=row(sgu_norm_b), wout=sgu_w_out.astype(BF16),
                            g_ffn=row(norm_ffn[1]), w_router=wr1, n_total_rows=n_tok)
    tok1, meta_p, cnt1 = sgu(ws=ws_p, bs=bs_p, cnt_in=zero_cnt, tok_prev=None, n_blocks=n_p // MIX_ROWS,
                             block_off=0, block_diag=False, emit_v=False)
    tok1, meta_s, cnt1, v_s = sgu(ws=ws_s, bs=bs_s, cnt_in=cnt1, tok_prev=tok1, n_blocks=n_s // MIX_ROWS,
                                  block_off=n_p // MIX_ROWS, block_diag=True, emit_v=True)
    ys1, pos1 = _moe_layer(tok1, jnp.concatenate([meta_p, meta_s], axis=1), cnt1, row(norm_ffn[1]),
                           moe_w_gate[1], moe_w_up[1], moe_w_down[1], row(norm_final), final_norm=True)
    y_p, y_s = _ungather(ys1, pos1, n_p)
    return (y_p.reshape(nb, ns, dm), y_s.reshape(db, ds, dm), ssm_re_p, ssm_im_p, ssm_re_s, ssm_im_s,
            v_s.reshape(db, ds, SGU_WIDTH))
```

```python
import functools

import jax
import jax.numpy as jnp
from jax import lax
from jax.experimental import pallas as pl
from jax.experimental.pallas import tpu as pltpu

D_MODEL = 1024
SSM_WIDTH = D_MODEL // 2
SSM_GROUP = 16
SSM_GROUPS = SSM_WIDTH // SSM_GROUP
SSM_STATE = 64
SGU_WIDTH = D_MODEL
SGU_HEADS = 8
SGU_HEAD_DIM = SGU_WIDTH // SGU_HEADS
CHUNK = 128
N_EXPERT_GROUPS = 4
EXPERTS_PER_GROUP = 4
N_EXPERTS = N_EXPERT_GROUPS * EXPERTS_PER_GROUP
D_EXPERT = D_MODEL // 4
EPS = 1e-6

SUBLANES = 8
LANES = 128
VMEM_LIMIT_BYTES = 56 * 1024 * 1024
N_DMA_PRIORITIES = 2

SLAB_GROUPS = 16
N_SLABS = SSM_GROUPS // SLAB_GROUPS
SLAB_HALF = SLAB_GROUPS * SSM_STATE
SLAB = 2 * SLAB_HALF
STATE_COLS = N_SLABS * SLAB
SLAB_IN = SLAB_GROUPS * SSM_GROUP
REC_COLS = 512

TOKEN_CHUNKS = D_MODEL // LANES
assert TOKEN_CHUNKS == SUBLANES

PAIR_A = (0, 0, 0, 1, 1, 3)
PAIR_B = (1, 2, 3, 3, 2, 2)
N_PAIRS = len(PAIR_A)
GATHER_DEPTH = 3
GATHER_UNROLL = 16
CAST_ROWS = 256
N_BUCKETS = N_EXPERT_GROUPS * N_PAIRS
MOE_TILE = 256
MIX_ROWS = 512
META_ROWS = SUBLANES
META_BUCKET, META_W_A, META_W_B, META_RANK = 0, 1, 2, 3
ROUTER_LANES = LANES
BF16 = jnp.bfloat16
F32 = jnp.float32


def _rms(x, g):
    return x * lax.rsqrt(jnp.mean(x * x, axis=-1, keepdims=True) + EPS) * g


def _s5_prep_kernel(lr_ref, li_ref, ls_ref, br_ref, bi_ref, lbr_ref, lbi_ref, bbr_ref, bbi_ref):
    lr = lr_ref[...]
    li = li_ref[...]
    dt = jnp.exp(ls_ref[...])
    mag = jnp.exp(lr * dt)
    ang = li * dt
    lb_re = mag * jnp.cos(ang)
    lb_im = mag * jnp.sin(ang)
    den = lr * lr + li * li
    nr = lb_re - 1.0
    coef_re = (nr * lr + lb_im * li) / den
    coef_im = (lb_im * lr - nr * li) / den
    br = br_ref[...]
    bi = bi_ref[...]
    lbr_ref[...] = lb_re
    lbi_ref[...] = lb_im
    bbr_ref[...] = coef_re * br - coef_im * bi
    bbi_ref[...] = coef_re * bi + coef_im * br


def _s5_prep(lam_re, lam_im, log_step, b_re, b_im):
    rows = SSM_GROUPS * SSM_GROUP
    rep = lambda a: jnp.repeat(a.astype(F32), SSM_GROUP, axis=0)
    lr = rep(lam_re)
    li = rep(lam_im)
    ls = rep(jnp.broadcast_to(log_step.astype(F32)[:, None], (SSM_GROUPS, SSM_STATE)))
    br = jnp.transpose(b_re.astype(F32), (0, 2, 1)).reshape(rows, SSM_STATE)
    bi = jnp.transpose(b_im.astype(F32), (0, 2, 1)).reshape(rows, SSM_STATE)
    sds = jax.ShapeDtypeStruct((rows, SSM_STATE), F32)
    lb_re, lb_im, bb_re, bb_im = pl.pallas_call(
        _s5_prep_kernel, out_shape=(sds, sds, sds, sds), name="s5_prep")(lr, li, ls, br, bi)
    return lb_re, lb_im, bb_re, bb_im


def _to_state_cols(a):
    return a.reshape(a.shape[:-2] + (N_SLABS, SLAB_HALF))


def _pack_state(re, im):
    return jnp.concatenate([_to_state_cols(re), _to_state_cols(im)], axis=-1).reshape(re.shape[0], STATE_COLS)


def _unpack_state(s):
    s = s.reshape(s.shape[0], N_SLABS, 2, SLAB_GROUPS, SSM_STATE)
    re = s[:, :, 0].reshape(s.shape[0], SSM_GROUPS, SSM_STATE)
    im = s[:, :, 1].reshape(s.shape[0], SSM_GROUPS, SSM_STATE)
    return re, im


def _s5_matrices(lb_re, lb_im, bb_re, bb_im, c_re, c_im, d):
    eye = jnp.eye(SLAB_GROUPS, dtype=F32)
    def in_side(bb):
        bb = bb.reshape(N_SLABS, SLAB_GROUPS, SSM_GROUP, SSM_STATE)
        m = bb[:, :, :, None, :] * eye[None, :, None, :, None]
        return m.reshape(N_SLABS, SLAB_IN, SLAB_HALF)
    bmat = jnp.concatenate([in_side(bb_re), in_side(bb_im)], axis=-1).astype(BF16)
    def out_side(c):
        c = jnp.transpose(c.astype(F32), (0, 2, 1)).reshape(N_SLABS, SLAB_GROUPS, SSM_STATE, SSM_GROUP)
        m = c[:, :, :, None, :] * eye[None, :, None, :, None]
        return m.reshape(N_SLABS, SLAB_HALF, SLAB_IN)
    cmat = jnp.concatenate([out_side(c_re), -out_side(c_im)], axis=1).astype(BF16)
    lam = lambda lb: _to_state_cols(lb.reshape(SSM_GROUPS, SSM_GROUP, SSM_STATE)[:, 0])
    a_re = jnp.concatenate([lam(lb_re), lam(lb_re)], axis=-1).reshape(1, STATE_COLS)
    a_im = jnp.concatenate([lam(lb_im), lam(lb_im)], axis=-1).reshape(1, STATE_COLS)
    return bmat, cmat, a_re, a_im, d.astype(F32).reshape(1, SSM_WIDTH)


def _rows_to_tokens(x, tok_ref):
    n = x.shape[0]
    for c in range(TOKEN_CHUNKS):
        tok_ref[pl.ds(c, n, stride=TOKEN_CHUNKS), :] = x[:, c * LANES:(c + 1) * LANES]


def _tokens_to_rows(tok_ref, n):
    return jnp.concatenate([tok_ref[pl.ds(c, n, stride=TOKEN_CHUNKS), :] for c in range(TOKEN_CHUNKS)], axis=-1)


def _start_token_gather(idx_ref, base, src_hbm, buf_ref, sem, n):
    def issue(blk, c):
        for k in range(GATHER_UNROLL):
            r = blk * GATHER_UNROLL + k
            tok = idx_ref[base + r]
            pltpu.make_async_copy(
                src_hbm.at[pl.ds(pl.multiple_of(tok * TOKEN_CHUNKS, TOKEN_CHUNKS), TOKEN_CHUNKS), :],
                buf_ref.at[pl.ds(pl.multiple_of(r * TOKEN_CHUNKS, TOKEN_CHUNKS), TOKEN_CHUNKS), :],
                sem).start(priority=k % N_DMA_PRIORITIES)
        return c

    lax.fori_loop(0, n // GATHER_UNROLL, issue, 0)


def _wait_token_gather(src_hbm, buf_ref, sem):
    pltpu.make_async_copy(src_hbm.at[pl.ds(0, buf_ref.shape[0]), :], buf_ref, sem).wait()


def _gather_prologue(idx_ref, first_block, n_blocks, src_hbm, buf_ref, sems, n):
    for k in range(GATHER_DEPTH - 1):
        blk = first_block + jnp.minimum(k, n_blocks - 1)
        _start_token_gather(idx_ref, blk * n, src_hbm, buf_ref.at[k], sems.at[k], n)


def _gather_step(i, idx_ref, first_block, n_blocks, src_hbm, buf_ref, sems, n):
    slot = lax.rem(i, GATHER_DEPTH)
    ahead = lax.rem(i + GATHER_DEPTH - 1, GATHER_DEPTH)
    _wait_token_gather(src_hbm, buf_ref.at[slot], sems.at[slot])
    blk = first_block + jnp.minimum(i + GATHER_DEPTH - 1, n_blocks - 1)
    _start_token_gather(idx_ref, blk * n, src_hbm, buf_ref.at[ahead], sems.at[ahead], n)
    return _tokens_to_rows(buf_ref.at[slot], n)


def _gather_epilogue(i, src_hbm, buf_ref, sems):
    for k in range(1, GATHER_DEPTH):
        slot = lax.rem(i + k, GATHER_DEPTH)
        _wait_token_gather(src_hbm, buf_ref.at[slot], sems.at[slot])


def _gathered_rows(idx_ref, first_block, n_blocks, src_hbm, buf_ref, sems, n):
    i = pl.program_id(0)

    @pl.when(i == 0)
    def _():
        _gather_prologue(idx_ref, first_block, n_blocks, src_hbm, buf_ref, sems, n)

    x = _gather_step(i, idx_ref, first_block, n_blocks, src_hbm, buf_ref, sems, n)

    @pl.when(i == n_blocks - 1)
    def _():
        _gather_epilogue(i, src_hbm, buf_ref, sems)

    return x


def _route_meta(x, g, wr, cnt_ref):
    xn = _rms(x, g)
    xh = xn.astype(BF16)
    xl = (xn - xh.astype(F32)).astype(BF16)
    p = jnp.dot(xh, wr, preferred_element_type=F32)
    q = jnp.dot(xl, wr[:, :ROUTER_LANES], preferred_element_type=F32)
    logits = p[:, :ROUTER_LANES] + (p[:, ROUTER_LANES:] + q)
    rows = logits.shape[0]
    lane = lax.broadcasted_iota(jnp.int32, logits.shape, 1)
    neg = jnp.float32(-jnp.inf)
    big = jnp.int32(ROUTER_LANES)
    is_g = lane < N_EXPERT_GROUPS
    lg = jnp.where(is_g, logits, neg)
    mg = jnp.max(lg, axis=-1, keepdims=True)
    g_idx = jnp.min(jnp.where(lg == mg, lane, big), axis=-1, keepdims=True)
    g_w = 1.0 / jnp.sum(jnp.where(is_g, jnp.exp(lg - mg), 0.0), axis=-1, keepdims=True)
    first = N_EXPERT_GROUPS + g_idx * EXPERTS_PER_GROUP
    in_grp = (lane >= first) & (lane < first + EXPERTS_PER_GROUP)
    le = jnp.where(in_grp, logits, neg)
    m1 = jnp.max(le, axis=-1, keepdims=True)
    i1 = jnp.min(jnp.where(le == m1, lane, big), axis=-1, keepdims=True)
    le2 = jnp.where(lane == i1, neg, le)
    m2 = jnp.max(le2, axis=-1, keepdims=True)
    i2 = jnp.min(jnp.where(le2 == m2, lane, big), axis=-1, keepdims=True)
    e2 = jnp.exp(m2 - m1)
    w1 = g_w / (1.0 + e2)
    w2 = g_w * e2 / (1.0 + e2)
    lo = jnp.minimum(i1, i2) - first
    hi = jnp.maximum(i1, i2) - first
    pair = jnp.where(lo == 0, hi - 1, jnp.where(lo == 1, 6 - hi, 5))
    bucket = g_idx * N_PAIRS + pair
    w_lo = jnp.where(i1 < i2, w1, w2)
    w_hi = jnp.where(i1 < i2, w2, w1)
    w_a = jnp.where(pair == 5, w_hi, w_lo)
    w_b = jnp.where(pair == 5, w_lo, w_hi)
    onehot = lane == bucket
    oh = jnp.where(onehot, 1.0, 0.0)
    r_i = lax.broadcasted_iota(jnp.int32, (rows, rows), 0)
    c_i = lax.broadcasted_iota(jnp.int32, (rows, rows), 1)
    before = jnp.where(r_i > c_i, 1.0, 0.0).astype(BF16)
    prefix = jnp.dot(before, oh.astype(BF16), preferred_element_type=F32)
    cnt = cnt_ref[...]
    rank = jnp.sum(jnp.where(onehot, prefix + cnt, 0.0), axis=-1, keepdims=True)
    cnt_ref[...] = cnt + jnp.sum(oh, axis=0, keepdims=True)
    return jnp.where(lane == META_BUCKET, bucket.astype(F32),
                     jnp.where(lane == META_W_A, w_a,
                               jnp.where(lane == META_W_B, w_b,
                                         jnp.where(lane == META_RANK, rank, 0.0))))


def _emit_tokens(x, tok_ref, meta_ref, cnt_in_ref, cnt_ref, gf_ref, wr_ref):
    @pl.when(pl.program_id(0) == 0)
    def _():
        cnt_ref[...] = cnt_in_ref[...]
    meta = _route_meta(x, gf_ref[...], wr_ref[...], cnt_ref)
    _rows_to_tokens(x, tok_ref)
    meta_ref[...] = meta.T[:META_ROWS, :]


def _const_spec(*shape):
    return pl.BlockSpec(shape, lambda i, *_: (0,) * len(shape))


_META_SPEC = pl.BlockSpec((META_ROWS, MIX_ROWS), lambda i, *_: (0, i))


def _tokens_alias(tok_prev, input_index):
    if tok_prev is None:
        return {}, jnp.zeros((SUBLANES, LANES), F32)
    return {input_index: 0}, tok_prev


def _tokens_out(n_total_rows, block_off):
    return (pl.BlockSpec((MIX_ROWS * TOKEN_CHUNKS, LANES), lambda i, *_: (i + block_off, 0)),
            jax.ShapeDtypeStruct((n_total_rows * TOKEN_CHUNKS, LANES), F32))


def _time_major_copies(x_hbm, xt_ref, sem, step, slot, n_seq_groups, steps):
    g0 = step * n_seq_groups if n_seq_groups > 1 else 0
    t0 = 0 if n_seq_groups > 1 else step * steps
    return [pltpu.make_async_copy(x_hbm.at[pl.ds(g0, n_seq_groups), b, pl.ds(t0, steps), :],
                                  xt_ref.at[slot, :, :, b, :], sem.at[slot]) for b in range(SUBLANES)]


def _s5_kernel(x_hbm, g_ref, win_ref, bm_ref, are_ref, aim_ref, cm_ref, d_ref, wout_ref, s0_ref,
               gf_ref, wr_ref, cnt_in_ref, tok_alias_ref,
               tok_ref, meta_ref, so_ref, cnt_ref, bu_ref, st_ref, xt_ref, xsem,
               *, n_blocks, n_seq_groups, steps, carry):
    del tok_alias_ref
    i = pl.program_id(0)
    slot = lax.rem(i, 2)
    copies = functools.partial(_time_major_copies, x_hbm, xt_ref, xsem, n_seq_groups=n_seq_groups, steps=steps)

    @pl.when(i == 0)
    def _():
        for c in copies(0, 0):
            c.start()

    @pl.when(i + 1 < n_blocks)
    def _():
        for c in copies(i + 1, 1 - slot):
            c.start()

    for c in copies(i, slot):
        c.wait()
    x = xt_ref[slot].reshape(MIX_ROWS, D_MODEL)
    h = _rms(x, g_ref[...]).astype(BF16)
    u = jnp.dot(h, win_ref[...], preferred_element_type=F32)
    ub = u.astype(BF16)
    for j in range(N_SLABS):
        bu_ref[:, j * SLAB:(j + 1) * SLAB] = jnp.dot(
            ub[:, j * SLAB_IN:(j + 1) * SLAB_IN], bm_ref[j], preferred_element_type=F32)

    if carry:
        @pl.when(pl.program_id(0) == 0)
        def _():
            st_ref[...] = s0_ref[0]

    for sg in range(n_seq_groups):
        for j in range(N_SLABS):
            for k in range(SLAB_HALF // REC_COLS):
                re0 = j * SLAB + k * REC_COLS
                im0 = re0 + SLAB_HALF
                ar = jnp.broadcast_to(are_ref[:, re0:re0 + REC_COLS], (SUBLANES, REC_COLS))
                ai = jnp.broadcast_to(aim_ref[:, re0:re0 + REC_COLS], (SUBLANES, REC_COLS))
                if carry:
                    sr = st_ref[:, re0:re0 + REC_COLS]
                    si = st_ref[:, im0:im0 + REC_COLS]
                else:
                    sr = s0_ref[sg, :, re0:re0 + REC_COLS]
                    si = s0_ref[sg, :, im0:im0 + REC_COLS]

                def step(t, c, re0=re0, im0=im0, ar=ar, ai=ai, sg=sg):
                    sr, si = c
                    row = pl.multiple_of((sg * steps + t) * SUBLANES, SUBLANES)
                    br = bu_ref[pl.ds(row, SUBLANES), re0:re0 + REC_COLS]
                    bi = bu_ref[pl.ds(row, SUBLANES), im0:im0 + REC_COLS]
                    nr = ar * sr - ai * si + br
                    ni = ar * si + ai * sr + bi
                    bu_ref[pl.ds(row, SUBLANES), re0:re0 + REC_COLS] = nr
                    bu_ref[pl.ds(row, SUBLANES), im0:im0 + REC_COLS] = ni
                    return nr, ni

                sr, si = lax.fori_loop(0, steps, step, (sr, si), unroll=min(steps, 8))
                if carry:
                    st_ref[:, re0:re0 + REC_COLS] = sr
                    st_ref[:, im0:im0 + REC_COLS] = si
                    so_ref[0, :, re0:re0 + REC_COLS] = sr
                    so_ref[0, :, im0:im0 + REC_COLS] = si
                else:
                    so_ref[sg, :, re0:re0 + REC_COLS] = sr
                    so_ref[sg, :, im0:im0 + REC_COLS] = si

    ys = [jnp.dot(bu_ref[:, j * SLAB:(j + 1) * SLAB].astype(BF16), cm_ref[j], preferred_element_type=F32)
          for j in range(N_SLABS)]
    y = jnp.concatenate(ys, axis=-1) + d_ref[...] * u
    y = jax.nn.gelu(y).astype(BF16)
    a = jnp.dot(y, wout_ref[...], preferred_element_type=F32)
    xo = x + a[:, :D_MODEL] * jax.nn.sigmoid(a[:, D_MODEL:])
    _emit_tokens(xo, tok_ref, meta_ref, cnt_in_ref, cnt_ref, gf_ref, wr_ref)


def _s5_layer(x_seq, s0, g, win, bmat, a_re, a_im, cmat, dvec, wout, g_ffn, w_router, cnt_in, tok_prev,
              *, n_total_rows, block_off, n_seq_groups, steps, carry):
    assert n_seq_groups * steps * SUBLANES == MIX_ROWS and x_seq.shape[1] == SUBLANES
    assert (n_seq_groups == 1 and x_seq.shape[0] == 1) or steps == x_seq.shape[2]
    n_blocks = x_seq.shape[0] * x_seq.shape[2] // (n_seq_groups * steps)
    s_idx = (lambda i: (0, 0, 0)) if carry else (lambda i: (i, 0, 0))
    kern = functools.partial(_s5_kernel, n_blocks=n_blocks, n_seq_groups=n_seq_groups, steps=steps, carry=carry)
    n_state_groups = 1 if carry else n_blocks * n_seq_groups
    tok_spec, tok_shape = _tokens_out(n_total_rows, block_off)
    aliases, tok_prev = _tokens_alias(tok_prev, 13)
    inputs = [x_seq, g, win, bmat, a_re, a_im, cmat, dvec, wout, s0, g_ffn, w_router, cnt_in, tok_prev]
    return pl.pallas_call(
        kern,
        grid=(n_blocks,),
        in_specs=[
            pl.BlockSpec(memory_space=pl.ANY),
            _const_spec(1, D_MODEL), _const_spec(D_MODEL, SSM_WIDTH), _const_spec(N_SLABS, SLAB_IN, SLAB),
            _const_spec(1, STATE_COLS), _const_spec(1, STATE_COLS), _const_spec(N_SLABS, SLAB, SLAB_IN),
            _const_spec(1, SSM_WIDTH), _const_spec(SSM_WIDTH, 2 * D_MODEL),
            pl.BlockSpec((n_seq_groups, SUBLANES, STATE_COLS), s_idx),
            _const_spec(1, D_MODEL), _const_spec(D_MODEL, 2 * ROUTER_LANES), _const_spec(1, LANES),
            pl.BlockSpec(memory_space=pl.ANY),
        ],
        out_specs=[
            tok_spec, _META_SPEC,
            pl.BlockSpec((n_seq_groups, SUBLANES, STATE_COLS), s_idx),
            _const_spec(1, LANES),
        ],
        out_shape=[
            tok_shape, jax.ShapeDtypeStruct((META_ROWS, n_blocks * MIX_ROWS), F32),
            jax.ShapeDtypeStruct((n_state_groups, SUBLANES, STATE_COLS), F32),
            jax.ShapeDtypeStruct((1, LANES), F32),
        ],
        input_output_aliases=aliases,
        scratch_shapes=[pltpu.VMEM((MIX_ROWS, STATE_COLS), F32), pltpu.VMEM((SUBLANES, STATE_COLS), F32),
                        pltpu.VMEM((2, n_seq_groups, steps, SUBLANES, D_MODEL), F32),
                        pltpu.SemaphoreType.DMA((2,))],
        compiler_params=pltpu.CompilerParams(
            dimension_semantics=("arbitrary",), vmem_limit_bytes=VMEM_LIMIT_BYTES),
        name="s5_carry" if carry else "s5_step",
    )(*inputs)


def _sgu_kernel(pos_ref, src_hbm, g_ref, win_ref, ng_ref, nb_ref, ws_ref, bs_ref, wout_ref,
                gf_ref, wr_ref, cnt_in_ref, tok_alias_ref, *refs, n_blocks, block_off, block_diag, emit_v):
    del tok_alias_ref
    tok_ref, meta_ref, cnt_ref = refs[:3]
    v_ref = refs[3] if emit_v else None
    xbuf, sems = refs[-2:]
    x = _gathered_rows(pos_ref, block_off, n_blocks, src_hbm, xbuf, sems, MIX_ROWS)
    rows = x.shape[0]
    h = _rms(x, g_ref[...]).astype(BF16)
    hh = jax.nn.gelu(jnp.dot(h, win_ref[...], preferred_element_type=F32))
    u = hh[:, :SGU_WIDTH]
    v = hh[:, SGU_WIDTH:]
    vc = v - jnp.mean(v, axis=-1, keepdims=True)
    v = vc * lax.rsqrt(jnp.mean(vc * vc, axis=-1, keepdims=True) + EPS) * ng_ref[...] + nb_ref[...]
    if emit_v:
        v_ref[...] = v
    r = lax.broadcasted_iota(jnp.int32, (CHUNK, CHUNK), 0)
    c = lax.broadcasted_iota(jnp.int32, (CHUNK, CHUNK), 1)
    if block_diag:
        keep = (r // SUBLANES == c // SUBLANES) & (r >= c)
    else:
        keep = r >= c
    vb = v.astype(BF16)
    n_chunks = rows // CHUNK
    zs = []
    for hd in range(SGU_HEADS):
        wm = jnp.where(keep, ws_ref[hd], 0.0).astype(BF16)
        cols = slice(hd * SGU_HEAD_DIM, (hd + 1) * SGU_HEAD_DIM)
        vcat = jnp.concatenate([vb[ck * CHUNK:(ck + 1) * CHUNK, cols] for ck in range(n_chunks)], axis=-1)
        zcat = jnp.dot(wm, vcat, preferred_element_type=F32)
        zs.append([zcat[:, ck * SGU_HEAD_DIM:(ck + 1) * SGU_HEAD_DIM] for ck in range(n_chunks)])
    z = jnp.concatenate(
        [jnp.concatenate([zs[hd][ck] for hd in range(SGU_HEADS)], axis=-1) + bs_ref[...]
         for ck in range(n_chunks)], axis=0)
    o = jnp.dot((u * z).astype(BF16), wout_ref[...], preferred_element_type=F32)
    _emit_tokens(x + o, tok_ref, meta_ref, cnt_in_ref, cnt_ref, gf_ref, wr_ref)


def _sgu_layer(pos, src_tokens, g, win, ng, nb, ws, bs, wout, g_ffn, w_router, cnt_in, tok_prev,
               *, n_total_rows, n_blocks, block_off, block_diag, emit_v):
    tok_spec, tok_shape = _tokens_out(n_total_rows, block_off)
    out_shape = [tok_shape, jax.ShapeDtypeStruct((META_ROWS, n_blocks * MIX_ROWS), F32),
                 jax.ShapeDtypeStruct((1, LANES), F32)]
    out_specs = [tok_spec, _META_SPEC, _const_spec(1, LANES)]
    if emit_v:
        out_shape.append(jax.ShapeDtypeStruct((n_blocks * MIX_ROWS, SGU_WIDTH), F32))
        out_specs.append(pl.BlockSpec((MIX_ROWS, SGU_WIDTH), lambda i, *_: (i, 0)))
    aliases, tok_prev = _tokens_alias(tok_prev, 12)
    inputs = [src_tokens, g, win, ng, nb, ws, bs, wout, g_ffn, w_router, cnt_in, tok_prev]
    return pl.pallas_call(
        functools.partial(_sgu_kernel, n_blocks=n_blocks, block_off=block_off, block_diag=block_diag,
                          emit_v=emit_v),
        grid_spec=pltpu.PrefetchScalarGridSpec(
            num_scalar_prefetch=1,
            grid=(n_blocks,),
            in_specs=[pl.BlockSpec(memory_space=pl.ANY), _const_spec(1, D_MODEL),
                      _const_spec(D_MODEL, 2 * SGU_WIDTH), _const_spec(1, SGU_WIDTH), _const_spec(1, SGU_WIDTH),
                      _const_spec(SGU_HEADS, CHUNK, CHUNK), _const_spec(CHUNK, SGU_WIDTH),
                      _const_spec(SGU_WIDTH, D_MODEL), _const_spec(1, D_MODEL),
                      _const_spec(D_MODEL, 2 * ROUTER_LANES), _const_spec(1, LANES),
                      pl.BlockSpec(memory_space=pl.ANY)],
            out_specs=out_specs,
            scratch_shapes=[pltpu.VMEM((GATHER_DEPTH, MIX_ROWS * TOKEN_CHUNKS, LANES), F32),
                            pltpu.SemaphoreType.DMA((GATHER_DEPTH,))],
        ),
        out_shape=out_shape,
        input_output_aliases=aliases,
        compiler_params=pltpu.CompilerParams(
            dimension_semantics=("arbitrary",), vmem_limit_bytes=VMEM_LIMIT_BYTES),
        name="sgu_block_diag" if block_diag else "sgu_chunked",
    )(pos, *inputs)


def _moe_plan(meta, cnt, n_tiles_max):
    i32 = jnp.int32
    n_tok = meta.shape[1]
    bucket = meta[META_BUCKET].astype(i32)
    rank = meta[META_RANK].astype(i32)
    n = cnt[0, :N_BUCKETS].astype(i32)
    nt = (n + MOE_TILE - 1) // MOE_TILE
    tile_end = jnp.cumsum(nt)
    tile_start = tile_end - nt
    pos = tile_start[bucket] * MOE_TILE + rank
    n_tiles = tile_end[-1]
    j = jnp.minimum(jnp.arange(n_tiles_max, dtype=i32), jnp.maximum(n_tiles - 1, 0))
    tb = jnp.minimum(jnp.sum((j[:, None] >= tile_end[None, :]).astype(i32), axis=1), N_BUCKETS - 1)
    grp = tb // N_PAIRS
    pair = tb % N_PAIRS
    e_a = grp * EXPERTS_PER_GROUP + jnp.asarray(PAIR_A, i32)[pair]
    e_b = grp * EXPERTS_PER_GROUP + jnp.asarray(PAIR_B, i32)[pair]
    pad_lo = jnp.minimum(tile_start * MOE_TILE + n, (n_tiles_max - 1) * MOE_TILE)
    bits = lambda w: lax.bitcast_convert_type(w, i32)
    token_of_slot, gate_a, gate_b = _invert_plan(
        pos, bits(meta[META_W_A]), bits(meta[META_W_B]), pad_lo, n_tiles_max * MOE_TILE)
    as_f32 = lambda w: lax.bitcast_convert_type(w, F32)[None, :]
    gates = jnp.concatenate(
        [as_f32(gate_a), as_f32(gate_b), jnp.zeros((META_ROWS - 2, n_tiles_max * MOE_TILE), F32)], axis=0)
    return pos, token_of_slot, gates, e_a, e_b, n_tiles.reshape(1)


def _invert_kernel(pos_ref, wa_ref, wb_ref, pad_lo_ref, tok_ref, ga_ref, gb_ref):
    def pad_bucket(b, c):
        base = pad_lo_ref[b]

        def pad(k, c):
            tok_ref[base + k] = 0
            ga_ref[base + k] = 0
            gb_ref[base + k] = 0
            return c

        return lax.fori_loop(0, MOE_TILE, pad, c, unroll=8)

    lax.fori_loop(0, N_BUCKETS, pad_bucket, 0)

    def place(t, c):
        p = pos_ref[t]
        tok_ref[p] = t
        ga_ref[p] = wa_ref[t]
        gb_ref[p] = wb_ref[t]
        return c

    lax.fori_loop(0, pos_ref.shape[0], place, 0, unroll=8)


def _invert_plan(pos, wa_bits, wb_bits, pad_lo, n_slots):
    smem = pl.BlockSpec(memory_space=pltpu.SMEM)
    sds = jax.ShapeDtypeStruct((n_slots,), jnp.int32)
    return pl.pallas_call(
        _invert_kernel, in_specs=[smem] * 4, out_specs=[smem] * 3, out_shape=[sds] * 3, name="moe_invert",
    )(pos, wa_bits, wb_bits, pad_lo)


def _moe_kernel(tok_of_slot_ref, ea_ref, eb_ref, nt_ref, src_hbm, gates_ref, g_ref,
                wg_a, wu_a, wd_a, wg_b, wu_b, wd_b, gf_ref, ys_ref, xbuf, wgu_ref, wd_ref, sems,
                *, final_norm):
    j = pl.program_id(0)
    n_tiles = nt_ref[0]

    @pl.when(j == 0)
    def _():
        _gather_prologue(tok_of_slot_ref, 0, n_tiles, src_hbm, xbuf, sems, MOE_TILE)

    @pl.when(j < n_tiles)
    def _():
        prev = jnp.maximum(j - 1, 0)
        for s, (e_ref, wg, wu, wd) in enumerate(((ea_ref, wg_a, wu_a, wd_a), (eb_ref, wg_b, wu_b, wd_b))):
            @pl.when((j == 0) | (e_ref[j] != e_ref[prev]))
            def _():
                def cast(c, carry, s=s, wg=wg, wu=wu):
                    rows = pl.ds(pl.multiple_of(c * CAST_ROWS, CAST_ROWS), CAST_ROWS)
                    wgu_ref[s, rows, :D_EXPERT] = wg[0, rows, :].astype(BF16)
                    wgu_ref[s, rows, D_EXPERT:] = wu[0, rows, :].astype(BF16)
                    return carry

                lax.fori_loop(0, D_MODEL // CAST_ROWS, cast, 0)
                wd_ref[s] = wd[0].astype(BF16)

        x = _gather_step(j, tok_of_slot_ref, 0, n_tiles, src_hbm, xbuf, sems, MOE_TILE)
        xn = _rms(x, g_ref[...]).astype(BF16)
        gates = jnp.concatenate(
            [gates_ref[...], jnp.zeros((LANES - META_ROWS, MOE_TILE), F32)], axis=0).T
        out = None
        for s in range(2):
            gu = jnp.dot(xn, wgu_ref[s], preferred_element_type=F32)
            hcur = (jax.nn.silu(gu[:, :D_EXPERT]) * gu[:, D_EXPERT:] * gates[:, s:s + 1]).astype(BF16)
            o = jnp.dot(hcur, wd_ref[s], preferred_element_type=F32)
            out = o if out is None else out + o
        y = x + out
        if final_norm:
            y = _rms(y, gf_ref[...])
        _rows_to_tokens(y, ys_ref)

        @pl.when(j == n_tiles - 1)
        def _():
            _gather_epilogue(j, src_hbm, xbuf, sems)


def _moe_experts(tokens, token_of_slot, gates, e_a, e_b, n_tiles, g, w_gate, w_up, w_down, g_final,
                 *, final_norm):
    n_tiles_max = e_a.shape[0]
    last = lambda j, nt: jnp.minimum(j, jnp.maximum(nt[0] - 1, 0))
    w_in = lambda which: pl.BlockSpec(
        (1, D_MODEL, D_EXPERT), lambda j, tos, ea, eb, nt: ((ea, eb)[which][j], 0, 0))
    w_out = lambda which: pl.BlockSpec(
        (1, D_EXPERT, D_MODEL), lambda j, tos, ea, eb, nt: ((ea, eb)[which][j], 0, 0))
    return pl.pallas_call(
        functools.partial(_moe_kernel, final_norm=final_norm),
        grid_spec=pltpu.PrefetchScalarGridSpec(
            num_scalar_prefetch=4,
            grid=(n_tiles_max,),
            in_specs=[pl.BlockSpec(memory_space=pl.ANY),
                      pl.BlockSpec((META_ROWS, MOE_TILE), lambda j, tos, ea, eb, nt: (0, last(j, nt))),
                      _const_spec(1, D_MODEL), w_in(0), w_in(0), w_out(0), w_in(1), w_in(1), w_out(1),
                      _const_spec(1, D_MODEL)],
            out_specs=pl.BlockSpec((MOE_TILE * TOKEN_CHUNKS, LANES),
                                   lambda j, tos, ea, eb, nt: (last(j, nt), 0)),
            scratch_shapes=[pltpu.VMEM((GATHER_DEPTH, MOE_TILE * TOKEN_CHUNKS, LANES), F32),
                            pltpu.VMEM((2, D_MODEL, 2 * D_EXPERT), BF16), pltpu.VMEM((2, D_EXPERT, D_MODEL), BF16),
                            pltpu.SemaphoreType.DMA((GATHER_DEPTH,))],
        ),
        out_shape=jax.ShapeDtypeStruct((n_tiles_max * MOE_TILE * TOKEN_CHUNKS, LANES), F32),
        compiler_params=pltpu.CompilerParams(
            dimension_semantics=("arbitrary",), vmem_limit_bytes=VMEM_LIMIT_BYTES),
        name="moe_experts_final" if final_norm else "moe_experts",
    )(token_of_slot, e_a, e_b, n_tiles, tokens, gates, g, w_gate, w_up, w_down, w_gate, w_up, w_down, g_final)


def _moe_layer(tokens, meta, cnt, g, w_gate, w_up, w_down, g_final, *, final_norm):
    n_tiles_max = meta.shape[1] // MOE_TILE + N_BUCKETS
    pos, token_of_slot, gates, e_a, e_b, n_tiles = _moe_plan(meta, cnt, n_tiles_max)
    ys = _moe_experts(tokens, token_of_slot, gates, e_a, e_b, n_tiles, g, w_gate, w_up, w_down, g_final,
                      final_norm=final_norm)
    return ys, pos


def _ungather_kernel(pos_ref, src_hbm, *refs, n_blocks, n_first_blocks):
    o_refs, (xbuf, sems) = refs[:-2], refs[-2:]
    i = pl.program_id(0)
    x = _gathered_rows(pos_ref, 0, n_blocks, src_hbm, xbuf, sems, MIX_ROWS)

    @pl.when(i < n_first_blocks)
    def _():
        o_refs[0][...] = x

    @pl.when(i >= n_first_blocks)
    def _():
        o_refs[1][...] = x


def _ungather(ys, pos, n_first_rows):
    n_blocks = pos.shape[0] // MIX_ROWS
    n_first = n_first_rows // MIX_ROWS
    spec = lambda f: pl.BlockSpec((MIX_ROWS, D_MODEL), f)
    return pl.pallas_call(
        functools.partial(_ungather_kernel, n_blocks=n_blocks, n_first_blocks=n_first),
        grid_spec=pltpu.PrefetchScalarGridSpec(
            num_scalar_prefetch=1,
            grid=(n_blocks,),
            in_specs=[pl.BlockSpec(memory_space=pl.ANY)],
            out_specs=[spec(lambda i, pos: (jnp.minimum(i, n_first - 1), 0)),
                       spec(lambda i, pos: (jnp.maximum(i - n_first, 0), 0))],
            scratch_shapes=[pltpu.VMEM((GATHER_DEPTH, MIX_ROWS * TOKEN_CHUNKS, LANES), F32),
                            pltpu.SemaphoreType.DMA((GATHER_DEPTH,))],
        ),
        out_shape=[jax.ShapeDtypeStruct((n_first_rows, D_MODEL), F32),
                   jax.ShapeDtypeStruct((pos.shape[0] - n_first_rows, D_MODEL), F32)],
        compiler_params=pltpu.CompilerParams(
            dimension_semantics=("arbitrary",), vmem_limit_bytes=VMEM_LIMIT_BYTES),
        name="moe_ungather",
    )(pos, ys)


def _router_weights(w_rg, w_re):
    pad = ROUTER_LANES - N_EXPERT_GROUPS - N_EXPERTS
    w = jnp.concatenate([w_rg.astype(F32), w_re.astype(F32), jnp.zeros((D_MODEL, pad), F32)], axis=-1)
    hi = w.astype(BF16)
    lo = (w - hi.astype(F32)).astype(BF16)
    return jnp.concatenate([hi, lo], axis=-1)


def kernel(x_prompt, x_sample, state_ssm_re, state_ssm_im, norm_mix, norm_ffn, norm_final, ssm_w_in, ssm_lambda_re, ssm_lambda_im, ssm_log_step, ssm_b_re, ssm_b_im, ssm_c_re, ssm_c_im, ssm_d, ssm_w_out, sgu_w_in, sgu_norm_g, sgu_norm_b, sgu_w_s, sgu_b_s, sgu_w_out, moe_router_group, moe_router_expert, moe_w_gate, moe_w_up, moe_w_down):
    nb, ns, dm = x_prompt.shape
    db, ds, _ = x_sample.shape
    n_p, n_s = nb * ns, db * ds
    n_tok = n_p + n_s
    n_sg = db // SUBLANES
    steps_p = MIX_ROWS // SUBLANES
    sg_per_block = MIX_ROWS // (ds * SUBLANES)
    assert nb == SUBLANES and dm == D_MODEL and ns % steps_p == 0 and n_sg % sg_per_block == 0
    assert n_p % MIX_ROWS == 0 and n_s % MIX_ROWS == 0 and MIX_ROWS % MOE_TILE == 0 and CHUNK % ds == 0
    row = lambda a: a.astype(F32).reshape(1, -1)
    zero_cnt = jnp.zeros((1, LANES), F32)

    lb_re, lb_im, bb_re, bb_im = _s5_prep(ssm_lambda_re, ssm_lambda_im, ssm_log_step, ssm_b_re, ssm_b_im)
    bmat, cmat, a_re, a_im, dvec = _s5_matrices(lb_re, lb_im, bb_re, bb_im, ssm_c_re, ssm_c_im, ssm_d)
    xp = x_prompt.astype(F32).reshape(1, nb, ns, dm)
    xs = x_sample.astype(F32).reshape(n_sg, SUBLANES, ds, dm)
    zero_state = jnp.zeros((1, SUBLANES, STATE_COLS), F32)
    s0 = _pack_state(state_ssm_re.astype(F32), state_ssm_im.astype(F32)).reshape(n_sg, SUBLANES, STATE_COLS)
    wr0 = _router_weights(moe_router_group[0], moe_router_expert[0])
    s5 = functools.partial(_s5_layer, g=row(norm_mix[0]), win=ssm_w_in.astype(BF16), bmat=bmat, a_re=a_re,
                           a_im=a_im, cmat=cmat, dvec=dvec, wout=ssm_w_out.astype(BF16),
                           g_ffn=row(norm_ffn[0]), w_router=wr0, n_total_rows=n_tok)
    tok0, meta_p, st_p, cnt0 = s5(xp, zero_state, cnt_in=zero_cnt, tok_prev=None, block_off=0,
                                  n_seq_groups=1, steps=steps_p, carry=True)
    tok0, meta_s, st_s, cnt0 = s5(xs, s0, cnt_in=cnt0, tok_prev=tok0, block_off=n_p // MIX_ROWS,
                                  n_seq_groups=sg_per_block, steps=ds, carry=False)
    ssm_re_p, ssm_im_p = _unpack_state(st_p.reshape(nb, STATE_COLS))
    ssm_re_s, ssm_im_s = _unpack_state(st_s.reshape(db, STATE_COLS))
    ys0, pos0 = _moe_layer(tok0, jnp.concatenate([meta_p, meta_s], axis=1), cnt0, row(norm_ffn[0]),
                           moe_w_gate[0], moe_w_up[0], moe_w_down[0], row(norm_final), final_norm=False)

    pos0 = jnp.concatenate([
        pos0[:n_p].reshape(ns, nb).T.reshape(-1),
        jnp.transpose(pos0[n_p:].reshape(n_sg, ds, SUBLANES), (0, 2, 1)).reshape(-1)])
    ws_p = sgu_w_s.astype(F32)[:, :CHUNK, :CHUNK]
    bs_p = jnp.repeat(sgu_b_s.astype(F32)[:, :CHUNK].T, SGU_HEAD_DIM, axis=1)
    reps = CHUNK // ds
    ws_s = jnp.tile(sgu_w_s.astype(F32)[:, :ds, :ds], (1, reps, reps))
    bs_s = jnp.tile(jnp.repeat(sgu_b_s.astype(F32)[:, :ds].T, SGU_HEAD_DIM, axis=1), (reps, 1))
    wr1 = _router_weights(moe_router_group[1], moe_router_expert[1])
    sgu = functools.partial(_sgu_layer, pos0, ys0, g=row(norm_mix[1]), win=sgu_w_in.astype(BF16),
                            ng=row(sgu_norm_g), nb=row(sgu_norm_b), wout=sgu_w_out.astype(BF16),
                            g_ffn=row(norm_ffn[1]), w_router=wr1, n_total_rows=n_tok)
    tok1, meta_p, cnt1 = sgu(ws=ws_p, bs=bs_p, cnt_in=zero_cnt, tok_prev=None, n_blocks=n_p // MIX_ROWS,
                             block_off=0, block_diag=False, emit_v=False)
    tok1, meta_s, cnt1, v_s = sgu(ws=ws_s, bs=bs_s, cnt_in=cnt1, tok_prev=tok1, n_blocks=n_s // MIX_ROWS,
                                  block_off=n_p // MIX_ROWS, block_diag=True, emit_v=True)
    ys1, pos1 = _moe_layer(tok1, jnp.concatenate([meta_p, meta_s], axis=1), cnt1, row(norm_ffn[1]),
                           moe_w_gate[1], moe_w_up[1], moe_w_down[1], row(norm_final), final_norm=True)
    y_p, y_s = _ungather(ys1, pos1, n_p)
    return (y_p.reshape(nb, ns, dm), y_s.reshape(db, ds, dm), ssm_re_p, ssm_im_p, ssm_re_s, ssm_im_s,
            v_s.reshape(db, ds, SGU_WIDTH))
```

```python
import functools

import jax
import jax.numpy as jnp
from jax import lax
from jax.experimental import pallas as pl
from jax.experimental.pallas import tpu as pltpu

D_MODEL = 1024
SSM_WIDTH = D_MODEL // 2
SSM_GROUP = 16
SSM_GROUPS = SSM_WIDTH // SSM_GROUP
SSM_STATE = 64
SGU_WIDTH = D_MODEL
SGU_HEADS = 8
SGU_HEAD_DIM = SGU_WIDTH // SGU_HEADS
CHUNK = 128
N_EXPERT_GROUPS = 4
EXPERTS_PER_GROUP = 4
N_EXPERTS = N_EXPERT_GROUPS * EXPERTS_PER_GROUP
D_EXPERT = D_MODEL // 4
EPS = 1e-6

SUBLANES = 8
LANES = 128
VMEM_LIMIT_BYTES = 56 * 1024 * 1024
N_DMA_PRIORITIES = 2

SLAB_GROUPS = 16
N_SLABS = SSM_GROUPS // SLAB_GROUPS
SLAB_HALF = SLAB_GROUPS * SSM_STATE
SLAB = 2 * SLAB_HALF
STATE_COLS = N_SLABS * SLAB
SLAB_IN = SLAB_GROUPS * SSM_GROUP
REC_COLS = 512

TOKEN_CHUNKS = D_MODEL // LANES
assert TOKEN_CHUNKS == SUBLANES

PAIR_A = (0, 0, 0, 1, 1, 3)
PAIR_B = (1, 2, 3, 3, 2, 2)
N_PAIRS = len(PAIR_A)
GATHER_DEPTH = 3
GATHER_UNROLL = 16
CAST_ROWS = 256
N_BUCKETS = N_EXPERT_GROUPS * N_PAIRS
MOE_TILE = 256
MIX_ROWS = 512
META_ROWS = SUBLANES
META_BUCKET, META_W_A, META_W_B, META_RANK = 0, 1, 2, 3
ROUTER_LANES = LANES
ROUTER_GROUP_ROW, ROUTER_EXPERT_ROW = 0, SUBLANES
CNT_ROWS = 32
BF16 = jnp.bfloat16
F32 = jnp.float32


def _rms(x, g):
    return x * lax.rsqrt(jnp.mean(x * x, axis=-1, keepdims=True) + EPS) * g


def _s5_prep_kernel(lr_ref, li_ref, ls_ref, br_ref, bi_ref, lbr_ref, lbi_ref, bbr_ref, bbi_ref):
    lr = lr_ref[...]
    li = li_ref[...]
    dt = jnp.exp(ls_ref[...])
    mag = jnp.exp(lr * dt)
    ang = li * dt
    lb_re = mag * jnp.cos(ang)
    lb_im = mag * jnp.sin(ang)
    den = lr * lr + li * li
    nr = lb_re - 1.0
    coef_re = (nr * lr + lb_im * li) / den
    coef_im = (lb_im * lr - nr * li) / den
    br = br_ref[...]
    bi = bi_ref[...]
    lbr_ref[...] = lb_re
    lbi_ref[...] = lb_im
    bbr_ref[...] = coef_re * br - coef_im * bi
    bbi_ref[...] = coef_re * bi + coef_im * br


def _s5_prep(lam_re, lam_im, log_step, b_re, b_im):
    rows = SSM_GROUPS * SSM_GROUP
    rep = lambda a: jnp.repeat(a.astype(F32), SSM_GROUP, axis=0)
    lr = rep(lam_re)
    li = rep(lam_im)
    ls = rep(jnp.broadcast_to(log_step.astype(F32)[:, None], (SSM_GROUPS, SSM_STATE)))
    br = jnp.transpose(b_re.astype(F32), (0, 2, 1)).reshape(rows, SSM_STATE)
    bi = jnp.transpose(b_im.astype(F32), (0, 2, 1)).reshape(rows, SSM_STATE)
    sds = jax.ShapeDtypeStruct((rows, SSM_STATE), F32)
    lb_re, lb_im, bb_re, bb_im = pl.pallas_call(
        _s5_prep_kernel, out_shape=(sds, sds, sds, sds), name="s5_prep")(lr, li, ls, br, bi)
    return lb_re, lb_im, bb_re, bb_im


def _to_state_cols(a):
    return a.reshape(a.shape[:-2] + (N_SLABS, SLAB_HALF))


def _pack_state(re, im):
    return jnp.concatenate([_to_state_cols(re), _to_state_cols(im)], axis=-1).reshape(re.shape[0], STATE_COLS)


def _unpack_state(s):
    s = s.reshape(s.shape[0], N_SLABS, 2, SLAB_GROUPS, SSM_STATE)
    re = s[:, :, 0].reshape(s.shape[0], SSM_GROUPS, SSM_STATE)
    im = s[:, :, 1].reshape(s.shape[0], SSM_GROUPS, SSM_STATE)
    return re, im


def _s5_matrices(lb_re, lb_im, bb_re, bb_im, c_re, c_im, d):
    eye = jnp.eye(SLAB_GROUPS, dtype=F32)
    def in_side(bb):
        bb = bb.reshape(N_SLABS, SLAB_GROUPS, SSM_GROUP, SSM_STATE)
        m = bb[:, :, :, None, :] * eye[None, :, None, :, None]
        return m.reshape(N_SLABS, SLAB_IN, SLAB_HALF)
    bmat = jnp.concatenate([in_side(bb_re), in_side(bb_im)], axis=-1).astype(BF16)
    def out_side(c):
        c = jnp.transpose(c.astype(F32), (0, 2, 1)).reshape(N_SLABS, SLAB_GROUPS, SSM_STATE, SSM_GROUP)
        m = c[:, :, :, None, :] * eye[None, :, None, :, None]
        return m.reshape(N_SLABS, SLAB_HALF, SLAB_IN)
    cmat = jnp.concatenate([out_side(c_re), -out_side(c_im)], axis=1).astype(BF16)
    lam = lambda lb: _to_state_cols(lb.reshape(SSM_GROUPS, SSM_GROUP, SSM_STATE)[:, 0])
    a_re = jnp.concatenate([lam(lb_re), lam(lb_re)], axis=-1).reshape(1, STATE_COLS)
    a_im = jnp.concatenate([lam(lb_im), lam(lb_im)], axis=-1).reshape(1, STATE_COLS)
    return bmat, cmat, a_re, a_im, d.astype(F32).reshape(1, SSM_WIDTH)


def _rows_to_tokens(x, tok_ref):
    n = x.shape[0]
    for c in range(TOKEN_CHUNKS):
        tok_ref[pl.ds(c, n, stride=TOKEN_CHUNKS), :] = x[:, c * LANES:(c + 1) * LANES]


def _tokens_to_rows(tok_ref, n):
    return jnp.concatenate([tok_ref[pl.ds(c, n, stride=TOKEN_CHUNKS), :] for c in range(TOKEN_CHUNKS)], axis=-1)


def _start_token_gather(idx_ref, base, src_hbm, buf_ref, sem, n):
    def issue(blk, c):
        for k in range(GATHER_UNROLL):
            r = blk * GATHER_UNROLL + k
            tok = idx_ref[base + r]
            pltpu.make_async_copy(
                src_hbm.at[pl.ds(pl.multiple_of(tok * TOKEN_CHUNKS, TOKEN_CHUNKS), TOKEN_CHUNKS), :],
                buf_ref.at[pl.ds(pl.multiple_of(r * TOKEN_CHUNKS, TOKEN_CHUNKS), TOKEN_CHUNKS), :],
                sem).start(priority=k % N_DMA_PRIORITIES)
        return c

    lax.fori_loop(0, n // GATHER_UNROLL, issue, 0)


def _wait_token_gather(src_hbm, buf_ref, sem):
    pltpu.make_async_copy(src_hbm.at[pl.ds(0, buf_ref.shape[0]), :], buf_ref, sem).wait()


def _gather_prologue(idx_ref, first_block, n_blocks, src_hbm, buf_ref, sems, n):
    for k in range(GATHER_DEPTH - 1):
        blk = first_block + jnp.minimum(k, n_blocks - 1)
        _start_token_gather(idx_ref, blk * n, src_hbm, buf_ref.at[k], sems.at[k], n)


def _gather_step(i, idx_ref, first_block, n_blocks, src_hbm, buf_ref, sems, n):
    slot = lax.rem(i, GATHER_DEPTH)
    ahead = lax.rem(i + GATHER_DEPTH - 1, GATHER_DEPTH)
    _wait_token_gather(src_hbm, buf_ref.at[slot], sems.at[slot])
    blk = first_block + jnp.minimum(i + GATHER_DEPTH - 1, n_blocks - 1)
    _start_token_gather(idx_ref, blk * n, src_hbm, buf_ref.at[ahead], sems.at[ahead], n)
    return _tokens_to_rows(buf_ref.at[slot], n)


def _gather_epilogue(i, src_hbm, buf_ref, sems):
    for k in range(1, GATHER_DEPTH):
        slot = lax.rem(i + k, GATHER_DEPTH)
        _wait_token_gather(src_hbm, buf_ref.at[slot], sems.at[slot])


def _gathered_rows(idx_ref, first_block, n_blocks, src_hbm, buf_ref, sems, n):
    i = pl.program_id(0)

    @pl.when(i == 0)
    def _():
        _gather_prologue(idx_ref, first_block, n_blocks, src_hbm, buf_ref, sems, n)

    x = _gather_step(i, idx_ref, first_block, n_blocks, src_hbm, buf_ref, sems, n)

    @pl.when(i == n_blocks - 1)
    def _():
        _gather_epilogue(i, src_hbm, buf_ref, sems)

    return x


def _route_meta(x, g, wr_t, cnt_ref):
    xn = _rms(x, g)
    n = x.shape[0]
    xh = xn.astype(BF16)
    xl = (xn - xh.astype(F32)).astype(BF16)
    nt = (((1,), (1,)), ((), ()))
    p = lax.dot_general(wr_t, xh, nt, preferred_element_type=F32)
    q = lax.dot_general(wr_t[:ROUTER_LANES], xl, nt, preferred_element_type=F32)
    logits = p[:ROUTER_LANES] + (p[ROUTER_LANES:] + q)
    lg = [logits[ROUTER_GROUP_ROW + k:ROUTER_GROUP_ROW + k + 1] for k in range(N_EXPERT_GROUPS)]
    mg = functools.reduce(jnp.maximum, lg)
    g_idx = jnp.full((1, n), N_EXPERT_GROUPS - 1, jnp.int32)
    for k in reversed(range(N_EXPERT_GROUPS - 1)):
        g_idx = jnp.where(lg[k] == mg, k, g_idx)
    g_w = 1.0 / functools.reduce(jnp.add, [jnp.exp(l - mg) for l in lg])
    le = []
    for k in range(EXPERTS_PER_GROUP):
        row = lambda grp: logits[ROUTER_EXPERT_ROW + grp * EXPERTS_PER_GROUP + k:
                                 ROUTER_EXPERT_ROW + grp * EXPERTS_PER_GROUP + k + 1]
        v = row(N_EXPERT_GROUPS - 1)
        for grp in reversed(range(N_EXPERT_GROUPS - 1)):
            v = jnp.where(g_idx == grp, row(grp), v)
        le.append(v)
    m1 = functools.reduce(jnp.maximum, le)
    i1 = jnp.full((1, n), EXPERTS_PER_GROUP - 1, jnp.int32)
    for k in reversed(range(EXPERTS_PER_GROUP - 1)):
        i1 = jnp.where(le[k] == m1, k, i1)
    neg = jnp.float32(-jnp.inf)
    le2 = [jnp.where(i1 == k, neg, le[k]) for k in range(EXPERTS_PER_GROUP)]
    m2 = functools.reduce(jnp.maximum, le2)
    i2 = jnp.full((1, n), EXPERTS_PER_GROUP - 1, jnp.int32)
    for k in reversed(range(EXPERTS_PER_GROUP - 1)):
        i2 = jnp.where(le2[k] == m2, k, i2)
    e2 = jnp.exp(m2 - m1)
    w1 = g_w / (1.0 + e2)
    w2 = g_w * e2 / (1.0 + e2)
    lo = jnp.minimum(i1, i2)
    hi = jnp.maximum(i1, i2)
    pair = jnp.where(lo == 0, hi - 1, jnp.where(lo == 1, 6 - hi, 5))
    bucket = g_idx * N_PAIRS + pair
    w_lo = jnp.where(i1 < i2, w1, w2)
    w_hi = jnp.where(i1 < i2, w2, w1)
    w_a = jnp.where(pair == 5, w_hi, w_lo)
    w_b = jnp.where(pair == 5, w_lo, w_hi)
    onehot = lax.broadcasted_iota(jnp.int32, (CNT_ROWS, n), 0) == bucket
    oh = jnp.where(onehot, 1.0, 0.0)
    r_i = lax.broadcasted_iota(jnp.int32, (n, n), 0)
    c_i = lax.broadcasted_iota(jnp.int32, (n, n), 1)
    before = jnp.where(r_i < c_i, 1.0, 0.0).astype(BF16)
    prefix = jnp.dot(oh.astype(BF16), before, preferred_element_type=F32)
    cnt = cnt_ref[...]
    rank = jnp.sum(jnp.where(onehot, prefix + cnt[:, :1], 0.0), axis=0, keepdims=True)
    cnt_ref[...] = cnt + jnp.sum(oh, axis=1, keepdims=True)
    return jnp.concatenate([bucket.astype(F32), w_a, w_b, rank, jnp.zeros((META_ROWS - 4, n), F32)], axis=0)


def _emit_tokens(x, tok_ref, meta_ref, cnt_in_ref, cnt_ref, gf_ref, wr_ref):
    @pl.when(pl.program_id(0) == 0)
    def _():
        cnt_ref[...] = cnt_in_ref[...]
    meta_ref[...] = _route_meta(x, gf_ref[...], wr_ref[...], cnt_ref)
    _rows_to_tokens(x, tok_ref)


def _const_spec(*shape):
    return pl.BlockSpec(shape, lambda i, *_: (0,) * len(shape))


_META_SPEC = pl.BlockSpec((META_ROWS, MIX_ROWS), lambda i, *_: (0, i))


def _tokens_alias(tok_prev, input_index):
    if tok_prev is None:
        return {}, jnp.zeros((SUBLANES, LANES), F32)
    return {input_index: 0}, tok_prev


def _tokens_out(n_total_rows, block_off):
    return (pl.BlockSpec((MIX_ROWS * TOKEN_CHUNKS, LANES), lambda i, *_: (i + block_off, 0)),
            jax.ShapeDtypeStruct((n_total_rows * TOKEN_CHUNKS, LANES), F32))


def _time_major_copies(x_hbm, xt_ref, sem, step, slot, n_seq_groups, steps):
    g0 = step * n_seq_groups if n_seq_groups > 1 else 0
    t0 = 0 if n_seq_groups > 1 else step * steps
    return [pltpu.make_async_copy(x_hbm.at[pl.ds(g0, n_seq_groups), b, pl.ds(t0, steps), :],
                                  xt_ref.at[slot, :, :, b, :], sem.at[slot]) for b in range(SUBLANES)]


def _s5_kernel(x_hbm, g_ref, win_ref, bm_ref, are_ref, aim_ref, cm_ref, d_ref, wout_ref, s0_ref,
               gf_ref, wr_ref, cnt_in_ref, tok_alias_ref,
               tok_ref, meta_ref, so_ref, cnt_ref, bu_ref, st_ref, xt_ref, xsem,
               *, n_blocks, n_seq_groups, steps, carry):
    del tok_alias_ref
    i = pl.program_id(0)
    slot = lax.rem(i, 2)
    copies = functools.partial(_time_major_copies, x_hbm, xt_ref, xsem, n_seq_groups=n_seq_groups, steps=steps)

    @pl.when(i == 0)
    def _():
        for c in copies(0, 0):
            c.start()

    @pl.when(i + 1 < n_blocks)
    def _():
        for c in copies(i + 1, 1 - slot):
            c.start()

    for c in copies(i, slot):
        c.wait()
    x = xt_ref[slot].reshape(MIX_ROWS, D_MODEL)
    h = _rms(x, g_ref[...]).astype(BF16)
    u = jnp.dot(h, win_ref[...], preferred_element_type=F32)
    ub = u.astype(BF16)
    for j in range(N_SLABS):
        bu_ref[:, j * SLAB:(j + 1) * SLAB] = jnp.dot(
            ub[:, j * SLAB_IN:(j + 1) * SLAB_IN], bm_ref[j], preferred_element_type=F32)

    if carry:
        @pl.when(pl.program_id(0) == 0)
        def _():
            st_ref[...] = s0_ref[0]

    for sg in range(n_seq_groups):
        for j in range(N_SLABS):
            for k in range(SLAB_HALF // REC_COLS):
                re0 = j * SLAB + k * REC_COLS
                im0 = re0 + SLAB_HALF
                ar = jnp.broadcast_to(are_ref[:, re0:re0 + REC_COLS], (SUBLANES, REC_COLS))
                ai = jnp.broadcast_to(aim_ref[:, re0:re0 + REC_COLS], (SUBLANES, REC_COLS))
                if carry:
                    sr = st_ref[:, re0:re0 + REC_COLS]
                    si = st_ref[:, im0:im0 + REC_COLS]
                else:
                    sr = s0_ref[sg, :, re0:re0 + REC_COLS]
                    si = s0_ref[sg, :, im0:im0 + REC_COLS]

                def step(t, c, re0=re0, im0=im0, ar=ar, ai=ai, sg=sg):
                    sr, si = c
                    row = pl.multiple_of((sg * steps + t) * SUBLANES, SUBLANES)
                    br = bu_ref[pl.ds(row, SUBLANES), re0:re0 + REC_COLS]
                    bi = bu_ref[pl.ds(row, SUBLANES), im0:im0 + REC_COLS]
                    nr = ar * sr - ai * si + br
                    ni = ar * si + ai * sr + bi
                    bu_ref[pl.ds(row, SUBLANES), re0:re0 + REC_COLS] = nr
                    bu_ref[pl.ds(row, SUBLANES), im0:im0 + REC_COLS] = ni
                    return nr, ni

                sr, si = lax.fori_loop(0, steps, step, (sr, si), unroll=min(steps, 8))
                if carry:
                    st_ref[:, re0:re0 + REC_COLS] = sr
                    st_ref[:, im0:im0 + REC_COLS] = si
                    so_ref[0, :, re0:re0 + REC_COLS] = sr
                    so_ref[0, :, im0:im0 + REC_COLS] = si
                else:
                    so_ref[sg, :, re0:re0 + REC_COLS] = sr
                    so_ref[sg, :, im0:im0 + REC_COLS] = si

    ys = [jnp.dot(bu_ref[:, j * SLAB:(j + 1) * SLAB].astype(BF16), cm_ref[j], preferred_element_type=F32)
          for j in range(N_SLABS)]
    y = jnp.concatenate(ys, axis=-1) + d_ref[...] * u
    y = jax.nn.gelu(y).astype(BF16)
    a = jnp.dot(y, wout_ref[...], preferred_element_type=F32)
    xo = x + a[:, :D_MODEL] * jax.nn.sigmoid(a[:, D_MODEL:])
    _emit_tokens(xo, tok_ref, meta_ref, cnt_in_ref, cnt_ref, gf_ref, wr_ref)


def _s5_layer(x_seq, s0, g, win, bmat, a_re, a_im, cmat, dvec, wout, g_ffn, w_router, cnt_in, tok_prev,
              *, n_total_rows, block_off, n_seq_groups, steps, carry):
    assert n_seq_groups * steps * SUBLANES == MIX_ROWS and x_seq.shape[1] == SUBLANES
    assert (n_seq_groups == 1 and x_seq.shape[0] == 1) or steps == x_seq.shape[2]
    n_blocks = x_seq.shape[0] * x_seq.shape[2] // (n_seq_groups * steps)
    s_idx = (lambda i: (0, 0, 0)) if carry else (lambda i: (i, 0, 0))
    kern = functools.partial(_s5_kernel, n_blocks=n_blocks, n_seq_groups=n_seq_groups, steps=steps, carry=carry)
    n_state_groups = 1 if carry else n_blocks * n_seq_groups
    tok_spec, tok_shape = _tokens_out(n_total_rows, block_off)
    aliases, tok_prev = _tokens_alias(tok_prev, 13)
    inputs = [x_seq, g, win, bmat, a_re, a_im, cmat, dvec, wout, s0, g_ffn, w_router, cnt_in, tok_prev]
    return pl.pallas_call(
        kern,
        grid=(n_blocks,),
        in_specs=[
            pl.BlockSpec(memory_space=pl.ANY),
            _const_spec(1, D_MODEL), _const_spec(D_MODEL, SSM_WIDTH), _const_spec(N_SLABS, SLAB_IN, SLAB),
            _const_spec(1, STATE_COLS), _const_spec(1, STATE_COLS), _const_spec(N_SLABS, SLAB, SLAB_IN),
            _const_spec(1, SSM_WIDTH), _const_spec(SSM_WIDTH, 2 * D_MODEL),
            pl.BlockSpec((n_seq_groups, SUBLANES, STATE_COLS), s_idx),
            _const_spec(1, D_MODEL), _const_spec(2 * ROUTER_LANES, D_MODEL), _const_spec(CNT_ROWS, LANES),
            pl.BlockSpec(memory_space=pl.ANY),
        ],
        out_specs=[
            tok_spec, _META_SPEC,
            pl.BlockSpec((n_seq_groups, SUBLANES, STATE_COLS), s_idx),
            _const_spec(CNT_ROWS, LANES),
        ],
        out_shape=[
            tok_shape, jax.ShapeDtypeStruct((META_ROWS, n_blocks * MIX_ROWS), F32),
            jax.ShapeDtypeStruct((n_state_groups, SUBLANES, STATE_COLS), F32),
            jax.ShapeDtypeStruct((CNT_ROWS, LANES), F32),
        ],
        input_output_aliases=aliases,
        scratch_shapes=[pltpu.VMEM((MIX_ROWS, STATE_COLS), F32), pltpu.VMEM((SUBLANES, STATE_COLS), F32),
                        pltpu.VMEM((2, n_seq_groups, steps, SUBLANES, D_MODEL), F32),
                        pltpu.SemaphoreType.DMA((2,))],
        compiler_params=pltpu.CompilerParams(
            dimension_semantics=("arbitrary",), vmem_limit_bytes=VMEM_LIMIT_BYTES),
        name="s5_carry" if carry else "s5_step",
    )(*inputs)


def _sgu_kernel(pos_ref, src_hbm, g_ref, win_ref, ng_ref, nb_ref, ws_ref, bs_ref, wout_ref,
                gf_ref, wr_ref, cnt_in_ref, tok_alias_ref, *refs, n_blocks, block_off, block_diag, emit_v):
    del tok_alias_ref
    tok_ref, meta_ref, cnt_ref = refs[:3]
    v_ref = refs[3] if emit_v else None
    xbuf, sems = refs[-2:]
    x = _gathered_rows(pos_ref, block_off, n_blocks, src_hbm, xbuf, sems, MIX_ROWS)
    rows = x.shape[0]
    h = _rms(x, g_ref[...]).astype(BF16)
    hh = jax.nn.gelu(jnp.dot(h, win_ref[...], preferred_element_type=F32))
    u = hh[:, :SGU_WIDTH]
    v = hh[:, SGU_WIDTH:]
    vc = v - jnp.mean(v, axis=-1, keepdims=True)
    v = vc * lax.rsqrt(jnp.mean(vc * vc, axis=-1, keepdims=True) + EPS) * ng_ref[...] + nb_ref[...]
    if emit_v:
        v_ref[...] = v
    r = lax.broadcasted_iota(jnp.int32, (CHUNK, CHUNK), 0)
    c = lax.broadcasted_iota(jnp.int32, (CHUNK, CHUNK), 1)
    if block_diag:
        keep = (r // SUBLANES == c // SUBLANES) & (r >= c)
    else:
        keep = r >= c
    vb = v.astype(BF16)
    n_chunks = rows // CHUNK
    zs = []
    for hd in range(SGU_HEADS):
        wm = jnp.where(keep, ws_ref[hd], 0.0).astype(BF16)
        cols = slice(hd * SGU_HEAD_DIM, (hd + 1) * SGU_HEAD_DIM)
        vcat = jnp.concatenate([vb[ck * CHUNK:(ck + 1) * CHUNK, cols] for ck in range(n_chunks)], axis=-1)
        zcat = jnp.dot(wm, vcat, preferred_element_type=F32)
        zs.append([zcat[:, ck * SGU_HEAD_DIM:(ck + 1) * SGU_HEAD_DIM] for ck in range(n_chunks)])
    z = jnp.concatenate(
        [jnp.concatenate([zs[hd][ck] for hd in range(SGU_HEADS)], axis=-1) + bs_ref[...]
         for ck in range(n_chunks)], axis=0)
    o = jnp.dot((u * z).astype(BF16), wout_ref[...], preferred_element_type=F32)
    _emit_tokens(x + o, tok_ref, meta_ref, cnt_in_ref, cnt_ref, gf_ref, wr_ref)


def _sgu_layer(pos, src_tokens, g, win, ng, nb, ws, bs, wout, g_ffn, w_router, cnt_in, tok_prev,
               *, n_total_rows, n_blocks, block_off, block_diag, emit_v):
    tok_spec, tok_shape = _tokens_out(n_total_rows, block_off)
    out_shape = [tok_shape, jax.ShapeDtypeStruct((META_ROWS, n_blocks * MIX_ROWS), F32),
                 jax.ShapeDtypeStruct((CNT_ROWS, LANES), F32)]
    out_specs = [tok_spec, _META_SPEC, _const_spec(CNT_ROWS, LANES)]
    if emit_v:
        out_shape.append(jax.ShapeDtypeStruct((n_blocks * MIX_ROWS, SGU_WIDTH), F32))
        out_specs.append(pl.BlockSpec((MIX_ROWS, SGU_WIDTH), lambda i, *_: (i, 0)))
    aliases, tok_prev = _tokens_alias(tok_prev, 12)
    inputs = [src_tokens, g, win, ng, nb, ws, bs, wout, g_ffn, w_router, cnt_in, tok_prev]
    return pl.pallas_call(
        functools.partial(_sgu_kernel, n_blocks=n_blocks, block_off=block_off, block_diag=block_diag,
                          emit_v=emit_v),
        grid_spec=pltpu.PrefetchScalarGridSpec(
            num_scalar_prefetch=1,
            grid=(n_blocks,),
            in_specs=[pl.BlockSpec(memory_space=pl.ANY), _const_spec(1, D_MODEL),
                      _const_spec(D_MODEL, 2 * SGU_WIDTH), _const_spec(1, SGU_WIDTH), _const_spec(1, SGU_WIDTH),
                      _const_spec(SGU_HEADS, CHUNK, CHUNK), _const_spec(CHUNK, SGU_WIDTH),
                      _const_spec(SGU_WIDTH, D_MODEL), _const_spec(1, D_MODEL),
                      _const_spec(2 * ROUTER_LANES, D_MODEL), _const_spec(CNT_ROWS, LANES),
                      pl.BlockSpec(memory_space=pl.ANY)],
            out_specs=out_specs,
            scratch_shapes=[pltpu.VMEM((GATHER_DEPTH, MIX_ROWS * TOKEN_CHUNKS, LANES), F32),
                            pltpu.SemaphoreType.DMA((GATHER_DEPTH,))],
        ),
        out_shape=out_shape,
        input_output_aliases=aliases,
        compiler_params=pltpu.CompilerParams(
            dimension_semantics=("arbitrary",), vmem_limit_bytes=VMEM_LIMIT_BYTES),
        name="sgu_block_diag" if block_diag else "sgu_chunked",
    )(pos, *inputs)


def _moe_plan(meta, cnt, n_tiles_max):
    i32 = jnp.int32
    n_tok = meta.shape[1]
    bucket = meta[META_BUCKET].astype(i32)
    rank = meta[META_RANK].astype(i32)
    n = cnt[:N_BUCKETS, 0].astype(i32)
    nt = (n + MOE_TILE - 1) // MOE_TILE
    tile_end = jnp.cumsum(nt)
    tile_start = tile_end - nt
    pos = tile_start[bucket] * MOE_TILE + rank
    n_tiles = tile_end[-1]
    j = jnp.minimum(jnp.arange(n_tiles_max, dtype=i32), jnp.maximum(n_tiles - 1, 0))
    tb = jnp.minimum(jnp.sum((j[:, None] >= tile_end[None, :]).astype(i32), axis=1), N_BUCKETS - 1)
    grp = tb // N_PAIRS
    pair = tb % N_PAIRS
    e_a = grp * EXPERTS_PER_GROUP + jnp.asarray(PAIR_A, i32)[pair]
    e_b = grp * EXPERTS_PER_GROUP + jnp.asarray(PAIR_B, i32)[pair]
    pad_lo = jnp.minimum(tile_start * MOE_TILE + n, (n_tiles_max - 1) * MOE_TILE)
    bits = lambda w: lax.bitcast_convert_type(w, i32)
    token_of_slot, gate_a, gate_b = _invert_plan(
        pos, bits(meta[META_W_A]), bits(meta[META_W_B]), pad_lo, n_tiles_max * MOE_TILE)
    as_f32 = lambda w: lax.bitcast_convert_type(w, F32)[None, :]
    gates = jnp.concatenate(
        [as_f32(gate_a), as_f32(gate_b), jnp.zeros((META_ROWS - 2, n_tiles_max * MOE_TILE), F32)], axis=0)
    return pos, token_of_slot, gates, e_a, e_b, n_tiles.reshape(1)


def _invert_kernel(pos_ref, wa_ref, wb_ref, pad_lo_ref, tok_ref, ga_ref, gb_ref):
    def pad_bucket(b, c):
        base = pad_lo_ref[b]

        def pad(k, c):
            tok_ref[base + k] = 0
            ga_ref[base + k] = 0
            gb_ref[base + k] = 0
            return c

        return lax.fori_loop(0, MOE_TILE, pad, c, unroll=8)

    lax.fori_loop(0, N_BUCKETS, pad_bucket, 0)

    def place(t, c):
        p = pos_ref[t]
        tok_ref[p] = t
        ga_ref[p] = wa_ref[t]
        gb_ref[p] = wb_ref[t]
        return c

    lax.fori_loop(0, pos_ref.shape[0], place, 0, unroll=8)


def _invert_plan(pos, wa_bits, wb_bits, pad_lo, n_slots):
    smem = pl.BlockSpec(memory_space=pltpu.SMEM)
    sds = jax.ShapeDtypeStruct((n_slots,), jnp.int32)
    return pl.pallas_call(
        _invert_kernel, in_specs=[smem] * 4, out_specs=[smem] * 3, out_shape=[sds] * 3, name="moe_invert",
    )(pos, wa_bits, wb_bits, pad_lo)


def _moe_kernel(tok_of_slot_ref, ea_ref, eb_ref, nt_ref, src_hbm, gates_ref, g_ref,
                wg_a, wu_a, wd_a, wg_b, wu_b, wd_b, gf_ref, ys_ref, xbuf, wgu_ref, wd_ref, sems,
                *, final_norm):
    j = pl.program_id(0)
    n_tiles = nt_ref[0]

    @pl.when(j == 0)
    def _():
        _gather_prologue(tok_of_slot_ref, 0, n_tiles, src_hbm, xbuf, sems, MOE_TILE)

    @pl.when(j < n_tiles)
    def _():
        prev = jnp.maximum(j - 1, 0)
        for s, (e_ref, wg, wu, wd) in enumerate(((ea_ref, wg_a, wu_a, wd_a), (eb_ref, wg_b, wu_b, wd_b))):
            @pl.when((j == 0) | (e_ref[j] != e_ref[prev]))
            def _():
                def cast(c, carry, s=s, wg=wg, wu=wu):
                    rows = pl.ds(pl.multiple_of(c * CAST_ROWS, CAST_ROWS), CAST_ROWS)
                    wgu_ref[s, rows, :D_EXPERT] = wg[0, rows, :].astype(BF16)
                    wgu_ref[s, rows, D_EXPERT:] = wu[0, rows, :].astype(BF16)
                    return carry

                lax.fori_loop(0, D_MODEL // CAST_ROWS, cast, 0)
                wd_ref[s] = wd[0].astype(BF16)

        x = _gather_step(j, tok_of_slot_ref, 0, n_tiles, src_hbm, xbuf, sems, MOE_TILE)
        xn = _rms(x, g_ref[...]).astype(BF16)
        gates = jnp.concatenate(
            [gates_ref[...], jnp.zeros((LANES - META_ROWS, MOE_TILE), F32)], axis=0).T
        out = None
        for s in range(2):
            gu = jnp.dot(xn, wgu_ref[s], preferred_element_type=F32)
            hcur = (jax.nn.silu(gu[:, :D_EXPERT]) * gu[:, D_EXPERT:] * gates[:, s:s + 1]).astype(BF16)
            o = jnp.dot(hcur, wd_ref[s], preferred_element_type=F32)
            out = o if out is None else out + o
        y = x + out
        if final_norm:
            y = _rms(y, gf_ref[...])
        _rows_to_tokens(y, ys_ref)

        @pl.when(j == n_tiles - 1)
        def _():
            _gather_epilogue(j, src_hbm, xbuf, sems)


def _moe_experts(tokens, token_of_slot, gates, e_a, e_b, n_tiles, g, w_gate, w_up, w_down, g_final,
                 *, layer, final_norm):
    n_tiles_max = e_a.shape[0]
    last = lambda j, nt: jnp.minimum(j, jnp.maximum(nt[0] - 1, 0))
    w_in = lambda which: pl.BlockSpec(
        (None, 1, D_MODEL, D_EXPERT), lambda j, tos, ea, eb, nt: (layer, (ea, eb)[which][j], 0, 0))
    w_out = lambda which: pl.BlockSpec(
        (None, 1, D_EXPERT, D_MODEL), lambda j, tos, ea, eb, nt: (layer, (ea, eb)[which][j], 0, 0))
    return pl.pallas_call(
        functools.partial(_moe_kernel, final_norm=final_norm),
        grid_spec=pltpu.PrefetchScalarGridSpec(
            num_scalar_prefetch=4,
            grid=(n_tiles_max,),
            in_specs=[pl.BlockSpec(memory_space=pl.ANY),
                      pl.BlockSpec((META_ROWS, MOE_TILE), lambda j, tos, ea, eb, nt: (0, last(j, nt))),
                      _const_spec(1, D_MODEL), w_in(0), w_in(0), w_out(0), w_in(1), w_in(1), w_out(1),
                      _const_spec(1, D_MODEL)],
            out_specs=pl.BlockSpec((MOE_TILE * TOKEN_CHUNKS, LANES),
                                   lambda j, tos, ea, eb, nt: (last(j, nt), 0)),
            scratch_shapes=[pltpu.VMEM((GATHER_DEPTH, MOE_TILE * TOKEN_CHUNKS, LANES), F32),
                            pltpu.VMEM((2, D_MODEL, 2 * D_EXPERT), BF16), pltpu.VMEM((2, D_EXPERT, D_MODEL), BF16),
                            pltpu.SemaphoreType.DMA((GATHER_DEPTH,))],
        ),
        out_shape=jax.ShapeDtypeStruct((n_tiles_max * MOE_TILE * TOKEN_CHUNKS, LANES), F32),
        compiler_params=pltpu.CompilerParams(
            dimension_semantics=("arbitrary",), vmem_limit_bytes=VMEM_LIMIT_BYTES),
        name="moe_experts_final" if final_norm else "moe_experts",
    )(token_of_slot, e_a, e_b, n_tiles, tokens, gates, g, w_gate, w_up, w_down, w_gate, w_up, w_down, g_final)


def _moe_layer(tokens, meta, cnt, g, w_gate, w_up, w_down, g_final, *, layer, final_norm):
    n_tiles_max = meta.shape[1] // MOE_TILE + N_BUCKETS
    pos, token_of_slot, gates, e_a, e_b, n_tiles = _moe_plan(meta, cnt, n_tiles_max)
    ys = _moe_experts(tokens, token_of_slot, gates, e_a, e_b, n_tiles, g, w_gate, w_up, w_down, g_final,
                      layer=layer, final_norm=final_norm)
    return ys, pos


def _ungather_kernel(pos_ref, src_hbm, *refs, n_blocks, n_first_blocks):
    o_refs, (xbuf, sems) = refs[:-2], refs[-2:]
    i = pl.program_id(0)
    x = _gathered_rows(pos_ref, 0, n_blocks, src_hbm, xbuf, sems, MIX_ROWS)

    @pl.when(i < n_first_blocks)
    def _():
        o_refs[0][...] = x

    @pl.when(i >= n_first_blocks)
    def _():
        o_refs[1][...] = x


def _ungather(ys, pos, n_first_rows):
    n_blocks = pos.shape[0] // MIX_ROWS
    n_first = n_first_rows // MIX_ROWS
    spec = lambda f: pl.BlockSpec((MIX_ROWS, D_MODEL), f)
    return pl.pallas_call(
        functools.partial(_ungather_kernel, n_blocks=n_blocks, n_first_blocks=n_first),
        grid_spec=pltpu.PrefetchScalarGridSpec(
            num_scalar_prefetch=1,
            grid=(n_blocks,),
            in_specs=[pl.BlockSpec(memory_space=pl.ANY)],
            out_specs=[spec(lambda i, pos: (jnp.minimum(i, n_first - 1), 0)),
                       spec(lambda i, pos: (jnp.maximum(i - n_first, 0), 0))],
            scratch_shapes=[pltpu.VMEM((GATHER_DEPTH, MIX_ROWS * TOKEN_CHUNKS, LANES), F32),
                            pltpu.SemaphoreType.DMA((GATHER_DEPTH,))],
        ),
        out_shape=[jax.ShapeDtypeStruct((n_first_rows, D_MODEL), F32),
                   jax.ShapeDtypeStruct((pos.shape[0] - n_first_rows, D_MODEL), F32)],
        compiler_params=pltpu.CompilerParams(
            dimension_semantics=("arbitrary",), vmem_limit_bytes=VMEM_LIMIT_BYTES),
        name="moe_ungather",
    )(pos, ys)


def _router_weights(w_rg, w_re):
    w = jnp.zeros((ROUTER_LANES, D_MODEL), F32)
    w = w.at[ROUTER_GROUP_ROW:ROUTER_GROUP_ROW + N_EXPERT_GROUPS].set(w_rg.astype(F32).T)
    w = w.at[ROUTER_EXPERT_ROW:ROUTER_EXPERT_ROW + N_EXPERTS].set(w_re.astype(F32).T)
    hi = w.astype(BF16)
    lo = (w - hi.astype(F32)).astype(BF16)
    return jnp.concatenate([hi, lo], axis=0)


def kernel(x_prompt, x_sample, state_ssm_re, state_ssm_im, norm_mix, norm_ffn, norm_final, ssm_w_in, ssm_lambda_re, ssm_lambda_im, ssm_log_step, ssm_b_re, ssm_b_im, ssm_c_re, ssm_c_im, ssm_d, ssm_w_out, sgu_w_in, sgu_norm_g, sgu_norm_b, sgu_w_s, sgu_b_s, sgu_w_out, moe_router_group, moe_router_expert, moe_w_gate, moe_w_up, moe_w_down):
    nb, ns, dm = x_prompt.shape
    db, ds, _ = x_sample.shape
    n_p, n_s = nb * ns, db * ds
    n_tok = n_p + n_s
    n_sg = db // SUBLANES
    steps_p = MIX_ROWS // SUBLANES
    sg_per_block = MIX_ROWS // (ds * SUBLANES)
    assert nb == SUBLANES and dm == D_MODEL and ns % steps_p == 0 and n_sg % sg_per_block == 0
    assert n_p % MIX_ROWS == 0 and n_s % MIX_ROWS == 0 and MIX_ROWS % MOE_TILE == 0 and CHUNK % ds == 0
    row = lambda a: a.astype(F32).reshape(1, -1)
    zero_cnt = jnp.zeros((CNT_ROWS, LANES), F32)

    lb_re, lb_im, bb_re, bb_im = _s5_prep(ssm_lambda_re, ssm_lambda_im, ssm_log_step, ssm_b_re, ssm_b_im)
    bmat, cmat, a_re, a_im, dvec = _s5_matrices(lb_re, lb_im, bb_re, bb_im, ssm_c_re, ssm_c_im, ssm_d)
    xp = x_prompt.astype(F32).reshape(1, nb, ns, dm)
    xs = x_sample.astype(F32).reshape(n_sg, SUBLANES, ds, dm)
    zero_state = jnp.zeros((1, SUBLANES, STATE_COLS), F32)
    s0 = _pack_state(state_ssm_re.astype(F32), state_ssm_im.astype(F32)).reshape(n_sg, SUBLANES, STATE_COLS)
    wr0 = _router_weights(moe_router_group[0], moe_router_expert[0])
    s5 = functools.partial(_s5_layer, g=row(norm_mix[0]), win=ssm_w_in.astype(BF16), bmat=bmat, a_re=a_re,
                           a_im=a_im, cmat=cmat, dvec=dvec, wout=ssm_w_out.astype(BF16),
                           g_ffn=row(norm_ffn[0]), w_router=wr0, n_total_rows=n_tok)
    tok0, meta_p, st_p, cnt0 = s5(xp, zero_state, cnt_in=zero_cnt, tok_prev=None, block_off=0,
                                  n_seq_groups=1, steps=steps_p, carry=True)
    tok0, meta_s, st_s, cnt0 = s5(xs, s0, cnt_in=cnt0, tok_prev=tok0, block_off=n_p // MIX_ROWS,
                                  n_seq_groups=sg_per_block, steps=ds, carry=False)
    ssm_re_p, ssm_im_p = _unpack_state(st_p.reshape(nb, STATE_COLS))
    ssm_re_s, ssm_im_s = _unpack_state(st_s.reshape(db, STATE_COLS))
    ys0, pos0 = _moe_layer(tok0, jnp.concatenate([meta_p, meta_s], axis=1), cnt0, row(norm_ffn[0]),
                           moe_w_gate, moe_w_up, moe_w_down, row(norm_final), layer=0, final_norm=False)

    pos0 = jnp.concatenate([
        pos0[:n_p].reshape(ns, nb).T.reshape(-1),
        jnp.transpose(pos0[n_p:].reshape(n_sg, ds, SUBLANES), (0, 2, 1)).reshape(-1)])
    ws_p = sgu_w_s.astype(F32)[:, :CHUNK, :CHUNK]
    bs_p = jnp.repeat(sgu_b_s.astype(F32)[:, :CHUNK].T, SGU_HEAD_DIM, axis=1)
    reps = CHUNK // ds
    ws_s = jnp.tile(sgu_w_s.astype(F32)[:, :ds, :ds], (1, reps, reps))
    bs_s = jnp.tile(jnp.repeat(sgu_b_s.astype(F32)[:, :ds].T, SGU_HEAD_DIM, axis=1), (reps, 1))
    wr1 = _router_weights(moe_router_group[1], moe_router_expert[1])
    sgu = functools.partial(_sgu_layer, pos0, ys0, g=row(norm_mix[1]), win=sgu_w_in.astype(BF16),
                            ng=row(sgu_norm_g), nb=row(sgu_norm_b), wout=sgu_w_out.astype(BF16),
                            g_ffn=row(norm_ffn[1]), w_router=wr1, n_total_rows=n_tok)
    tok1, meta_p, cnt1 = sgu(ws=ws_p, bs=bs_p, cnt_in=zero_cnt, tok_prev=None, n_blocks=n_p // MIX_ROWS,
                             block_off=0, block_diag=False, emit_v=False)
    tok1, meta_s, cnt1, v_s = sgu(ws=ws_s, bs=bs_s, cnt_in=cnt1, tok_prev=tok1, n_blocks=n_s // MIX_ROWS,
                                  block_off=n_p // MIX_ROWS, block_diag=True, emit_v=True)
    ys1, pos1 = _moe_layer(tok1, jnp.concatenate([meta_p, meta_s], axis=1), cnt1, row(norm_ffn[1]),
                           moe_w_gate, moe_w_up, moe_w_down, row(norm_final), layer=1, final_norm=True)
    y_p, y_s = _ungather(ys1, pos1, n_p)
    return (y_p.reshape(nb, ns, dm), y_s.reshape(db, ds, dm), ssm_re_p, ssm_im_p, ssm_re_s, ssm_im_s,
            v_s.reshape(db, ds, SGU_WIDTH))
```

```python
import functools

import jax
import jax.numpy as jnp
from jax import lax
from jax.experimental import pallas as pl
from jax.experimental.pallas import tpu as pltpu

D_MODEL = 1024
SSM_WIDTH = D_MODEL // 2
SSM_GROUP = 16
SSM_GROUPS = SSM_WIDTH // SSM_GROUP
SSM_STATE = 64
SGU_WIDTH = D_MODEL
SGU_HEADS = 8
SGU_HEAD_DIM = SGU_WIDTH // SGU_HEADS
CHUNK = 128
N_EXPERT_GROUPS = 4
EXPERTS_PER_GROUP = 4
N_EXPERTS = N_EXPERT_GROUPS * EXPERTS_PER_GROUP
D_EXPERT = D_MODEL // 4
EPS = 1e-6

SUBLANES = 8
LANES = 128
VMEM_LIMIT_BYTES = 56 * 1024 * 1024
N_DMA_PRIORITIES = 2

SLAB_GROUPS = 16
N_SLABS = SSM_GROUPS // SLAB_GROUPS
SLAB_HALF = SLAB_GROUPS * SSM_STATE
SLAB = 2 * SLAB_HALF
STATE_COLS = N_SLABS * SLAB
SLAB_IN = SLAB_GROUPS * SSM_GROUP
REC_COLS = 512

TOKEN_CHUNKS = D_MODEL // LANES
assert TOKEN_CHUNKS == SUBLANES

PAIR_A = (0, 0, 0, 1, 1, 3)
PAIR_B = (1, 2, 3, 3, 2, 2)
N_PAIRS = len(PAIR_A)
GATHER_DEPTH = 3
GATHER_UNROLL = 16
CAST_ROWS = 256
ROUTED_ROWS = TOKEN_CHUNKS + 1
ROUTED_META_ROW = TOKEN_CHUNKS
N_BUCKETS = N_EXPERT_GROUPS * N_PAIRS
MOE_TILE = 256
MIX_ROWS = 512
META_ROWS = SUBLANES
META_BUCKET, META_W_A, META_W_B, META_RANK = 0, 1, 2, 3
ROUTER_LANES = LANES
ROUTER_GROUP_ROW, ROUTER_EXPERT_ROW = 0, SUBLANES
CNT_ROWS = 32
BF16 = jnp.bfloat16
F32 = jnp.float32


def _rms(x, g):
    return x * lax.rsqrt(jnp.mean(x * x, axis=-1, keepdims=True) + EPS) * g


def _s5_prep_kernel(lr_ref, li_ref, ls_ref, br_ref, bi_ref, lbr_ref, lbi_ref, bbr_ref, bbi_ref):
    lr = lr_ref[...]
    li = li_ref[...]
    dt = jnp.exp(ls_ref[...])
    mag = jnp.exp(lr * dt)
    ang = li * dt
    lb_re = mag * jnp.cos(ang)
    lb_im = mag * jnp.sin(ang)
    den = lr * lr + li * li
    nr = lb_re - 1.0
    coef_re = (nr * lr + lb_im * li) / den
    coef_im = (lb_im * lr - nr * li) / den
    br = br_ref[...]
    bi = bi_ref[...]
    lbr_ref[...] = lb_re
    lbi_ref[...] = lb_im
    bbr_ref[...] = coef_re * br - coef_im * bi
    bbi_ref[...] = coef_re * bi + coef_im * br


def _s5_prep(lam_re, lam_im, log_step, b_re, b_im):
    rows = SSM_GROUPS * SSM_GROUP
    rep = lambda a: jnp.repeat(a.astype(F32), SSM_GROUP, axis=0)
    lr = rep(lam_re)
    li = rep(lam_im)
    ls = rep(jnp.broadcast_to(log_step.astype(F32)[:, None], (SSM_GROUPS, SSM_STATE)))
    br = jnp.transpose(b_re.astype(F32), (0, 2, 1)).reshape(rows, SSM_STATE)
    bi = jnp.transpose(b_im.astype(F32), (0, 2, 1)).reshape(rows, SSM_STATE)
    sds = jax.ShapeDtypeStruct((rows, SSM_STATE), F32)
    lb_re, lb_im, bb_re, bb_im = pl.pallas_call(
        _s5_prep_kernel, out_shape=(sds, sds, sds, sds), name="s5_prep")(lr, li, ls, br, bi)
    return lb_re, lb_im, bb_re, bb_im


def _to_state_cols(a):
    return a.reshape(a.shape[:-2] + (N_SLABS, SLAB_HALF))


def _pack_state(re, im):
    return jnp.concatenate([_to_state_cols(re), _to_state_cols(im)], axis=-1).reshape(re.shape[0], STATE_COLS)


def _unpack_state(s):
    s = s.reshape(s.shape[0], N_SLABS, 2, SLAB_GROUPS, SSM_STATE)
    re = s[:, :, 0].reshape(s.shape[0], SSM_GROUPS, SSM_STATE)
    im = s[:, :, 1].reshape(s.shape[0], SSM_GROUPS, SSM_STATE)
    return re, im


def _s5_matrices(lb_re, lb_im, bb_re, bb_im, c_re, c_im, d):
    eye = jnp.eye(SLAB_GROUPS, dtype=F32)
    def in_side(bb):
        bb = bb.reshape(N_SLABS, SLAB_GROUPS, SSM_GROUP, SSM_STATE)
        m = bb[:, :, :, None, :] * eye[None, :, None, :, None]
        return m.reshape(N_SLABS, SLAB_IN, SLAB_HALF)
    bmat = jnp.concatenate([in_side(bb_re), in_side(bb_im)], axis=-1).astype(BF16)
    def out_side(c):
        c = jnp.transpose(c.astype(F32), (0, 2, 1)).reshape(N_SLABS, SLAB_GROUPS, SSM_STATE, SSM_GROUP)
        m = c[:, :, :, None, :] * eye[None, :, None, :, None]
        return m.reshape(N_SLABS, SLAB_HALF, SLAB_IN)
    cmat = jnp.concatenate([out_side(c_re), -out_side(c_im)], axis=1).astype(BF16)
    lam = lambda lb: _to_state_cols(lb.reshape(SSM_GROUPS, SSM_GROUP, SSM_STATE)[:, 0])
    a_re = jnp.concatenate([lam(lb_re), lam(lb_re)], axis=-1).reshape(1, STATE_COLS)
    a_im = jnp.concatenate([lam(lb_im), lam(lb_im)], axis=-1).reshape(1, STATE_COLS)
    return bmat, cmat, a_re, a_im, d.astype(F32).reshape(1, SSM_WIDTH)


def _rows_to_tokens(x, tok_ref):
    n = x.shape[0]
    for c in range(TOKEN_CHUNKS):
        tok_ref[pl.ds(c, n, stride=TOKEN_CHUNKS), :] = x[:, c * LANES:(c + 1) * LANES]


def _tokens_to_rows(tok_ref, n):
    return jnp.concatenate([tok_ref[pl.ds(c, n, stride=TOKEN_CHUNKS), :] for c in range(TOKEN_CHUNKS)], axis=-1)


def _start_token_gather(idx_ref, base, src_hbm, buf_ref, sem, n):
    def issue(blk, c):
        for k in range(GATHER_UNROLL):
            r = blk * GATHER_UNROLL + k
            tok = idx_ref[base + r]
            pltpu.make_async_copy(
                src_hbm.at[pl.ds(pl.multiple_of(tok * TOKEN_CHUNKS, TOKEN_CHUNKS), TOKEN_CHUNKS), :],
                buf_ref.at[pl.ds(pl.multiple_of(r * TOKEN_CHUNKS, TOKEN_CHUNKS), TOKEN_CHUNKS), :],
                sem).start(priority=k % N_DMA_PRIORITIES)
        return c

    lax.fori_loop(0, n // GATHER_UNROLL, issue, 0)


def _wait_token_gather(src_hbm, buf_ref, sem):
    pltpu.make_async_copy(src_hbm.at[pl.ds(0, buf_ref.shape[0]), :], buf_ref, sem).wait()


def _gather_prologue(idx_ref, first_block, n_blocks, src_hbm, buf_ref, sems, n):
    for k in range(GATHER_DEPTH - 1):
        blk = first_block + jnp.minimum(k, n_blocks - 1)
        _start_token_gather(idx_ref, blk * n, src_hbm, buf_ref.at[k], sems.at[k], n)


def _gather_step(i, idx_ref, first_block, n_blocks, src_hbm, buf_ref, sems, n):
    slot = lax.rem(i, GATHER_DEPTH)
    ahead = lax.rem(i + GATHER_DEPTH - 1, GATHER_DEPTH)
    _wait_token_gather(src_hbm, buf_ref.at[slot], sems.at[slot])
    blk = first_block + jnp.minimum(i + GATHER_DEPTH - 1, n_blocks - 1)
    _start_token_gather(idx_ref, blk * n, src_hbm, buf_ref.at[ahead], sems.at[ahead], n)
    return _tokens_to_rows(buf_ref.at[slot], n)


def _gather_epilogue(i, src_hbm, buf_ref, sems):
    for k in range(1, GATHER_DEPTH):
        slot = lax.rem(i + k, GATHER_DEPTH)
        _wait_token_gather(src_hbm, buf_ref.at[slot], sems.at[slot])


def _gathered_rows(idx_ref, first_block, n_blocks, src_hbm, buf_ref, sems, n):
    i = pl.program_id(0)

    @pl.when(i == 0)
    def _():
        _gather_prologue(idx_ref, first_block, n_blocks, src_hbm, buf_ref, sems, n)

    x = _gather_step(i, idx_ref, first_block, n_blocks, src_hbm, buf_ref, sems, n)

    @pl.when(i == n_blocks - 1)
    def _():
        _gather_epilogue(i, src_hbm, buf_ref, sems)

    return x


def _route_meta(x, g, wr_t, cnt_ref):
    xn = _rms(x, g)
    n = x.shape[0]
    xh = xn.astype(BF16)
    xl = (xn - xh.astype(F32)).astype(BF16)
    nt = (((1,), (1,)), ((), ()))
    p = lax.dot_general(wr_t, xh, nt, preferred_element_type=F32)
    q = lax.dot_general(wr_t[:ROUTER_LANES], xl, nt, preferred_element_type=F32)
    logits = p[:ROUTER_LANES] + (p[ROUTER_LANES:] + q)
    lg = [logits[ROUTER_GROUP_ROW + k:ROUTER_GROUP_ROW + k + 1] for k in range(N_EXPERT_GROUPS)]
    mg = functools.reduce(jnp.maximum, lg)
    g_idx = jnp.full((1, n), N_EXPERT_GROUPS - 1, jnp.int32)
    for k in reversed(range(N_EXPERT_GROUPS - 1)):
        g_idx = jnp.where(lg[k] == mg, k, g_idx)
    g_w = 1.0 / functools.reduce(jnp.add, [jnp.exp(l - mg) for l in lg])
    le = []
    for k in range(EXPERTS_PER_GROUP):
        row = lambda grp: logits[ROUTER_EXPERT_ROW + grp * EXPERTS_PER_GROUP + k:
                                 ROUTER_EXPERT_ROW + grp * EXPERTS_PER_GROUP + k + 1]
        v = row(N_EXPERT_GROUPS - 1)
        for grp in reversed(range(N_EXPERT_GROUPS - 1)):
            v = jnp.where(g_idx == grp, row(grp), v)
        le.append(v)
    m1 = functools.reduce(jnp.maximum, le)
    i1 = jnp.full((1, n), EXPERTS_PER_GROUP - 1, jnp.int32)
    for k in reversed(range(EXPERTS_PER_GROUP - 1)):
        i1 = jnp.where(le[k] == m1, k, i1)
    neg = jnp.float32(-jnp.inf)
    le2 = [jnp.where(i1 == k, neg, le[k]) for k in range(EXPERTS_PER_GROUP)]
    m2 = functools.reduce(jnp.maximum, le2)
    i2 = jnp.full((1, n), EXPERTS_PER_GROUP - 1, jnp.int32)
    for k in reversed(range(EXPERTS_PER_GROUP - 1)):
        i2 = jnp.where(le2[k] == m2, k, i2)
    e2 = jnp.exp(m2 - m1)
    w1 = g_w / (1.0 + e2)
    w2 = g_w * e2 / (1.0 + e2)
    lo = jnp.minimum(i1, i2)
    hi = jnp.maximum(i1, i2)
    pair = jnp.where(lo == 0, hi - 1, jnp.where(lo == 1, 6 - hi, 5))
    bucket = g_idx * N_PAIRS + pair
    w_lo = jnp.where(i1 < i2, w1, w2)
    w_hi = jnp.where(i1 < i2, w2, w1)
    w_a = jnp.where(pair == 5, w_hi, w_lo)
    w_b = jnp.where(pair == 5, w_lo, w_hi)
    onehot = lax.broadcasted_iota(jnp.int32, (CNT_ROWS, n), 0) == bucket
    oh = jnp.where(onehot, 1.0, 0.0)
    r_i = lax.broadcasted_iota(jnp.int32, (n, n), 0)
    c_i = lax.broadcasted_iota(jnp.int32, (n, n), 1)
    before = jnp.where(r_i < c_i, 1.0, 0.0).astype(BF16)
    prefix = jnp.dot(oh.astype(BF16), before, preferred_element_type=F32)
    cnt = cnt_ref[...]
    rank = jnp.sum(jnp.where(onehot, prefix + cnt[:, :1], 0.0), axis=0, keepdims=True)
    cnt_ref[...] = cnt + jnp.sum(oh, axis=1, keepdims=True)
    return jnp.concatenate([bucket.astype(F32), w_a, w_b, rank, jnp.zeros((META_ROWS - 4, n), F32)], axis=0)


def _emit_tokens(x, tok_ref, meta_ref, cnt_in_ref, cnt_ref, gf_ref, wr_ref):
    @pl.when(pl.program_id(0) == 0)
    def _():
        cnt_ref[...] = cnt_in_ref[...]
    n = x.shape[0]
    meta = _route_meta(x, gf_ref[...], wr_ref[...], cnt_ref)
    meta_ref[...] = meta
    for c in range(TOKEN_CHUNKS):
        tok_ref[pl.ds(c, n, stride=ROUTED_ROWS), :] = x[:, c * LANES:(c + 1) * LANES]
    tok_ref[pl.ds(ROUTED_META_ROW, n, stride=ROUTED_ROWS), :] = jnp.concatenate(
        [meta, jnp.zeros((LANES - META_ROWS, n), F32)], axis=0).T


def _const_spec(*shape):
    return pl.BlockSpec(shape, lambda i, *_: (0,) * len(shape))


_META_SPEC = pl.BlockSpec((META_ROWS, MIX_ROWS), lambda i, *_: (0, i))


def _tokens_alias(tok_prev, input_index):
    if tok_prev is None:
        return {}, jnp.zeros((SUBLANES, LANES), F32)
    return {input_index: 0}, tok_prev


def _tokens_out(n_total_rows, block_off):
    return (pl.BlockSpec((MIX_ROWS * ROUTED_ROWS, LANES), lambda i, *_: (i + block_off, 0)),
            jax.ShapeDtypeStruct((n_total_rows * ROUTED_ROWS, LANES), F32))


def _time_major_copies(x_hbm, xt_ref, sem, step, slot, n_seq_groups, steps):
    g0 = step * n_seq_groups if n_seq_groups > 1 else 0
    t0 = 0 if n_seq_groups > 1 else step * steps
    return [pltpu.make_async_copy(x_hbm.at[pl.ds(g0, n_seq_groups), b, pl.ds(t0, steps), :],
                                  xt_ref.at[slot, :, :, b, :], sem.at[slot]) for b in range(SUBLANES)]


def _s5_kernel(x_hbm, g_ref, win_ref, bm_ref, are_ref, aim_ref, cm_ref, d_ref, wout_ref, s0_ref,
               gf_ref, wr_ref, cnt_in_ref, tok_alias_ref,
               tok_ref, meta_ref, so_ref, cnt_ref, bu_ref, st_ref, xt_ref, xsem,
               *, n_blocks, n_seq_groups, steps, carry):
    del tok_alias_ref
    i = pl.program_id(0)
    slot = lax.rem(i, 2)
    copies = functools.partial(_time_major_copies, x_hbm, xt_ref, xsem, n_seq_groups=n_seq_groups, steps=steps)

    @pl.when(i == 0)
    def _():
        for c in copies(0, 0):
            c.start()

    @pl.when(i + 1 < n_blocks)
    def _():
        for c in copies(i + 1, 1 - slot):
            c.start()

    for c in copies(i, slot):
        c.wait()
    x = xt_ref[slot].reshape(MIX_ROWS, D_MODEL)
    h = _rms(x, g_ref[...]).astype(BF16)
    u = jnp.dot(h, win_ref[...], preferred_element_type=F32)
    ub = u.astype(BF16)
    for j in range(N_SLABS):
        bu_ref[:, j * SLAB:(j + 1) * SLAB] = jnp.dot(
            ub[:, j * SLAB_IN:(j + 1) * SLAB_IN], bm_ref[j], preferred_element_type=F32)

    if carry:
        @pl.when(pl.program_id(0) == 0)
        def _():
            st_ref[...] = s0_ref[0]

    for sg in range(n_seq_groups):
        for j in range(N_SLABS):
            for k in range(SLAB_HALF // REC_COLS):
                re0 = j * SLAB + k * REC_COLS
                im0 = re0 + SLAB_HALF
                ar = jnp.broadcast_to(are_ref[:, re0:re0 + REC_COLS], (SUBLANES, REC_COLS))
                ai = jnp.broadcast_to(aim_ref[:, re0:re0 + REC_COLS], (SUBLANES, REC_COLS))
                if carry:
                    sr = st_ref[:, re0:re0 + REC_COLS]
                    si = st_ref[:, im0:im0 + REC_COLS]
                else:
                    sr = s0_ref[sg, :, re0:re0 + REC_COLS]
                    si = s0_ref[sg, :, im0:im0 + REC_COLS]

                def step(t, c, re0=re0, im0=im0, ar=ar, ai=ai, sg=sg):
                    sr, si = c
                    row = pl.multiple_of((sg * steps + t) * SUBLANES, SUBLANES)
                    br = bu_ref[pl.ds(row, SUBLANES), re0:re0 + REC_COLS]
                    bi = bu_ref[pl.ds(row, SUBLANES), im0:im0 + REC_COLS]
                    nr = ar * sr - ai * si + br
                    ni = ar * si + ai * sr + bi
                    bu_ref[pl.ds(row, SUBLANES), re0:re0 + REC_COLS] = nr
                    bu_ref[pl.ds(row, SUBLANES), im0:im0 + REC_COLS] = ni
                    return nr, ni

                sr, si = lax.fori_loop(0, steps, step, (sr, si), unroll=min(steps, 8))
                if carry:
                    st_ref[:, re0:re0 + REC_COLS] = sr
                    st_ref[:, im0:im0 + REC_COLS] = si
                    so_ref[0, :, re0:re0 + REC_COLS] = sr
                    so_ref[0, :, im0:im0 + REC_COLS] = si
                else:
                    so_ref[sg, :, re0:re0 + REC_COLS] = sr
                    so_ref[sg, :, im0:im0 + REC_COLS] = si

    ys = [jnp.dot(bu_ref[:, j * SLAB:(j + 1) * SLAB].astype(BF16), cm_ref[j], preferred_element_type=F32)
          for j in range(N_SLABS)]
    y = jnp.concatenate(ys, axis=-1) + d_ref[...] * u
    y = jax.nn.gelu(y).astype(BF16)
    a = jnp.dot(y, wout_ref[...], preferred_element_type=F32)
    xo = x + a[:, :D_MODEL] * jax.nn.sigmoid(a[:, D_MODEL:])
    _emit_tokens(xo, tok_ref, meta_ref, cnt_in_ref, cnt_ref, gf_ref, wr_ref)


def _s5_layer(x_seq, s0, g, win, bmat, a_re, a_im, cmat, dvec, wout, g_ffn, w_router, cnt_in, tok_prev,
              *, n_total_rows, block_off, n_seq_groups, steps, carry):
    assert n_seq_groups * steps * SUBLANES == MIX_ROWS and x_seq.shape[1] == SUBLANES
    assert (n_seq_groups == 1 and x_seq.shape[0] == 1) or steps == x_seq.shape[2]
    n_blocks = x_seq.shape[0] * x_seq.shape[2] // (n_seq_groups * steps)
    s_idx = (lambda i: (0, 0, 0)) if carry else (lambda i: (i, 0, 0))
    kern = functools.partial(_s5_kernel, n_blocks=n_blocks, n_seq_groups=n_seq_groups, steps=steps, carry=carry)
    n_state_groups = 1 if carry else n_blocks * n_seq_groups
    tok_spec, tok_shape = _tokens_out(n_total_rows, block_off)
    aliases, tok_prev = _tokens_alias(tok_prev, 13)
    inputs = [x_seq, g, win, bmat, a_re, a_im, cmat, dvec, wout, s0, g_ffn, w_router, cnt_in, tok_prev]
    return pl.pallas_call(
        kern,
        grid=(n_blocks,),
        in_specs=[
            pl.BlockSpec(memory_space=pl.ANY),
            _const_spec(1, D_MODEL), _const_spec(D_MODEL, SSM_WIDTH), _const_spec(N_SLABS, SLAB_IN, SLAB),
            _const_spec(1, STATE_COLS), _const_spec(1, STATE_COLS), _const_spec(N_SLABS, SLAB, SLAB_IN),
            _const_spec(1, SSM_WIDTH), _const_spec(SSM_WIDTH, 2 * D_MODEL),
            pl.BlockSpec((n_seq_groups, SUBLANES, STATE_COLS), s_idx),
            _const_spec(1, D_MODEL), _const_spec(2 * ROUTER_LANES, D_MODEL), _const_spec(CNT_ROWS, LANES),
            pl.BlockSpec(memory_space=pl.ANY),
        ],
        out_specs=[
            tok_spec, _META_SPEC,
            pl.BlockSpec((n_seq_groups, SUBLANES, STATE_COLS), s_idx),
            _const_spec(CNT_ROWS, LANES),
        ],
        out_shape=[
            tok_shape, jax.ShapeDtypeStruct((META_ROWS, n_blocks * MIX_ROWS), F32),
            jax.ShapeDtypeStruct((n_state_groups, SUBLANES, STATE_COLS), F32),
            jax.ShapeDtypeStruct((CNT_ROWS, LANES), F32),
        ],
        input_output_aliases=aliases,
        scratch_shapes=[pltpu.VMEM((MIX_ROWS, STATE_COLS), F32), pltpu.VMEM((SUBLANES, STATE_COLS), F32),
                        pltpu.VMEM((2, n_seq_groups, steps, SUBLANES, D_MODEL), F32),
                        pltpu.SemaphoreType.DMA((2,))],
        compiler_params=pltpu.CompilerParams(
            dimension_semantics=("arbitrary",), vmem_limit_bytes=VMEM_LIMIT_BYTES),
        name="s5_carry" if carry else "s5_step",
    )(*inputs)


def _sgu_kernel(pos_ref, src_hbm, g_ref, win_ref, ng_ref, nb_ref, ws_ref, bs_ref, wout_ref,
                gf_ref, wr_ref, cnt_in_ref, tok_alias_ref, *refs, n_blocks, block_off, block_diag, emit_v):
    del tok_alias_ref
    tok_ref, meta_ref, cnt_ref = refs[:3]
    v_ref = refs[3] if emit_v else None
    xbuf, sems = refs[-2:]
    x = _gathered_rows(pos_ref, block_off, n_blocks, src_hbm, xbuf, sems, MIX_ROWS)
    rows = x.shape[0]
    h = _rms(x, g_ref[...]).astype(BF16)
    hh = jax.nn.gelu(jnp.dot(h, win_ref[...], preferred_element_type=F32))
    u = hh[:, :SGU_WIDTH]
    v = hh[:, SGU_WIDTH:]
    vc = v - jnp.mean(v, axis=-1, keepdims=True)
    v = vc * lax.rsqrt(jnp.mean(vc * vc, axis=-1, keepdims=True) + EPS) * ng_ref[...] + nb_ref[...]
    if emit_v:
        v_ref[...] = v
    r = lax.broadcasted_iota(jnp.int32, (CHUNK, CHUNK), 0)
    c = lax.broadcasted_iota(jnp.int32, (CHUNK, CHUNK), 1)
    if block_diag:
        keep = (r // SUBLANES == c // SUBLANES) & (r >= c)
    else:
        keep = r >= c
    vb = v.astype(BF16)
    n_chunks = rows // CHUNK
    zs = []
    for hd in range(SGU_HEADS):
        wm = jnp.where(keep, ws_ref[hd], 0.0).astype(BF16)
        cols = slice(hd * SGU_HEAD_DIM, (hd + 1) * SGU_HEAD_DIM)
        vcat = jnp.concatenate([vb[ck * CHUNK:(ck + 1) * CHUNK, cols] for ck in range(n_chunks)], axis=-1)
        zcat = jnp.dot(wm, vcat, preferred_element_type=F32)
        zs.append([zcat[:, ck * SGU_HEAD_DIM:(ck + 1) * SGU_HEAD_DIM] for ck in range(n_chunks)])
    z = jnp.concatenate(
        [jnp.concatenate([zs[hd][ck] for hd in range(SGU_HEADS)], axis=-1) + bs_ref[...]
         for ck in range(n_chunks)], axis=0)
    o = jnp.dot((u * z).astype(BF16), wout_ref[...], preferred_element_type=F32)
    _emit_tokens(x + o, tok_ref, meta_ref, cnt_in_ref, cnt_ref, gf_ref, wr_ref)


def _sgu_layer(pos, src_tokens, g, win, ng, nb, ws, bs, wout, g_ffn, w_router, cnt_in, tok_prev,
               *, n_total_rows, n_blocks, block_off, block_diag, emit_v):
    tok_spec, tok_shape = _tokens_out(n_total_rows, block_off)
    out_shape = [tok_shape, jax.ShapeDtypeStruct((META_ROWS, n_blocks * MIX_ROWS), F32),
                 jax.ShapeDtypeStruct((CNT_ROWS, LANES), F32)]
    out_specs = [tok_spec, _META_SPEC, _const_spec(CNT_ROWS, LANES)]
    if emit_v:
        out_shape.append(jax.ShapeDtypeStruct((n_blocks * MIX_ROWS, SGU_WIDTH), F32))
        out_specs.append(pl.BlockSpec((MIX_ROWS, SGU_WIDTH), lambda i, *_: (i, 0)))
    aliases, tok_prev = _tokens_alias(tok_prev, 12)
    inputs = [src_tokens, g, win, ng, nb, ws, bs, wout, g_ffn, w_router, cnt_in, tok_prev]
    return pl.pallas_call(
        functools.partial(_sgu_kernel, n_blocks=n_blocks, block_off=block_off, block_diag=block_diag,
                          emit_v=emit_v),
        grid_spec=pltpu.PrefetchScalarGridSpec(
            num_scalar_prefetch=1,
            grid=(n_blocks,),
            in_specs=[pl.BlockSpec(memory_space=pl.ANY), _const_spec(1, D_MODEL),
                      _const_spec(D_MODEL, 2 * SGU_WIDTH), _const_spec(1, SGU_WIDTH), _const_spec(1, SGU_WIDTH),
                      _const_spec(SGU_HEADS, CHUNK, CHUNK), _const_spec(CHUNK, SGU_WIDTH),
                      _const_spec(SGU_WIDTH, D_MODEL), _const_spec(1, D_MODEL),
                      _const_spec(2 * ROUTER_LANES, D_MODEL), _const_spec(CNT_ROWS, LANES),
                      pl.BlockSpec(memory_space=pl.ANY)],
            out_specs=out_specs,
            scratch_shapes=[pltpu.VMEM((GATHER_DEPTH, MIX_ROWS * TOKEN_CHUNKS, LANES), F32),
                            pltpu.SemaphoreType.DMA((GATHER_DEPTH,))],
        ),
        out_shape=out_shape,
        input_output_aliases=aliases,
        compiler_params=pltpu.CompilerParams(
            dimension_semantics=("arbitrary",), vmem_limit_bytes=VMEM_LIMIT_BYTES),
        name="sgu_block_diag" if block_diag else "sgu_chunked",
    )(pos, *inputs)


def _moe_plan(meta, cnt, n_tiles_max):
    i32 = jnp.int32
    n_tok = meta.shape[1]
    bucket = meta[META_BUCKET].astype(i32)
    rank = meta[META_RANK].astype(i32)
    n = cnt[:N_BUCKETS, 0].astype(i32)
    nt = (n + MOE_TILE - 1) // MOE_TILE
    tile_end = jnp.cumsum(nt)
    tile_start = tile_end - nt
    pos = tile_start[bucket] * MOE_TILE + rank
    n_tiles = tile_end[-1]
    j = jnp.minimum(jnp.arange(n_tiles_max, dtype=i32), jnp.maximum(n_tiles - 1, 0))
    tb = jnp.minimum(jnp.sum((j[:, None] >= tile_end[None, :]).astype(i32), axis=1), N_BUCKETS - 1)
    grp = tb // N_PAIRS
    pair = tb % N_PAIRS
    e_a = grp * EXPERTS_PER_GROUP + jnp.asarray(PAIR_A, i32)[pair]
    e_b = grp * EXPERTS_PER_GROUP + jnp.asarray(PAIR_B, i32)[pair]
    zslot = jnp.where(nt > 0, (tile_end - 1) * MOE_TILE, -1).astype(i32)
    return pos, zslot, e_a, e_b, n_tiles.reshape(1)


def _dispatch_kernel(pos_ref, zslot_ref, tok_ref, xs_ref, zbuf_ref, sem, zsem):
    i = pl.program_id(0)
    n = tok_ref.shape[0] // ROUTED_ROWS
    routed = lambda ref, first, count: ref.at[pl.ds(first * ROUTED_ROWS, count * ROUTED_ROWS), :]

    @pl.when(i == 0)
    def _():
        zbuf_ref[...] = jnp.zeros_like(zbuf_ref)
        zero_copy = lambda b: pltpu.make_async_copy(zbuf_ref, routed(xs_ref, zslot_ref[b], MOE_TILE), zsem)
        for b in range(N_BUCKETS):
            @pl.when(zslot_ref[b] >= 0)
            def _():
                zero_copy(b).start()
        for b in range(N_BUCKETS):
            @pl.when(zslot_ref[b] >= 0)
            def _():
                zero_copy(b).wait()

    def issue(blk, c):
        for k in range(GATHER_UNROLL):
            r = blk * GATHER_UNROLL + k
            pltpu.make_async_copy(routed(tok_ref, r, 1), routed(xs_ref, pos_ref[i * n + r], 1), sem).start(
                priority=k % N_DMA_PRIORITIES)
        return c

    lax.fori_loop(0, n // GATHER_UNROLL, issue, 0)
    pltpu.make_async_copy(tok_ref, routed(xs_ref, 0, n), sem).wait()


def _dispatch(tokens, pos, zslot, n_slots):
    return pl.pallas_call(
        _dispatch_kernel,
        grid_spec=pltpu.PrefetchScalarGridSpec(
            num_scalar_prefetch=2,
            grid=(tokens.shape[0] // (MIX_ROWS * ROUTED_ROWS),),
            in_specs=[pl.BlockSpec((MIX_ROWS * ROUTED_ROWS, LANES), lambda i, pos, zslot: (i, 0))],
            out_specs=pl.BlockSpec(memory_space=pl.ANY),
            scratch_shapes=[pltpu.VMEM((MOE_TILE * ROUTED_ROWS, LANES), F32), pltpu.SemaphoreType.DMA,
                            pltpu.SemaphoreType.DMA],
        ),
        out_shape=jax.ShapeDtypeStruct((n_slots * ROUTED_ROWS, LANES), F32),
        compiler_params=pltpu.CompilerParams(
            dimension_semantics=("arbitrary",), vmem_limit_bytes=VMEM_LIMIT_BYTES),
        name="moe_dispatch",
    )(pos, zslot, tokens)


def _moe_kernel(ea_ref, eb_ref, nt_ref, xs_ref, g_ref, wg_a, wu_a, wd_a, wg_b, wu_b, wd_b, gf_ref,
                ys_ref, wgu_ref, wd_ref, *, final_norm):
    j = pl.program_id(0)

    @pl.when(j < nt_ref[0])
    def _():
        prev = jnp.maximum(j - 1, 0)
        for s, (e_ref, wg, wu, wd) in enumerate(((ea_ref, wg_a, wu_a, wd_a), (eb_ref, wg_b, wu_b, wd_b))):
            @pl.when((j == 0) | (e_ref[j] != e_ref[prev]))
            def _():
                def cast(c, carry, s=s, wg=wg, wu=wu):
                    rows = pl.ds(pl.multiple_of(c * CAST_ROWS, CAST_ROWS), CAST_ROWS)
                    wgu_ref[s, rows, :D_EXPERT] = wg[0, rows, :].astype(BF16)
                    wgu_ref[s, rows, D_EXPERT:] = wu[0, rows, :].astype(BF16)
                    return carry

                lax.fori_loop(0, D_MODEL // CAST_ROWS, cast, 0)
                wd_ref[s] = wd[0].astype(BF16)

        x = jnp.concatenate([xs_ref[pl.ds(c, MOE_TILE, stride=ROUTED_ROWS), :]
                             for c in range(TOKEN_CHUNKS)], axis=-1)
        record = xs_ref[pl.ds(ROUTED_META_ROW, MOE_TILE, stride=ROUTED_ROWS), :]
        xn = _rms(x, g_ref[...]).astype(BF16)
        out = None
        for s, lane in enumerate((META_W_A, META_W_B)):
            gu = jnp.dot(xn, wgu_ref[s], preferred_element_type=F32)
            gate = record[:, lane:lane + 1]
            hcur = (jax.nn.silu(gu[:, :D_EXPERT]) * gu[:, D_EXPERT:] * gate).astype(BF16)
            o = jnp.dot(hcur, wd_ref[s], preferred_element_type=F32)
            out = o if out is None else out + o
        y = x + out
        if final_norm:
            y = _rms(y, gf_ref[...])
        _rows_to_tokens(y, ys_ref)


def _moe_experts(xs, e_a, e_b, n_tiles, g, w_gate, w_up, w_down, g_final, *, layer, final_norm):
    n_tiles_max = e_a.shape[0]
    last = lambda j, nt: jnp.minimum(j, jnp.maximum(nt[0] - 1, 0))
    w_in = lambda which: pl.BlockSpec(
        (None, 1, D_MODEL, D_EXPERT), lambda j, ea, eb, nt: (layer, (ea, eb)[which][j], 0, 0))
    w_out = lambda which: pl.BlockSpec(
        (None, 1, D_EXPERT, D_MODEL), lambda j, ea, eb, nt: (layer, (ea, eb)[which][j], 0, 0))
    return pl.pallas_call(
        functools.partial(_moe_kernel, final_norm=final_norm),
        grid_spec=pltpu.PrefetchScalarGridSpec(
            num_scalar_prefetch=3,
            grid=(n_tiles_max,),
            in_specs=[pl.BlockSpec((MOE_TILE * ROUTED_ROWS, LANES), lambda j, ea, eb, nt: (last(j, nt), 0)),
                      _const_spec(1, D_MODEL), w_in(0), w_in(0), w_out(0), w_in(1), w_in(1), w_out(1),
                      _const_spec(1, D_MODEL)],
            out_specs=pl.BlockSpec((MOE_TILE * TOKEN_CHUNKS, LANES), lambda j, ea, eb, nt: (last(j, nt), 0)),
            scratch_shapes=[pltpu.VMEM((2, D_MODEL, 2 * D_EXPERT), BF16), pltpu.VMEM((2, D_EXPERT, D_MODEL), BF16)],
        ),
        out_shape=jax.ShapeDtypeStruct((n_tiles_max * MOE_TILE * TOKEN_CHUNKS, LANES), F32),
        compiler_params=pltpu.CompilerParams(
            dimension_semantics=("arbitrary",), vmem_limit_bytes=VMEM_LIMIT_BYTES),
        name="moe_experts_final" if final_norm else "moe_experts",
    )(e_a, e_b, n_tiles, xs, g, w_gate, w_up, w_down, w_gate, w_up, w_down, g_final)


def _moe_layer(tokens, meta, cnt, g, w_gate, w_up, w_down, g_final, *, layer, final_norm):
    n_tiles_max = meta.shape[1] // MOE_TILE + N_BUCKETS
    pos, zslot, e_a, e_b, n_tiles = _moe_plan(meta, cnt, n_tiles_max)
    xs = _dispatch(tokens, pos, zslot, n_tiles_max * MOE_TILE)
    ys = _moe_experts(xs, e_a, e_b, n_tiles, g, w_gate, w_up, w_down, g_final, layer=layer,
                      final_norm=final_norm)
    return ys, pos


def _ungather_kernel(pos_ref, src_hbm, *refs, n_blocks, n_first_blocks):
    o_refs, (xbuf, sems) = refs[:-2], refs[-2:]
    i = pl.program_id(0)
    x = _gathered_rows(pos_ref, 0, n_blocks, src_hbm, xbuf, sems, MIX_ROWS)

    @pl.when(i < n_first_blocks)
    def _():
        o_refs[0][...] = x

    @pl.when(i >= n_first_blocks)
    def _():
        o_refs[1][...] = x


def _ungather(ys, pos, n_first_rows):
    n_blocks = pos.shape[0] // MIX_ROWS
    n_first = n_first_rows // MIX_ROWS
    spec = lambda f: pl.BlockSpec((MIX_ROWS, D_MODEL), f)
    return pl.pallas_call(
        functools.partial(_ungather_kernel, n_blocks=n_blocks, n_first_blocks=n_first),
        grid_spec=pltpu.PrefetchScalarGridSpec(
            num_scalar_prefetch=1,
            grid=(n_blocks,),
            in_specs=[pl.BlockSpec(memory_space=pl.ANY)],
            out_specs=[spec(lambda i, pos: (jnp.minimum(i, n_first - 1), 0)),
                       spec(lambda i, pos: (jnp.maximum(i - n_first, 0), 0))],
            scratch_shapes=[pltpu.VMEM((GATHER_DEPTH, MIX_ROWS * TOKEN_CHUNKS, LANES), F32),
                            pltpu.SemaphoreType.DMA((GATHER_DEPTH,))],
        ),
        out_shape=[jax.ShapeDtypeStruct((n_first_rows, D_MODEL), F32),
                   jax.ShapeDtypeStruct((pos.shape[0] - n_first_rows, D_MODEL), F32)],
        compiler_params=pltpu.CompilerParams(
            dimension_semantics=("arbitrary",), vmem_limit_bytes=VMEM_LIMIT_BYTES),
        name="moe_ungather",
    )(pos, ys)


def _router_weights(w_rg, w_re):
    w = jnp.zeros((ROUTER_LANES, D_MODEL), F32)
    w = w.at[ROUTER_GROUP_ROW:ROUTER_GROUP_ROW + N_EXPERT_GROUPS].set(w_rg.astype(F32).T)
    w = w.at[ROUTER_EXPERT_ROW:ROUTER_EXPERT_ROW + N_EXPERTS].set(w_re.astype(F32).T)
    hi = w.astype(BF16)
    lo = (w - hi.astype(F32)).astype(BF16)
    return jnp.concatenate([hi, lo], axis=0)


def kernel(x_prompt, x_sample, state_ssm_re, state_ssm_im, norm_mix, norm_ffn, norm_final, ssm_w_in, ssm_lambda_re, ssm_lambda_im, ssm_log_step, ssm_b_re, ssm_b_im, ssm_c_re, ssm_c_im, ssm_d, ssm_w_out, sgu_w_in, sgu_norm_g, sgu_norm_b, sgu_w_s, sgu_b_s, sgu_w_out, moe_router_group, moe_router_expert, moe_w_gate, moe_w_up, moe_w_down):
    nb, ns, dm = x_prompt.shape
    db, ds, _ = x_sample.shape
    n_p, n_s = nb * ns, db * ds
    n_tok = n_p + n_s
    n_sg = db // SUBLANES
    steps_p = MIX_ROWS // SUBLANES
    sg_per_block = MIX_ROWS // (ds * SUBLANES)
    assert nb == SUBLANES and dm == D_MODEL and ns % steps_p == 0 and n_sg % sg_per_block == 0
    assert n_p % MIX_ROWS == 0 and n_s % MIX_ROWS == 0 and MIX_ROWS % MOE_TILE == 0 and CHUNK % ds == 0
    row = lambda a: a.astype(F32).reshape(1, -1)
    zero_cnt = jnp.zeros((CNT_ROWS, LANES), F32)

    lb_re, lb_im, bb_re, bb_im = _s5_prep(ssm_lambda_re, ssm_lambda_im, ssm_log_step, ssm_b_re, ssm_b_im)
    bmat, cmat, a_re, a_im, dvec = _s5_matrices(lb_re, lb_im, bb_re, bb_im, ssm_c_re, ssm_c_im, ssm_d)
    xp = x_prompt.astype(F32).reshape(1, nb, ns, dm)
    xs = x_sample.astype(F32).reshape(n_sg, SUBLANES, ds, dm)
    zero_state = jnp.zeros((1, SUBLANES, STATE_COLS), F32)
    s0 = _pack_state(state_ssm_re.astype(F32), state_ssm_im.astype(F32)).reshape(n_sg, SUBLANES, STATE_COLS)
    wr0 = _router_weights(moe_router_group[0], moe_router_expert[0])
    s5 = functools.partial(_s5_layer, g=row(norm_mix[0]), win=ssm_w_in.astype(BF16), bmat=bmat, a_re=a_re,
                           a_im=a_im, cmat=cmat, dvec=dvec, wout=ssm_w_out.astype(BF16),
                           g_ffn=row(norm_ffn[0]), w_router=wr0, n_total_rows=n_tok)
    tok0, meta_p, st_p, cnt0 = s5(xp, zero_state, cnt_in=zero_cnt, tok_prev=None, block_off=0,
                                  n_seq_groups=1, steps=steps_p, carry=True)
    tok0, meta_s, st_s, cnt0 = s5(xs, s0, cnt_in=cnt0, tok_prev=tok0, block_off=n_p // MIX_ROWS,
                                  n_seq_groups=sg_per_block, steps=ds, carry=False)
    ssm_re_p, ssm_im_p = _unpack_state(st_p.reshape(nb, STATE_COLS))
    ssm_re_s, ssm_im_s = _unpack_state(st_s.reshape(db, STATE_COLS))
    ys0, pos0 = _moe_layer(tok0, jnp.concatenate([meta_p, meta_s], axis=1), cnt0, row(norm_ffn[0]),
                           moe_w_gate, moe_w_up, moe_w_down, row(norm_final), layer=0, final_norm=False)

    pos0 = jnp.concatenate([
        pos0[:n_p].reshape(ns, nb).T.reshape(-1),
        jnp.transpose(pos0[n_p:].reshape(n_sg, ds, SUBLANES), (0, 2, 1)).reshape(-1)])
    ws_p = sgu_w_s.astype(F32)[:, :CHUNK, :CHUNK]
    bs_p = jnp.repeat(sgu_b_s.astype(F32)[:, :CHUNK].T, SGU_HEAD_DIM, axis=1)
    reps = CHUNK // ds
    ws_s = jnp.tile(sgu_w_s.astype(F32)[:, :ds, :ds], (1, reps, reps))
    bs_s = jnp.tile(jnp.repeat(sgu_b_s.astype(F32)[:, :ds].T, SGU_HEAD_DIM, axis=1), (reps, 1))
    wr1 = _router_weights(moe_router_group[1], moe_router_expert[1])
    sgu = functools.partial(_sgu_layer, pos0, ys0, g=row(norm_mix[1]), win=sgu_w_in.astype(BF16),
                            ng=row(sgu_norm_g), nb=row(sgu_norm_b), wout=sgu_w_out.astype(BF16),
                            g_ffn=row(norm_ffn[1]), w_router=wr1, n_total_rows=n_tok)
    tok1, meta_p, cnt1 = sgu(ws=ws_p, bs=bs_p, cnt_in=zero_cnt, tok_prev=None, n_blocks=n_p // MIX_ROWS,
                             block_off=0, block_diag=False, emit_v=False)
    tok1, meta_s, cnt1, v_s = sgu(ws=ws_s, bs=bs_s, cnt_in=cnt1, tok_prev=tok1, n_blocks=n_s // MIX_ROWS,
                                  block_off=n_p // MIX_ROWS, block_diag=True, emit_v=True)
    ys1, pos1 = _moe_layer(tok1, jnp.concatenate([meta_p, meta_s], axis=1), cnt1, row(norm_ffn[1]),
                           moe_w_gate, moe_w_up, moe_w_down, row(norm_final), layer=1, final_norm=True)
    y_p, y_s = _ungather(ys1, pos1, n_p)
    return (y_p.reshape(nb, ns, dm), y_s.reshape(db, ds, dm), ssm_re_p, ssm_im_p, ssm_re_s, ssm_im_s,
            v_s.reshape(db, ds, SGU_WIDTH))
```

```python
import functools

import jax
import jax.numpy as jnp
from jax import lax
from jax.experimental import pallas as pl
from jax.experimental.pallas import tpu as pltpu

D_MODEL = 1024
SSM_WIDTH = D_MODEL // 2
SSM_GROUP = 16
SSM_GROUPS = SSM_WIDTH // SSM_GROUP
SSM_STATE = 64
SGU_WIDTH = D_MODEL
SGU_HEADS = 8
SGU_HEAD_DIM = SGU_WIDTH // SGU_HEADS
CHUNK = 128
N_EXPERT_GROUPS = 4
EXPERTS_PER_GROUP = 4
N_EXPERTS = N_EXPERT_GROUPS * EXPERTS_PER_GROUP
D_EXPERT = D_MODEL // 4
EPS = 1e-6

SUBLANES = 8
LANES = 128
VMEM_LIMIT_BYTES = 56 * 1024 * 1024
N_DMA_PRIORITIES = 2

SLAB_GROUPS = 16
N_SLABS = SSM_GROUPS // SLAB_GROUPS
SLAB_HALF = SLAB_GROUPS * SSM_STATE
SLAB = 2 * SLAB_HALF
STATE_COLS = N_SLABS * SLAB
STATE_HALF = SSM_GROUPS * SSM_STATE
SLAB_IN = SLAB_GROUPS * SSM_GROUP
REC_COLS = 1024

TOKEN_CHUNKS = D_MODEL // LANES
assert TOKEN_CHUNKS == SUBLANES

PAIR_A = (0, 0, 0, 1, 1, 3)
PAIR_B = (1, 2, 3, 3, 2, 2)
N_PAIRS = len(PAIR_A)
GATHER_DEPTH = 3
GATHER_UNROLL = 16
CAST_ROWS = 256
ROUTED_ROWS = TOKEN_CHUNKS + 1
ROUTED_META_ROW = TOKEN_CHUNKS
N_BUCKETS = N_EXPERT_GROUPS * N_PAIRS
MOE_TILE = 256
MIX_ROWS = 512
META_ROWS = SUBLANES
META_BUCKET, META_W_A, META_W_B, META_RANK = 0, 1, 2, 3
ROUTER_LANES = LANES
ROUTER_GROUP_ROW, ROUTER_EXPERT_ROW = 0, SUBLANES
CNT_ROWS = 32
BF16 = jnp.bfloat16
F32 = jnp.float32


def _rms(x, g):
    return x * lax.rsqrt(jnp.mean(x * x, axis=-1, keepdims=True) + EPS) * g


def _s5_prep_kernel(lr_ref, li_ref, ls_ref, br_ref, bi_ref, lbr_ref, lbi_ref, bbr_ref, bbi_ref):
    lr = lr_ref[...]
    li = li_ref[...]
    dt = jnp.exp(ls_ref[...])
    mag = jnp.exp(lr * dt)
    ang = li * dt
    lb_re = mag * jnp.cos(ang)
    lb_im = mag * jnp.sin(ang)
    den = lr * lr + li * li
    nr = lb_re - 1.0
    coef_re = (nr * lr + lb_im * li) / den
    coef_im = (lb_im * lr - nr * li) / den
    br = br_ref[...]
    bi = bi_ref[...]
    lbr_ref[...] = lb_re
    lbi_ref[...] = lb_im
    bbr_ref[...] = coef_re * br - coef_im * bi
    bbi_ref[...] = coef_re * bi + coef_im * br


def _s5_prep(lam_re, lam_im, log_step, b_re, b_im):
    rows = SSM_GROUPS * SSM_GROUP
    rep = lambda a: jnp.repeat(a.astype(F32), SSM_GROUP, axis=0)
    lr = rep(lam_re)
    li = rep(lam_im)
    ls = rep(jnp.broadcast_to(log_step.astype(F32)[:, None], (SSM_GROUPS, SSM_STATE)))
    br = jnp.transpose(b_re.astype(F32), (0, 2, 1)).reshape(rows, SSM_STATE)
    bi = jnp.transpose(b_im.astype(F32), (0, 2, 1)).reshape(rows, SSM_STATE)
    sds = jax.ShapeDtypeStruct((rows, SSM_STATE), F32)
    lb_re, lb_im, bb_re, bb_im = pl.pallas_call(
        _s5_prep_kernel, out_shape=(sds, sds, sds, sds), name="s5_prep")(lr, li, ls, br, bi)
    return lb_re, lb_im, bb_re, bb_im


def _to_state_cols(a):
    return a.reshape(a.shape[:-2] + (N_SLABS, SLAB_HALF))


def _s5_matrices(lb_re, lb_im, bb_re, bb_im, c_re, c_im, d):
    eye = jnp.eye(SLAB_GROUPS, dtype=F32)
    def in_side(bb):
        bb = bb.reshape(N_SLABS, SLAB_GROUPS, SSM_GROUP, SSM_STATE)
        m = bb[:, :, :, None, :] * eye[None, :, None, :, None]
        return m.reshape(N_SLABS, SLAB_IN, SLAB_HALF)
    bmat = jnp.concatenate([in_side(bb_re), in_side(bb_im)], axis=-1).astype(BF16)
    def out_side(c):
        c = jnp.transpose(c.astype(F32), (0, 2, 1)).reshape(N_SLABS, SLAB_GROUPS, SSM_STATE, SSM_GROUP)
        m = c[:, :, :, None, :] * eye[None, :, None, :, None]
        return m.reshape(N_SLABS, SLAB_HALF, SLAB_IN)
    cmat = jnp.concatenate([out_side(c_re), -out_side(c_im)], axis=1).astype(BF16)
    lam = lambda lb: _to_state_cols(lb.reshape(SSM_GROUPS, SSM_GROUP, SSM_STATE)[:, 0])
    a_re = jnp.concatenate([lam(lb_re), lam(lb_re)], axis=-1).reshape(1, STATE_COLS)
    a_im = jnp.concatenate([lam(lb_im), lam(lb_im)], axis=-1).reshape(1, STATE_COLS)
    return bmat, cmat, a_re, a_im, d.astype(F32).reshape(1, SSM_WIDTH)


def _rows_to_tokens(x, tok_ref):
    n = x.shape[0]
    for c in range(TOKEN_CHUNKS):
        tok_ref[pl.ds(c, n, stride=TOKEN_CHUNKS), :] = x[:, c * LANES:(c + 1) * LANES]


def _tokens_to_rows(tok_ref, n):
    return jnp.concatenate([tok_ref[pl.ds(c, n, stride=TOKEN_CHUNKS), :] for c in range(TOKEN_CHUNKS)], axis=-1)


def _start_token_gather(idx_ref, base, src_hbm, buf_ref, sem, n):
    def issue(blk, c):
        for k in range(GATHER_UNROLL):
            r = blk * GATHER_UNROLL + k
            tok = idx_ref[base + r]
            pltpu.make_async_copy(
                src_hbm.at[pl.ds(pl.multiple_of(tok * TOKEN_CHUNKS, TOKEN_CHUNKS), TOKEN_CHUNKS), :],
                buf_ref.at[pl.ds(pl.multiple_of(r * TOKEN_CHUNKS, TOKEN_CHUNKS), TOKEN_CHUNKS), :],
                sem).start(priority=k % N_DMA_PRIORITIES)
        return c

    lax.fori_loop(0, n // GATHER_UNROLL, issue, 0)


def _wait_token_gather(src_hbm, buf_ref, sem):
    pltpu.make_async_copy(src_hbm.at[pl.ds(0, buf_ref.shape[0]), :], buf_ref, sem).wait()


def _gather_prologue(idx_ref, first_block, n_blocks, src_hbm, buf_ref, sems, n):
    for k in range(GATHER_DEPTH - 1):
        blk = first_block + jnp.minimum(k, n_blocks - 1)
        _start_token_gather(idx_ref, blk * n, src_hbm, buf_ref.at[k], sems.at[k], n)


def _gather_step(i, idx_ref, first_block, n_blocks, src_hbm, buf_ref, sems, n):
    slot = lax.rem(i, GATHER_DEPTH)
    ahead = lax.rem(i + GATHER_DEPTH - 1, GATHER_DEPTH)
    _wait_token_gather(src_hbm, buf_ref.at[slot], sems.at[slot])
    blk = first_block + jnp.minimum(i + GATHER_DEPTH - 1, n_blocks - 1)
    _start_token_gather(idx_ref, blk * n, src_hbm, buf_ref.at[ahead], sems.at[ahead], n)
    return _tokens_to_rows(buf_ref.at[slot], n)


def _gather_epilogue(i, src_hbm, buf_ref, sems):
    for k in range(1, GATHER_DEPTH):
        slot = lax.rem(i + k, GATHER_DEPTH)
        _wait_token_gather(src_hbm, buf_ref.at[slot], sems.at[slot])


def _gathered_rows(idx_ref, first_block, n_blocks, src_hbm, buf_ref, sems, n):
    i = pl.program_id(0)

    @pl.when(i == 0)
    def _():
        _gather_prologue(idx_ref, first_block, n_blocks, src_hbm, buf_ref, sems, n)

    x = _gather_step(i, idx_ref, first_block, n_blocks, src_hbm, buf_ref, sems, n)

    @pl.when(i == n_blocks - 1)
    def _():
        _gather_epilogue(i, src_hbm, buf_ref, sems)

    return x


def _route_meta(x, g, wr_t, cnt_ref):
    xn = _rms(x, g)
    n = x.shape[0]
    xh = xn.astype(BF16)
    xl = (xn - xh.astype(F32)).astype(BF16)
    nt = (((1,), (1,)), ((), ()))
    p = lax.dot_general(wr_t, xh, nt, preferred_element_type=F32)
    q = lax.dot_general(wr_t[:ROUTER_LANES], xl, nt, preferred_element_type=F32)
    logits = p[:ROUTER_LANES] + (p[ROUTER_LANES:] + q)
    lg = [logits[ROUTER_GROUP_ROW + k:ROUTER_GROUP_ROW + k + 1] for k in range(N_EXPERT_GROUPS)]
    mg = functools.reduce(jnp.maximum, lg)
    g_idx = jnp.full((1, n), N_EXPERT_GROUPS - 1, jnp.int32)
    for k in reversed(range(N_EXPERT_GROUPS - 1)):
        g_idx = jnp.where(lg[k] == mg, k, g_idx)
    g_w = 1.0 / functools.reduce(jnp.add, [jnp.exp(l - mg) for l in lg])
    le = []
    for k in range(EXPERTS_PER_GROUP):
        row = lambda grp: logits[ROUTER_EXPERT_ROW + grp * EXPERTS_PER_GROUP + k:
                                 ROUTER_EXPERT_ROW + grp * EXPERTS_PER_GROUP + k + 1]
        v = row(N_EXPERT_GROUPS - 1)
        for grp in reversed(range(N_EXPERT_GROUPS - 1)):
            v = jnp.where(g_idx == grp, row(grp), v)
        le.append(v)
    m1 = functools.reduce(jnp.maximum, le)
    i1 = jnp.full((1, n), EXPERTS_PER_GROUP - 1, jnp.int32)
    for k in reversed(range(EXPERTS_PER_GROUP - 1)):
        i1 = jnp.where(le[k] == m1, k, i1)
    neg = jnp.float32(-jnp.inf)
    le2 = [jnp.where(i1 == k, neg, le[k]) for k in range(EXPERTS_PER_GROUP)]
    m2 = functools.reduce(jnp.maximum, le2)
    i2 = jnp.full((1, n), EXPERTS_PER_GROUP - 1, jnp.int32)
    for k in reversed(range(EXPERTS_PER_GROUP - 1)):
        i2 = jnp.where(le2[k] == m2, k, i2)
    e2 = jnp.exp(m2 - m1)
    w1 = g_w / (1.0 + e2)
    w2 = g_w * e2 / (1.0 + e2)
    lo = jnp.minimum(i1, i2)
    hi = jnp.maximum(i1, i2)
    pair = jnp.where(lo == 0, hi - 1, jnp.where(lo == 1, 6 - hi, 5))
    bucket = g_idx * N_PAIRS + pair
    w_lo = jnp.where(i1 < i2, w1, w2)
    w_hi = jnp.where(i1 < i2, w2, w1)
    w_a = jnp.where(pair == 5, w_hi, w_lo)
    w_b = jnp.where(pair == 5, w_lo, w_hi)
    onehot = lax.broadcasted_iota(jnp.int32, (CNT_ROWS, n), 0) == bucket
    oh = jnp.where(onehot, 1.0, 0.0)
    r_i = lax.broadcasted_iota(jnp.int32, (n, n), 0)
    c_i = lax.broadcasted_iota(jnp.int32, (n, n), 1)
    before = jnp.where(r_i < c_i, 1.0, 0.0).astype(BF16)
    prefix = jnp.dot(oh.astype(BF16), before, preferred_element_type=F32)
    cnt = cnt_ref[...]
    rank = jnp.sum(jnp.where(onehot, prefix + cnt[:, :1], 0.0), axis=0, keepdims=True)
    cnt_ref[...] = cnt + jnp.sum(oh, axis=1, keepdims=True)
    return jnp.concatenate([bucket.astype(F32), w_a, w_b, rank, jnp.zeros((META_ROWS - 4, n), F32)], axis=0)


def _emit_tokens(x, tok_ref, meta_ref, cnt_in_ref, cnt_ref, gf_ref, wr_ref):
    @pl.when(pl.program_id(0) == 0)
    def _():
        cnt_ref[...] = cnt_in_ref[...]
    n = x.shape[0]
    meta = _route_meta(x, gf_ref[...], wr_ref[...], cnt_ref)
    meta_ref[...] = meta
    for c in range(TOKEN_CHUNKS):
        tok_ref[pl.ds(c, n, stride=ROUTED_ROWS), :] = x[:, c * LANES:(c + 1) * LANES]
    tok_ref[pl.ds(ROUTED_META_ROW, n, stride=ROUTED_ROWS), :] = jnp.concatenate(
        [meta, jnp.zeros((LANES - META_ROWS, n), F32)], axis=0).T


def _const_spec(*shape):
    return pl.BlockSpec(shape, lambda i, *_: (0,) * len(shape))


_META_SPEC = pl.BlockSpec((META_ROWS, MIX_ROWS), lambda i, *_: (0, i))


def _tokens_alias(tok_prev, input_index):
    if tok_prev is None:
        return {}, jnp.zeros((SUBLANES, LANES), F32)
    return {input_index: 0}, tok_prev


def _tokens_out(n_total_rows, block_off):
    return (pl.BlockSpec((MIX_ROWS * ROUTED_ROWS, LANES), lambda i, *_: (i + block_off, 0)),
            jax.ShapeDtypeStruct((n_total_rows * ROUTED_ROWS, LANES), F32))


def _time_major_copies(x_hbm, xt_ref, sem, step, slot, n_seq_groups, steps):
    g0 = step * n_seq_groups if n_seq_groups > 1 else 0
    t0 = 0 if n_seq_groups > 1 else step * steps
    return [pltpu.make_async_copy(x_hbm.at[pl.ds(g0, n_seq_groups), b, pl.ds(t0, steps), :],
                                  xt_ref.at[slot, :, :, b, :], sem.at[slot]) for b in range(SUBLANES)]


def _s5_kernel(x_hbm, g_ref, win_ref, bm_ref, are_ref, aim_ref, cm_ref, d_ref, wout_ref, s0re_ref, s0im_ref,
               gf_ref, wr_ref, cnt_in_ref, tok_alias_ref,
               tok_ref, meta_ref, sore_ref, soim_ref, cnt_ref, bu_ref, st_ref, xt_ref, xsem,
               *, n_blocks, n_seq_groups, steps, carry):
    del tok_alias_ref
    i = pl.program_id(0)
    slot = lax.rem(i, 2)
    copies = functools.partial(_time_major_copies, x_hbm, xt_ref, xsem, n_seq_groups=n_seq_groups, steps=steps)

    @pl.when(i == 0)
    def _():
        for c in copies(0, 0):
            c.start()

    @pl.when(i + 1 < n_blocks)
    def _():
        for c in copies(i + 1, 1 - slot):
            c.start()

    for c in copies(i, slot):
        c.wait()
    x = xt_ref[slot].reshape(MIX_ROWS, D_MODEL)
    h = _rms(x, g_ref[...]).astype(BF16)
    u = jnp.dot(h, win_ref[...], preferred_element_type=F32)
    ub = u.astype(BF16)
    for j in range(N_SLABS):
        bu_ref[:, j * SLAB:(j + 1) * SLAB] = jnp.dot(
            ub[:, j * SLAB_IN:(j + 1) * SLAB_IN], bm_ref[j], preferred_element_type=F32)

    if carry:
        @pl.when(pl.program_id(0) == 0)
        def _():
            for j in range(N_SLABS):
                st_ref[:, j * SLAB:j * SLAB + SLAB_HALF] = s0re_ref[0, :, j * SLAB_HALF:(j + 1) * SLAB_HALF]
                st_ref[:, j * SLAB + SLAB_HALF:(j + 1) * SLAB] = s0im_ref[0, :, j * SLAB_HALF:(j + 1) * SLAB_HALF]

    for sg in range(n_seq_groups):
        for j in range(N_SLABS):
            for k in range(SLAB_HALF // REC_COLS):
                re0 = j * SLAB + k * REC_COLS
                im0 = re0 + SLAB_HALF
                io0 = j * SLAB_HALF + k * REC_COLS
                ar = jnp.broadcast_to(are_ref[:, re0:re0 + REC_COLS], (SUBLANES, REC_COLS))
                ai = jnp.broadcast_to(aim_ref[:, re0:re0 + REC_COLS], (SUBLANES, REC_COLS))
                if carry:
                    sr = st_ref[:, re0:re0 + REC_COLS]
                    si = st_ref[:, im0:im0 + REC_COLS]
                else:
                    sr = s0re_ref[sg, :, io0:io0 + REC_COLS]
                    si = s0im_ref[sg, :, io0:io0 + REC_COLS]

                def step(t, c, re0=re0, im0=im0, ar=ar, ai=ai, sg=sg):
                    sr, si = c
                    row = pl.multiple_of((sg * steps + t) * SUBLANES, SUBLANES)
                    br = bu_ref[pl.ds(row, SUBLANES), re0:re0 + REC_COLS]
                    bi = bu_ref[pl.ds(row, SUBLANES), im0:im0 + REC_COLS]
                    nr = ar * sr - ai * si + br
                    ni = ar * si + ai * sr + bi
                    bu_ref[pl.ds(row, SUBLANES), re0:re0 + REC_COLS] = nr
                    bu_ref[pl.ds(row, SUBLANES), im0:im0 + REC_COLS] = ni
                    return nr, ni

                sr, si = lax.fori_loop(0, steps, step, (sr, si), unroll=min(steps, 8))
                if carry:
                    st_ref[:, re0:re0 + REC_COLS] = sr
                    st_ref[:, im0:im0 + REC_COLS] = si
                    sore_ref[0, :, io0:io0 + REC_COLS] = sr
                    soim_ref[0, :, io0:io0 + REC_COLS] = si
                else:
                    sore_ref[sg, :, io0:io0 + REC_COLS] = sr
                    soim_ref[sg, :, io0:io0 + REC_COLS] = si

    ys = [jnp.dot(bu_ref[:, j * SLAB:(j + 1) * SLAB].astype(BF16), cm_ref[j], preferred_element_type=F32)
          for j in range(N_SLABS)]
    y = jnp.concatenate(ys, axis=-1) + d_ref[...] * u
    y = jax.nn.gelu(y).astype(BF16)
    a = jnp.dot(y, wout_ref[...], preferred_element_type=F32)
    xo = x + a[:, :D_MODEL] * jax.nn.sigmoid(a[:, D_MODEL:])
    _emit_tokens(xo, tok_ref, meta_ref, cnt_in_ref, cnt_ref, gf_ref, wr_ref)


def _s5_layer(x_seq, s0, g, win, bmat, a_re, a_im, cmat, dvec, wout, g_ffn, w_router, cnt_in, tok_prev,
              *, n_total_rows, block_off, n_seq_groups, steps, carry):
    assert n_seq_groups * steps * SUBLANES == MIX_ROWS and x_seq.shape[1] == SUBLANES
    assert (n_seq_groups == 1 and x_seq.shape[0] == 1) or steps == x_seq.shape[2]
    n_blocks = x_seq.shape[0] * x_seq.shape[2] // (n_seq_groups * steps)
    s_idx = (lambda i: (0, 0, 0)) if carry else (lambda i: (i, 0, 0))
    kern = functools.partial(_s5_kernel, n_blocks=n_blocks, n_seq_groups=n_seq_groups, steps=steps, carry=carry)
    n_state_groups = 1 if carry else n_blocks * n_seq_groups
    state_spec = pl.BlockSpec((n_seq_groups, SUBLANES, STATE_HALF), s_idx)
    state_shape = jax.ShapeDtypeStruct((n_state_groups, SUBLANES, STATE_HALF), F32)
    tok_spec, tok_shape = _tokens_out(n_total_rows, block_off)
    aliases, tok_prev = _tokens_alias(tok_prev, 14)
    inputs = [x_seq, g, win, bmat, a_re, a_im, cmat, dvec, wout, *s0, g_ffn, w_router, cnt_in, tok_prev]
    return pl.pallas_call(
        kern,
        grid=(n_blocks,),
        in_specs=[
            pl.BlockSpec(memory_space=pl.ANY),
            _const_spec(1, D_MODEL), _const_spec(D_MODEL, SSM_WIDTH), _const_spec(N_SLABS, SLAB_IN, SLAB),
            _const_spec(1, STATE_COLS), _const_spec(1, STATE_COLS), _const_spec(N_SLABS, SLAB, SLAB_IN),
            _const_spec(1, SSM_WIDTH), _const_spec(SSM_WIDTH, 2 * D_MODEL),
            state_spec, state_spec,
            _const_spec(1, D_MODEL), _const_spec(2 * ROUTER_LANES, D_MODEL), _const_spec(CNT_ROWS, LANES),
            pl.BlockSpec(memory_space=pl.ANY),
        ],
        out_specs=[
            tok_spec, _META_SPEC,
            state_spec, state_spec,
            _const_spec(CNT_ROWS, LANES),
        ],
        out_shape=[
            tok_shape, jax.ShapeDtypeStruct((META_ROWS, n_blocks * MIX_ROWS), F32),
            state_shape, state_shape,
            jax.ShapeDtypeStruct((CNT_ROWS, LANES), F32),
        ],
        input_output_aliases=aliases,
        scratch_shapes=[pltpu.VMEM((MIX_ROWS, STATE_COLS), F32), pltpu.VMEM((SUBLANES, STATE_COLS), F32),
                        pltpu.VMEM((2, n_seq_groups, steps, SUBLANES, D_MODEL), F32),
                        pltpu.SemaphoreType.DMA((2,))],
        compiler_params=pltpu.CompilerParams(
            dimension_semantics=("arbitrary",), vmem_limit_bytes=VMEM_LIMIT_BYTES),
        name="s5_carry" if carry else "s5_step",
    )(*inputs)


def _sgu_kernel(pos_ref, src_hbm, g_ref, win_ref, ng_ref, nb_ref, ws_ref, bs_ref, wout_ref,
                gf_ref, wr_ref, cnt_in_ref, tok_alias_ref, *refs, n_blocks, block_off, block_diag, emit_v):
    del tok_alias_ref
    tok_ref, meta_ref, cnt_ref = refs[:3]
    v_ref = refs[3] if emit_v else None
    xbuf, sems = refs[-2:]
    x = _gathered_rows(pos_ref, block_off, n_blocks, src_hbm, xbuf, sems, MIX_ROWS)
    rows = x.shape[0]
    h = _rms(x, g_ref[...]).astype(BF16)
    hh = jax.nn.gelu(jnp.dot(h, win_ref[...], preferred_element_type=F32))
    u = hh[:, :SGU_WIDTH]
    v = hh[:, SGU_WIDTH:]
    vc = v - jnp.mean(v, axis=-1, keepdims=True)
    v = vc * lax.rsqrt(jnp.mean(vc * vc, axis=-1, keepdims=True) + EPS) * ng_ref[...] + nb_ref[...]
    if emit_v:
        v_ref[...] = v
    r = lax.broadcasted_iota(jnp.int32, (CHUNK, CHUNK), 0)
    c = lax.broadcasted_iota(jnp.int32, (CHUNK, CHUNK), 1)
    if block_diag:
        keep = (r // SUBLANES == c // SUBLANES) & (r >= c)
    else:
        keep = r >= c
    vb = v.astype(BF16)
    n_chunks = rows // CHUNK
    rows_tiled = lambda a: a if a.shape[0] == CHUNK else jnp.concatenate([a] * (CHUNK // a.shape[0]), axis=0)
    bias = rows_tiled(bs_ref[...])
    zs = []
    for hd in range(SGU_HEADS):
        wm = jnp.where(keep, rows_tiled(ws_ref[hd]), 0.0).astype(BF16)
        cols = slice(hd * SGU_HEAD_DIM, (hd + 1) * SGU_HEAD_DIM)
        vcat = jnp.concatenate([vb[ck * CHUNK:(ck + 1) * CHUNK, cols] for ck in range(n_chunks)], axis=-1)
        zcat = jnp.dot(wm, vcat, preferred_element_type=F32)
        zs.append([zcat[:, ck * SGU_HEAD_DIM:(ck + 1) * SGU_HEAD_DIM] for ck in range(n_chunks)])
    z = jnp.concatenate(
        [jnp.concatenate([zs[hd][ck] for hd in range(SGU_HEADS)], axis=-1) + bias
         for ck in range(n_chunks)], axis=0)
    o = jnp.dot((u * z).astype(BF16), wout_ref[...], preferred_element_type=F32)
    _emit_tokens(x + o, tok_ref, meta_ref, cnt_in_ref, cnt_ref, gf_ref, wr_ref)


def _sgu_layer(pos, src_tokens, g, win, ng, nb, ws, bs, wout, g_ffn, w_router, cnt_in, tok_prev,
               *, n_total_rows, n_blocks, block_off, block_diag, emit_v):
    tok_spec, tok_shape = _tokens_out(n_total_rows, block_off)
    out_shape = [tok_shape, jax.ShapeDtypeStruct((META_ROWS, n_blocks * MIX_ROWS), F32),
                 jax.ShapeDtypeStruct((CNT_ROWS, LANES), F32)]
    out_specs = [tok_spec, _META_SPEC, _const_spec(CNT_ROWS, LANES)]
    if emit_v:
        out_shape.append(jax.ShapeDtypeStruct((n_blocks * MIX_ROWS, SGU_WIDTH), F32))
        out_specs.append(pl.BlockSpec((MIX_ROWS, SGU_WIDTH), lambda i, *_: (i, 0)))
    aliases, tok_prev = _tokens_alias(tok_prev, 12)
    inputs = [src_tokens, g, win, ng, nb, ws, bs, wout, g_ffn, w_router, cnt_in, tok_prev]
    return pl.pallas_call(
        functools.partial(_sgu_kernel, n_blocks=n_blocks, block_off=block_off, block_diag=block_diag,
                          emit_v=emit_v),
        grid_spec=pltpu.PrefetchScalarGridSpec(
            num_scalar_prefetch=1,
            grid=(n_blocks,),
            in_specs=[pl.BlockSpec(memory_space=pl.ANY), _const_spec(1, D_MODEL),
                      _const_spec(D_MODEL, 2 * SGU_WIDTH), _const_spec(1, SGU_WIDTH), _const_spec(1, SGU_WIDTH),
                      _const_spec(*ws.shape), _const_spec(*bs.shape),
                      _const_spec(SGU_WIDTH, D_MODEL), _const_spec(1, D_MODEL),
                      _const_spec(2 * ROUTER_LANES, D_MODEL), _const_spec(CNT_ROWS, LANES),
                      pl.BlockSpec(memory_space=pl.ANY)],
            out_specs=out_specs,
            scratch_shapes=[pltpu.VMEM((GATHER_DEPTH, MIX_ROWS * TOKEN_CHUNKS, LANES), F32),
                            pltpu.SemaphoreType.DMA((GATHER_DEPTH,))],
        ),
        out_shape=out_shape,
        input_output_aliases=aliases,
        compiler_params=pltpu.CompilerParams(
            dimension_semantics=("arbitrary",), vmem_limit_bytes=VMEM_LIMIT_BYTES),
        name="sgu_block_diag" if block_diag else "sgu_chunked",
    )(pos, *inputs)


def _plan_kernel(cnt_ref, meta_ref, pos_ref, zslot_ref, ea_ref, eb_ref, nt_ref, off_ref):
    n_tiles_max = ea_ref.shape[0]
    tile0 = jnp.int32(0)
    for b in range(N_BUCKETS):
        n_tiles_b = lax.shift_right_logical(cnt_ref[b, 0].astype(jnp.int32) + (MOE_TILE - 1),
                                            MOE_TILE.bit_length() - 1)
        off_ref[b] = tile0 * MOE_TILE
        zslot_ref[b] = jnp.where(n_tiles_b > 0, (tile0 + n_tiles_b - 1) * MOE_TILE, -1)
        grp, pair = divmod(b, N_PAIRS)

        def fill(k, c, tile0=tile0, grp=grp, pair=pair):
            ea_ref[tile0 + k] = grp * EXPERTS_PER_GROUP + PAIR_A[pair]
            eb_ref[tile0 + k] = grp * EXPERTS_PER_GROUP + PAIR_B[pair]
            return c

        lax.fori_loop(0, n_tiles_b, fill, 0)
        tile0 = tile0 + n_tiles_b
    nt_ref[0] = tile0
    last = jnp.maximum(tile0 - 1, 0)
    last_a = ea_ref[last]
    last_b = eb_ref[last]

    def spare(j, c):
        ea_ref[j] = last_a
        eb_ref[j] = last_b
        return c

    lax.fori_loop(tile0, n_tiles_max, spare, 0)
    bucket = meta_ref[META_BUCKET:META_BUCKET + 1, :]
    off = jnp.zeros_like(bucket)
    for b in range(N_BUCKETS):
        off = jnp.where(bucket == b, off_ref[b].astype(F32), off)
    pos_ref[...] = (off + meta_ref[META_RANK:META_RANK + 1, :]).astype(jnp.int32)


def _moe_plan(meta, cnt, n_tiles_max):
    assert MOE_TILE & (MOE_TILE - 1) == 0
    i32 = jnp.int32
    smem = pl.BlockSpec(memory_space=pltpu.SMEM)
    vmem = pl.BlockSpec(memory_space=pltpu.VMEM)
    pos, zslot, e_a, e_b, n_tiles = pl.pallas_call(
        _plan_kernel,
        in_specs=[smem, vmem],
        out_specs=[vmem, smem, smem, smem, smem],
        out_shape=[jax.ShapeDtypeStruct((1, meta.shape[1]), i32), jax.ShapeDtypeStruct((N_BUCKETS,), i32),
                   jax.ShapeDtypeStruct((n_tiles_max,), i32), jax.ShapeDtypeStruct((n_tiles_max,), i32),
                   jax.ShapeDtypeStruct((1,), i32)],
        scratch_shapes=[pltpu.SMEM((N_BUCKETS,), i32)],
        name="moe_plan",
    )(cnt, meta)
    return pos.reshape(-1), zslot, e_a, e_b, n_tiles


def _dispatch_kernel(pos_ref, zslot_ref, tok_ref, xs_ref, zbuf_ref, sem, zsem):
    i = pl.program_id(0)
    n = tok_ref.shape[0] // ROUTED_ROWS
    routed = lambda ref, first, count: ref.at[pl.ds(first * ROUTED_ROWS, count * ROUTED_ROWS), :]

    @pl.when(i == 0)
    def _():
        zbuf_ref[...] = jnp.zeros_like(zbuf_ref)
        zero_copy = lambda b: pltpu.make_async_copy(zbuf_ref, routed(xs_ref, zslot_ref[b], MOE_TILE), zsem)
        for b in range(N_BUCKETS):
            @pl.when(zslot_ref[b] >= 0)
            def _():
                zero_copy(b).start()
        for b in range(N_BUCKETS):
            @pl.when(zslot_ref[b] >= 0)
            def _():
                zero_copy(b).wait()

    def issue(blk, c):
        for k in range(GATHER_UNROLL):
            r = blk * GATHER_UNROLL + k
            pltpu.make_async_copy(routed(tok_ref, r, 1), routed(xs_ref, pos_ref[i * n + r], 1), sem).start(
                priority=k % N_DMA_PRIORITIES)
        return c

    lax.fori_loop(0, n // GATHER_UNROLL, issue, 0)
    pltpu.make_async_copy(tok_ref, routed(xs_ref, 0, n), sem).wait()


def _dispatch(tokens, pos, zslot, n_slots):
    return pl.pallas_call(
        _dispatch_kernel,
        grid_spec=pltpu.PrefetchScalarGridSpec(
            num_scalar_prefetch=2,
            grid=(tokens.shape[0] // (MIX_ROWS * ROUTED_ROWS),),
            in_specs=[pl.BlockSpec((MIX_ROWS * ROUTED_ROWS, LANES), lambda i, pos, zslot: (i, 0))],
            out_specs=pl.BlockSpec(memory_space=pl.ANY),
            scratch_shapes=[pltpu.VMEM((MOE_TILE * ROUTED_ROWS, LANES), F32), pltpu.SemaphoreType.DMA,
                            pltpu.SemaphoreType.DMA],
        ),
        out_shape=jax.ShapeDtypeStruct((n_slots * ROUTED_ROWS, LANES), F32),
        compiler_params=pltpu.CompilerParams(
            dimension_semantics=("arbitrary",), vmem_limit_bytes=VMEM_LIMIT_BYTES),
        name="moe_dispatch",
    )(pos, zslot, tokens)


def _moe_kernel(ea_ref, eb_ref, nt_ref, xs_ref, g_ref, wg_a, wu_a, wd_a, wg_b, wu_b, wd_b, gf_ref,
                ys_ref, wgu_ref, wd_ref, *, final_norm):
    j = pl.program_id(0)

    @pl.when(j < nt_ref[0])
    def _():
        prev = jnp.maximum(j - 1, 0)
        for s, (e_ref, wg, wu, wd) in enumerate(((ea_ref, wg_a, wu_a, wd_a), (eb_ref, wg_b, wu_b, wd_b))):
            @pl.when((j == 0) | (e_ref[j] != e_ref[prev]))
            def _():
                def cast(c, carry, s=s, wg=wg, wu=wu):
                    rows = pl.ds(pl.multiple_of(c * CAST_ROWS, CAST_ROWS), CAST_ROWS)
                    wgu_ref[s, rows, :D_EXPERT] = wg[0, rows, :].astype(BF16)
                    wgu_ref[s, rows, D_EXPERT:] = wu[0, rows, :].astype(BF16)
                    return carry

                lax.fori_loop(0, D_MODEL // CAST_ROWS, cast, 0)
                wd_ref[s] = wd[0].astype(BF16)

        x = jnp.concatenate([xs_ref[pl.ds(c, MOE_TILE, stride=ROUTED_ROWS), :]
                             for c in range(TOKEN_CHUNKS)], axis=-1)
        record = xs_ref[pl.ds(ROUTED_META_ROW, MOE_TILE, stride=ROUTED_ROWS), :]
        xn = _rms(x, g_ref[...]).astype(BF16)
        out = None
        for s, lane in enumerate((META_W_A, META_W_B)):
            gu = jnp.dot(xn, wgu_ref[s], preferred_element_type=F32)
            gate = record[:, lane:lane + 1]
            hcur = (jax.nn.silu(gu[:, :D_EXPERT]) * gu[:, D_EXPERT:] * gate).astype(BF16)
            o = jnp.dot(hcur, wd_ref[s], preferred_element_type=F32)
            out = o if out is None else out + o
        y = x + out
        if final_norm:
            y = _rms(y, gf_ref[...])
        _rows_to_tokens(y, ys_ref)


def _moe_experts(xs, e_a, e_b, n_tiles, g, w_gate, w_up, w_down, g_final, *, layer, final_norm):
    n_tiles_max = e_a.shape[0]
    last = lambda j, nt: jnp.minimum(j, jnp.maximum(nt[0] - 1, 0))
    w_in = lambda which: pl.BlockSpec(
        (None, 1, D_MODEL, D_EXPERT), lambda j, ea, eb, nt: (layer, (ea, eb)[which][j], 0, 0))
    w_out = lambda which: pl.BlockSpec(
        (None, 1, D_EXPERT, D_MODEL), lambda j, ea, eb, nt: (layer, (ea, eb)[which][j], 0, 0))
    return pl.pallas_call(
        functools.partial(_moe_kernel, final_norm=final_norm),
        grid_spec=pltpu.PrefetchScalarGridSpec(
            num_scalar_prefetch=3,
            grid=(n_tiles_max,),
            in_specs=[pl.BlockSpec((MOE_TILE * ROUTED_ROWS, LANES), lambda j, ea, eb, nt: (last(j, nt), 0)),
                      _const_spec(1, D_MODEL), w_in(0), w_in(0), w_out(0), w_in(1), w_in(1), w_out(1),
                      _const_spec(1, D_MODEL)],
            out_specs=pl.BlockSpec((MOE_TILE * TOKEN_CHUNKS, LANES), lambda j, ea, eb, nt: (last(j, nt), 0)),
            scratch_shapes=[pltpu.VMEM((2, D_MODEL, 2 * D_EXPERT), BF16), pltpu.VMEM((2, D_EXPERT, D_MODEL), BF16)],
        ),
        out_shape=jax.ShapeDtypeStruct((n_tiles_max * MOE_TILE * TOKEN_CHUNKS, LANES), F32),
        compiler_params=pltpu.CompilerParams(
            dimension_semantics=("arbitrary",), vmem_limit_bytes=VMEM_LIMIT_BYTES),
        name="moe_experts_final" if final_norm else "moe_experts",
    )(e_a, e_b, n_tiles, xs, g, w_gate, w_up, w_down, w_gate, w_up, w_down, g_final)


def _moe_layer(tokens, meta, cnt, g, w_gate, w_up, w_down, g_final, *, layer, final_norm):
    n_tiles_max = meta.shape[1] // MOE_TILE + N_BUCKETS
    pos, zslot, e_a, e_b, n_tiles = _moe_plan(meta, cnt, n_tiles_max)
    xs = _dispatch(tokens, pos, zslot, n_tiles_max * MOE_TILE)
    ys = _moe_experts(xs, e_a, e_b, n_tiles, g, w_gate, w_up, w_down, g_final, layer=layer,
                      final_norm=final_norm)
    return ys, pos


def _ungather_kernel(pos_ref, src_hbm, *refs, n_blocks, n_first_blocks):
    o_refs, (xbuf, sems) = refs[:-2], refs[-2:]
    i = pl.program_id(0)
    x = _gathered_rows(pos_ref, 0, n_blocks, src_hbm, xbuf, sems, MIX_ROWS)

    @pl.when(i < n_first_blocks)
    def _():
        o_refs[0][...] = x

    @pl.when(i >= n_first_blocks)
    def _():
        o_refs[1][...] = x


def _ungather(ys, pos, n_first_rows):
    n_blocks = pos.shape[0] // MIX_ROWS
    n_first = n_first_rows // MIX_ROWS
    spec = lambda f: pl.BlockSpec((MIX_ROWS, D_MODEL), f)
    return pl.pallas_call(
        functools.partial(_ungather_kernel, n_blocks=n_blocks, n_first_blocks=n_first),
        grid_spec=pltpu.PrefetchScalarGridSpec(
            num_scalar_prefetch=1,
            grid=(n_blocks,),
            in_specs=[pl.BlockSpec(memory_space=pl.ANY)],
            out_specs=[spec(lambda i, pos: (jnp.minimum(i, n_first - 1), 0)),
                       spec(lambda i, pos: (jnp.maximum(i - n_first, 0), 0))],
            scratch_shapes=[pltpu.VMEM((GATHER_DEPTH, MIX_ROWS * TOKEN_CHUNKS, LANES), F32),
                            pltpu.SemaphoreType.DMA((GATHER_DEPTH,))],
        ),
        out_shape=[jax.ShapeDtypeStruct((n_first_rows, D_MODEL), F32),
                   jax.ShapeDtypeStruct((pos.shape[0] - n_first_rows, D_MODEL), F32)],
        compiler_params=pltpu.CompilerParams(
            dimension_semantics=("arbitrary",), vmem_limit_bytes=VMEM_LIMIT_BYTES),
        name="moe_ungather",
    )(pos, ys)


def _router_weights(w_rg, w_re):
    w = jnp.zeros((ROUTER_LANES, D_MODEL), F32)
    w = w.at[ROUTER_GROUP_ROW:ROUTER_GROUP_ROW + N_EXPERT_GROUPS].set(w_rg.astype(F32).T)
    w = w.at[ROUTER_EXPERT_ROW:ROUTER_EXPERT_ROW + N_EXPERTS].set(w_re.astype(F32).T)
    hi = w.astype(BF16)
    lo = (w - hi.astype(F32)).astype(BF16)
    return jnp.concatenate([hi, lo], axis=0)


def kernel(x_prompt, x_sample, state_ssm_re, state_ssm_im, norm_mix, norm_ffn, norm_final, ssm_w_in, ssm_lambda_re, ssm_lambda_im, ssm_log_step, ssm_b_re, ssm_b_im, ssm_c_re, ssm_c_im, ssm_d, ssm_w_out, sgu_w_in, sgu_norm_g, sgu_norm_b, sgu_w_s, sgu_b_s, sgu_w_out, moe_router_group, moe_router_expert, moe_w_gate, moe_w_up, moe_w_down):
    nb, ns, dm = x_prompt.shape
    db, ds, _ = x_sample.shape
    n_p, n_s = nb * ns, db * ds
    n_tok = n_p + n_s
    n_sg = db // SUBLANES
    steps_p = MIX_ROWS // SUBLANES
    sg_per_block = MIX_ROWS // (ds * SUBLANES)
    assert nb == SUBLANES and dm == D_MODEL and ns % steps_p == 0 and n_sg % sg_per_block == 0
    assert n_p % MIX_ROWS == 0 and n_s % MIX_ROWS == 0 and MIX_ROWS % MOE_TILE == 0 and CHUNK % ds == 0
    row = lambda a: a.astype(F32).reshape(1, -1)
    zero_cnt = jnp.zeros((CNT_ROWS, LANES), F32)

    lb_re, lb_im, bb_re, bb_im = _s5_prep(ssm_lambda_re, ssm_lambda_im, ssm_log_step, ssm_b_re, ssm_b_im)
    bmat, cmat, a_re, a_im, dvec = _s5_matrices(lb_re, lb_im, bb_re, bb_im, ssm_c_re, ssm_c_im, ssm_d)
    xp = x_prompt.astype(F32).reshape(1, nb, ns, dm)
    xs = x_sample.astype(F32).reshape(n_sg, SUBLANES, ds, dm)
    zero_state = (jnp.zeros((1, SUBLANES, STATE_HALF), F32),) * 2
    s0 = (state_ssm_re.astype(F32).reshape(n_sg, SUBLANES, STATE_HALF),
          state_ssm_im.astype(F32).reshape(n_sg, SUBLANES, STATE_HALF))
    wr0 = _router_weights(moe_router_group[0], moe_router_expert[0])
    s5 = functools.partial(_s5_layer, g=row(norm_mix[0]), win=ssm_w_in.astype(BF16), bmat=bmat, a_re=a_re,
                           a_im=a_im, cmat=cmat, dvec=dvec, wout=ssm_w_out.astype(BF16),
                           g_ffn=row(norm_ffn[0]), w_router=wr0, n_total_rows=n_tok)
    tok0, meta_p, re_p, im_p, cnt0 = s5(xp, zero_state, cnt_in=zero_cnt, tok_prev=None, block_off=0,
                                  n_seq_groups=1, steps=steps_p, carry=True)
    tok0, meta_s, re_s, im_s, cnt0 = s5(xs, s0, cnt_in=cnt0, tok_prev=tok0, block_off=n_p // MIX_ROWS,
                                  n_seq_groups=sg_per_block, steps=ds, carry=False)
    state = lambda a, n: a.reshape(n, SSM_GROUPS, SSM_STATE)
    ssm_re_p, ssm_im_p, ssm_re_s, ssm_im_s = state(re_p, nb), state(im_p, nb), state(re_s, db), state(im_s, db)
    ys0, pos0 = _moe_layer(tok0, jnp.concatenate([meta_p, meta_s], axis=1), cnt0, row(norm_ffn[0]),
                           moe_w_gate, moe_w_up, moe_w_down, row(norm_final), layer=0, final_norm=False)

    pos0 = jnp.concatenate([
        pos0[:n_p].reshape(ns, nb).T.reshape(-1),
        jnp.transpose(pos0[n_p:].reshape(n_sg, ds, SUBLANES), (0, 2, 1)).reshape(-1)])
    ws_p = sgu_w_s.astype(F32)[:, :CHUNK, :CHUNK]
    bs_p = jnp.repeat(sgu_b_s.astype(F32)[:, :CHUNK].T, SGU_HEAD_DIM, axis=1)
    ws_s = jnp.tile(sgu_w_s.astype(F32)[:, :ds, :ds], (1, 1, CHUNK // ds))
    bs_s = jnp.repeat(sgu_b_s.astype(F32)[:, :ds].T, SGU_HEAD_DIM, axis=1)
    wr1 = _router_weights(moe_router_group[1], moe_router_expert[1])
    sgu = functools.partial(_sgu_layer, pos0, ys0, g=row(norm_mix[1]), win=sgu_w_in.astype(BF16),
                            ng=row(sgu_norm_g), nb=row(sgu_norm_b), wout=sgu_w_out.astype(BF16),
                            g_ffn=row(norm_ffn[1]), w_router=wr1, n_total_rows=n_tok)
    tok1, meta_p, cnt1 = sgu(ws=ws_p, bs=bs_p, cnt_in=zero_cnt, tok_prev=None, n_blocks=n_p // MIX_ROWS,
                             block_off=0, block_diag=False, emit_v=False)
    tok1, meta_s, cnt1, v_s = sgu(ws=ws_s, bs=bs_s, cnt_in=cnt1, tok_prev=tok1, n_blocks=n_s // MIX_ROWS,
                                  block_off=n_p // MIX_ROWS, block_diag=True, emit_v=True)
    ys1, pos1 = _moe_layer(tok1, jnp.concatenate([meta_p, meta_s], axis=1), cnt1, row(norm_ffn[1]),
                           moe_w_gate, moe_w_up, moe_w_down, row(norm_final), layer=1, final_norm=True)
    y_p, y_s = _ungather(ys1, pos1, n_p)
    return (y_p.reshape(nb, ns, dm), y_s.reshape(db, ds, dm), ssm_re_p, ssm_im_p, ssm_re_s, ssm_im_s,
            v_s.reshape(db, ds, SGU_WIDTH))
```

```python
import functools

import jax
import jax.numpy as jnp
from jax import lax
from jax.experimental import pallas as pl
from jax.experimental.pallas import tpu as pltpu

D_MODEL = 1024
SSM_WIDTH = D_MODEL // 2
SSM_GROUP = 16
SSM_GROUPS = SSM_WIDTH // SSM_GROUP
SSM_STATE = 64
SGU_WIDTH = D_MODEL
SGU_HEADS = 8
SGU_HEAD_DIM = SGU_WIDTH // SGU_HEADS
CHUNK = 128
N_EXPERT_GROUPS = 4
EXPERTS_PER_GROUP = 4
N_EXPERTS = N_EXPERT_GROUPS * EXPERTS_PER_GROUP
D_EXPERT = D_MODEL // 4
EPS = 1e-6

SUBLANES = 8
LANES = 128
VMEM_LIMIT_BYTES = 56 * 1024 * 1024
N_DMA_PRIORITIES = 2

SLAB_GROUPS = 16
N_SLABS = SSM_GROUPS // SLAB_GROUPS
SLAB_HALF = SLAB_GROUPS * SSM_STATE
SLAB = 2 * SLAB_HALF
STATE_COLS = N_SLABS * SLAB
STATE_HALF = SSM_GROUPS * SSM_STATE
SLAB_IN = SLAB_GROUPS * SSM_GROUP
REC_COLS = 1024

TOKEN_CHUNKS = D_MODEL // LANES
assert TOKEN_CHUNKS == SUBLANES

PAIR_A = (0, 0, 0, 1, 1, 3)
PAIR_B = (1, 2, 3, 3, 2, 2)
N_PAIRS = len(PAIR_A)
GATHER_DEPTH = 3
GATHER_UNROLL = 16
DISPATCH_DEPTH = 3
CAST_ROWS = 256
ROUTED_ROWS = TOKEN_CHUNKS + 1
ROUTED_META_ROW = TOKEN_CHUNKS
N_BUCKETS = N_EXPERT_GROUPS * N_PAIRS
MOE_TILE = 256
MIX_ROWS = 512
META_ROWS = SUBLANES
META_BUCKET, META_W_A, META_W_B, META_RANK = 0, 1, 2, 3
ROUTER_LANES = LANES
ROUTER_GROUP_ROW, ROUTER_EXPERT_ROW = 0, SUBLANES
CNT_ROWS = 32
BF16 = jnp.bfloat16
F32 = jnp.float32


def _rms(x, g):
    return x * lax.rsqrt(jnp.mean(x * x, axis=-1, keepdims=True) + EPS) * g


def _s5_prep_kernel(lr_ref, li_ref, ls_ref, br_ref, bi_ref, lbr_ref, lbi_ref, bbr_ref, bbi_ref):
    lr = lr_ref[...]
    li = li_ref[...]
    dt = jnp.exp(ls_ref[...])
    mag = jnp.exp(lr * dt)
    ang = li * dt
    lb_re = mag * jnp.cos(ang)
    lb_im = mag * jnp.sin(ang)
    den = lr * lr + li * li
    nr = lb_re - 1.0
    coef_re = (nr * lr + lb_im * li) / den
    coef_im = (lb_im * lr - nr * li) / den
    br = br_ref[...]
    bi = bi_ref[...]
    lbr_ref[...] = lb_re
    lbi_ref[...] = lb_im
    bbr_ref[...] = coef_re * br - coef_im * bi
    bbi_ref[...] = coef_re * bi + coef_im * br


def _s5_prep(lam_re, lam_im, log_step, b_re, b_im):
    rows = SSM_GROUPS * SSM_GROUP
    rep = lambda a: jnp.repeat(a.astype(F32), SSM_GROUP, axis=0)
    lr = rep(lam_re)
    li = rep(lam_im)
    ls = rep(jnp.broadcast_to(log_step.astype(F32)[:, None], (SSM_GROUPS, SSM_STATE)))
    br = jnp.transpose(b_re.astype(F32), (0, 2, 1)).reshape(rows, SSM_STATE)
    bi = jnp.transpose(b_im.astype(F32), (0, 2, 1)).reshape(rows, SSM_STATE)
    sds = jax.ShapeDtypeStruct((rows, SSM_STATE), F32)
    lb_re, lb_im, bb_re, bb_im = pl.pallas_call(
        _s5_prep_kernel, out_shape=(sds, sds, sds, sds), name="s5_prep")(lr, li, ls, br, bi)
    return lb_re, lb_im, bb_re, bb_im


def _to_state_cols(a):
    return a.reshape(a.shape[:-2] + (N_SLABS, SLAB_HALF))


def _s5_matrices(lb_re, lb_im, bb_re, bb_im, c_re, c_im, d):
    eye = jnp.eye(SLAB_GROUPS, dtype=F32)
    def in_side(bb):
        bb = bb.reshape(N_SLABS, SLAB_GROUPS, SSM_GROUP, SSM_STATE)
        m = bb[:, :, :, None, :] * eye[None, :, None, :, None]
        return m.reshape(N_SLABS, SLAB_IN, SLAB_HALF)
    bmat = jnp.concatenate([in_side(bb_re), in_side(bb_im)], axis=-1).astype(BF16)
    def out_side(c):
        c = jnp.transpose(c.astype(F32), (0, 2, 1)).reshape(N_SLABS, SLAB_GROUPS, SSM_STATE, SSM_GROUP)
        m = c[:, :, :, None, :] * eye[None, :, None, :, None]
        return m.reshape(N_SLABS, SLAB_HALF, SLAB_IN)
    cmat = jnp.concatenate([out_side(c_re), -out_side(c_im)], axis=1).astype(BF16)
    lam = lambda lb: _to_state_cols(lb.reshape(SSM_GROUPS, SSM_GROUP, SSM_STATE)[:, 0])
    a_re = jnp.concatenate([lam(lb_re), lam(lb_re)], axis=-1).reshape(1, STATE_COLS)
    a_im = jnp.concatenate([lam(lb_im), lam(lb_im)], axis=-1).reshape(1, STATE_COLS)
    return bmat, cmat, a_re, a_im, d.astype(F32).reshape(1, SSM_WIDTH)


def _rows_to_tokens(x, tok_ref):
    n = x.shape[0]
    for c in range(TOKEN_CHUNKS):
        tok_ref[pl.ds(c, n, stride=TOKEN_CHUNKS), :] = x[:, c * LANES:(c + 1) * LANES]


def _tokens_to_rows(tok_ref, n):
    return jnp.concatenate([tok_ref[pl.ds(c, n, stride=TOKEN_CHUNKS), :] for c in range(TOKEN_CHUNKS)], axis=-1)


def _start_token_gather(idx_ref, base, src_hbm, buf_ref, sem, n):
    def issue(blk, c):
        for k in range(GATHER_UNROLL):
            r = blk * GATHER_UNROLL + k
            tok = idx_ref[base + r]
            pltpu.make_async_copy(
                src_hbm.at[pl.ds(pl.multiple_of(tok * TOKEN_CHUNKS, TOKEN_CHUNKS), TOKEN_CHUNKS), :],
                buf_ref.at[pl.ds(pl.multiple_of(r * TOKEN_CHUNKS, TOKEN_CHUNKS), TOKEN_CHUNKS), :],
                sem).start(priority=k % N_DMA_PRIORITIES)
        return c

    lax.fori_loop(0, n // GATHER_UNROLL, issue, 0)


def _wait_token_gather(src_hbm, buf_ref, sem):
    pltpu.make_async_copy(src_hbm.at[pl.ds(0, buf_ref.shape[0]), :], buf_ref, sem).wait()


def _gather_prologue(idx_ref, first_block, n_blocks, src_hbm, buf_ref, sems, n):
    for k in range(GATHER_DEPTH - 1):
        blk = first_block + jnp.minimum(k, n_blocks - 1)
        _start_token_gather(idx_ref, blk * n, src_hbm, buf_ref.at[k], sems.at[k], n)


def _gather_step(i, idx_ref, first_block, n_blocks, src_hbm, buf_ref, sems, n):
    slot = lax.rem(i, GATHER_DEPTH)
    ahead = lax.rem(i + GATHER_DEPTH - 1, GATHER_DEPTH)
    _wait_token_gather(src_hbm, buf_ref.at[slot], sems.at[slot])
    blk = first_block + jnp.minimum(i + GATHER_DEPTH - 1, n_blocks - 1)
    _start_token_gather(idx_ref, blk * n, src_hbm, buf_ref.at[ahead], sems.at[ahead], n)
    return _tokens_to_rows(buf_ref.at[slot], n)


def _gather_epilogue(i, src_hbm, buf_ref, sems):
    for k in range(1, GATHER_DEPTH):
        slot = lax.rem(i + k, GATHER_DEPTH)
        _wait_token_gather(src_hbm, buf_ref.at[slot], sems.at[slot])


def _gathered_rows(idx_ref, first_block, n_blocks, src_hbm, buf_ref, sems, n):
    i = pl.program_id(0)

    @pl.when(i == 0)
    def _():
        _gather_prologue(idx_ref, first_block, n_blocks, src_hbm, buf_ref, sems, n)

    x = _gather_step(i, idx_ref, first_block, n_blocks, src_hbm, buf_ref, sems, n)

    @pl.when(i == n_blocks - 1)
    def _():
        _gather_epilogue(i, src_hbm, buf_ref, sems)

    return x


def _route_meta(x, g, wr_t, cnt_ref):
    xn = _rms(x, g)
    n = x.shape[0]
    xh = xn.astype(BF16)
    xl = (xn - xh.astype(F32)).astype(BF16)
    nt = (((1,), (1,)), ((), ()))
    p = lax.dot_general(wr_t, xh, nt, preferred_element_type=F32)
    q = lax.dot_general(wr_t[:ROUTER_LANES], xl, nt, preferred_element_type=F32)
    logits = p[:ROUTER_LANES] + (p[ROUTER_LANES:] + q)
    lg = [logits[ROUTER_GROUP_ROW + k:ROUTER_GROUP_ROW + k + 1] for k in range(N_EXPERT_GROUPS)]
    mg = functools.reduce(jnp.maximum, lg)
    g_idx = jnp.full((1, n), N_EXPERT_GROUPS - 1, jnp.int32)
    for k in reversed(range(N_EXPERT_GROUPS - 1)):
        g_idx = jnp.where(lg[k] == mg, k, g_idx)
    g_w = 1.0 / functools.reduce(jnp.add, [jnp.exp(l - mg) for l in lg])
    le = []
    for k in range(EXPERTS_PER_GROUP):
        row = lambda grp: logits[ROUTER_EXPERT_ROW + grp * EXPERTS_PER_GROUP + k:
                                 ROUTER_EXPERT_ROW + grp * EXPERTS_PER_GROUP + k + 1]
        v = row(N_EXPERT_GROUPS - 1)
        for grp in reversed(range(N_EXPERT_GROUPS - 1)):
            v = jnp.where(g_idx == grp, row(grp), v)
        le.append(v)
    m1 = functools.reduce(jnp.maximum, le)
    i1 = jnp.full((1, n), EXPERTS_PER_GROUP - 1, jnp.int32)
    for k in reversed(range(EXPERTS_PER_GROUP - 1)):
        i1 = jnp.where(le[k] == m1, k, i1)
    neg = jnp.float32(-jnp.inf)
    le2 = [jnp.where(i1 == k, neg, le[k]) for k in range(EXPERTS_PER_GROUP)]
    m2 = functools.reduce(jnp.maximum, le2)
    i2 = jnp.full((1, n), EXPERTS_PER_GROUP - 1, jnp.int32)
    for k in reversed(range(EXPERTS_PER_GROUP - 1)):
        i2 = jnp.where(le2[k] == m2, k, i2)
    e2 = jnp.exp(m2 - m1)
    w1 = g_w / (1.0 + e2)
    w2 = g_w * e2 / (1.0 + e2)
    lo = jnp.minimum(i1, i2)
    hi = jnp.maximum(i1, i2)
    pair = jnp.where(lo == 0, hi - 1, jnp.where(lo == 1, 6 - hi, 5))
    bucket = g_idx * N_PAIRS + pair
    w_lo = jnp.where(i1 < i2, w1, w2)
    w_hi = jnp.where(i1 < i2, w2, w1)
    w_a = jnp.where(pair == 5, w_hi, w_lo)
    w_b = jnp.where(pair == 5, w_lo, w_hi)
    onehot = lax.broadcasted_iota(jnp.int32, (CNT_ROWS, n), 0) == bucket
    oh = jnp.where(onehot, 1.0, 0.0)
    r_i = lax.broadcasted_iota(jnp.int32, (n, n), 0)
    c_i = lax.broadcasted_iota(jnp.int32, (n, n), 1)
    before = jnp.where(r_i < c_i, 1.0, 0.0).astype(BF16)
    prefix = jnp.dot(oh.astype(BF16), before, preferred_element_type=F32)
    cnt = cnt_ref[...]
    rank = jnp.sum(jnp.where(onehot, prefix + cnt[:, :1], 0.0), axis=0, keepdims=True)
    cnt_ref[...] = cnt + jnp.sum(oh, axis=1, keepdims=True)
    return jnp.concatenate([bucket.astype(F32), w_a, w_b, rank, jnp.zeros((META_ROWS - 4, n), F32)], axis=0)


def _emit_tokens(x, tok_ref, meta_ref, cnt_in_ref, cnt_ref, gf_ref, wr_ref):
    @pl.when(pl.program_id(0) == 0)
    def _():
        cnt_ref[...] = cnt_in_ref[...]
    n = x.shape[0]
    meta = _route_meta(x, gf_ref[...], wr_ref[...], cnt_ref)
    meta_ref[...] = meta
    for c in range(TOKEN_CHUNKS):
        tok_ref[pl.ds(c, n, stride=ROUTED_ROWS), :] = x[:, c * LANES:(c + 1) * LANES]
    tok_ref[pl.ds(ROUTED_META_ROW, n, stride=ROUTED_ROWS), :] = jnp.concatenate(
        [meta, jnp.zeros((LANES - META_ROWS, n), F32)], axis=0).T


def _const_spec(*shape):
    return pl.BlockSpec(shape, lambda i, *_: (0,) * len(shape))


_META_SPEC = pl.BlockSpec((META_ROWS, MIX_ROWS), lambda i, *_: (0, i))


def _tokens_alias(tok_prev, input_index):
    if tok_prev is None:
        return {}, jnp.zeros((SUBLANES, LANES), F32)
    return {input_index: 0}, tok_prev


def _tokens_out(n_total_rows, block_off):
    return (pl.BlockSpec((MIX_ROWS * ROUTED_ROWS, LANES), lambda i, *_: (i + block_off, 0)),
            jax.ShapeDtypeStruct((n_total_rows * ROUTED_ROWS, LANES), F32))


def _time_major_copies(x_hbm, xt_ref, sem, step, slot, n_seq_groups, steps):
    g0 = step * n_seq_groups if n_seq_groups > 1 else 0
    t0 = 0 if n_seq_groups > 1 else step * steps
    return [pltpu.make_async_copy(x_hbm.at[pl.ds(g0, n_seq_groups), b, pl.ds(t0, steps), :],
                                  xt_ref.at[slot, :, :, b, :], sem.at[slot]) for b in range(SUBLANES)]


def _s5_kernel(x_hbm, g_ref, win_ref, bm_ref, are_ref, aim_ref, cm_ref, d_ref, wout_ref, s0re_ref, s0im_ref,
               gf_ref, wr_ref, cnt_in_ref, tok_alias_ref,
               tok_ref, meta_ref, sore_ref, soim_ref, cnt_ref, bu_ref, st_ref, xt_ref, xsem,
               *, n_blocks, n_seq_groups, steps, carry):
    del tok_alias_ref
    i = pl.program_id(0)
    slot = lax.rem(i, 2)
    copies = functools.partial(_time_major_copies, x_hbm, xt_ref, xsem, n_seq_groups=n_seq_groups, steps=steps)

    @pl.when(i == 0)
    def _():
        for c in copies(0, 0):
            c.start()

    @pl.when(i + 1 < n_blocks)
    def _():
        for c in copies(i + 1, 1 - slot):
            c.start()

    for c in copies(i, slot):
        c.wait()
    x = xt_ref[slot].reshape(MIX_ROWS, D_MODEL)
    h = _rms(x, g_ref[...]).astype(BF16)
    u = jnp.dot(h, win_ref[...], preferred_element_type=F32)
    ub = u.astype(BF16)
    for j in range(N_SLABS):
        bu_ref[:, j * SLAB:(j + 1) * SLAB] = jnp.dot(
            ub[:, j * SLAB_IN:(j + 1) * SLAB_IN], bm_ref[j], preferred_element_type=F32)

    if carry:
        @pl.when(pl.program_id(0) == 0)
        def _():
            for j in range(N_SLABS):
                st_ref[:, j * SLAB:j * SLAB + SLAB_HALF] = s0re_ref[0, :, j * SLAB_HALF:(j + 1) * SLAB_HALF]
                st_ref[:, j * SLAB + SLAB_HALF:(j + 1) * SLAB] = s0im_ref[0, :, j * SLAB_HALF:(j + 1) * SLAB_HALF]

    for sg in range(n_seq_groups):
        for j in range(N_SLABS):
            for k in range(SLAB_HALF // REC_COLS):
                re0 = j * SLAB + k * REC_COLS
                im0 = re0 + SLAB_HALF
                io0 = j * SLAB_HALF + k * REC_COLS
                ar = jnp.broadcast_to(are_ref[:, re0:re0 + REC_COLS], (SUBLANES, REC_COLS))
                ai = jnp.broadcast_to(aim_ref[:, re0:re0 + REC_COLS], (SUBLANES, REC_COLS))
                if carry:
                    sr = st_ref[:, re0:re0 + REC_COLS]
                    si = st_ref[:, im0:im0 + REC_COLS]
                else:
                    sr = s0re_ref[sg, :, io0:io0 + REC_COLS]
                    si = s0im_ref[sg, :, io0:io0 + REC_COLS]

                def step(t, c, re0=re0, im0=im0, ar=ar, ai=ai, sg=sg):
                    sr, si = c
                    row = pl.multiple_of((sg * steps + t) * SUBLANES, SUBLANES)
                    br = bu_ref[pl.ds(row, SUBLANES), re0:re0 + REC_COLS]
                    bi = bu_ref[pl.ds(row, SUBLANES), im0:im0 + REC_COLS]
                    nr = ar * sr - ai * si + br
                    ni = ar * si + ai * sr + bi
                    bu_ref[pl.ds(row, SUBLANES), re0:re0 + REC_COLS] = nr
                    bu_ref[pl.ds(row, SUBLANES), im0:im0 + REC_COLS] = ni
                    return nr, ni

                sr, si = lax.fori_loop(0, steps, step, (sr, si), unroll=True)
                if carry:
                    st_ref[:, re0:re0 + REC_COLS] = sr
                    st_ref[:, im0:im0 + REC_COLS] = si
                    sore_ref[0, :, io0:io0 + REC_COLS] = sr
                    soim_ref[0, :, io0:io0 + REC_COLS] = si
                else:
                    sore_ref[sg, :, io0:io0 + REC_COLS] = sr
                    soim_ref[sg, :, io0:io0 + REC_COLS] = si

    ys = [jnp.dot(bu_ref[:, j * SLAB:(j + 1) * SLAB].astype(BF16), cm_ref[j], preferred_element_type=F32)
          for j in range(N_SLABS)]
    y = jnp.concatenate(ys, axis=-1) + d_ref[...] * u
    y = jax.nn.gelu(y).astype(BF16)
    a = jnp.dot(y, wout_ref[...], preferred_element_type=F32)
    xo = x + a[:, :D_MODEL] * jax.nn.sigmoid(a[:, D_MODEL:])
    _emit_tokens(xo, tok_ref, meta_ref, cnt_in_ref, cnt_ref, gf_ref, wr_ref)


def _s5_layer(x_seq, s0, g, win, bmat, a_re, a_im, cmat, dvec, wout, g_ffn, w_router, cnt_in, tok_prev,
              *, n_total_rows, block_off, n_seq_groups, steps, carry):
    assert n_seq_groups * steps * SUBLANES == MIX_ROWS and x_seq.shape[1] == SUBLANES
    assert (n_seq_groups == 1 and x_seq.shape[0] == 1) or steps == x_seq.shape[2]
    n_blocks = x_seq.shape[0] * x_seq.shape[2] // (n_seq_groups * steps)
    s_idx = (lambda i: (0, 0, 0)) if carry else (lambda i: (i, 0, 0))
    kern = functools.partial(_s5_kernel, n_blocks=n_blocks, n_seq_groups=n_seq_groups, steps=steps, carry=carry)
    n_state_groups = 1 if carry else n_blocks * n_seq_groups
    state_spec = pl.BlockSpec((n_seq_groups, SUBLANES, STATE_HALF), s_idx)
    state_shape = jax.ShapeDtypeStruct((n_state_groups, SUBLANES, STATE_HALF), F32)
    tok_spec, tok_shape = _tokens_out(n_total_rows, block_off)
    aliases, tok_prev = _tokens_alias(tok_prev, 14)
    inputs = [x_seq, g, win, bmat, a_re, a_im, cmat, dvec, wout, *s0, g_ffn, w_router, cnt_in, tok_prev]
    return pl.pallas_call(
        kern,
        grid=(n_blocks,),
        in_specs=[
            pl.BlockSpec(memory_space=pl.ANY),
            _const_spec(1, D_MODEL), _const_spec(D_MODEL, SSM_WIDTH), _const_spec(N_SLABS, SLAB_IN, SLAB),
            _const_spec(1, STATE_COLS), _const_spec(1, STATE_COLS), _const_spec(N_SLABS, SLAB, SLAB_IN),
            _const_spec(1, SSM_WIDTH), _const_spec(SSM_WIDTH, 2 * D_MODEL),
            state_spec, state_spec,
            _const_spec(1, D_MODEL), _const_spec(2 * ROUTER_LANES, D_MODEL), _const_spec(CNT_ROWS, LANES),
            pl.BlockSpec(memory_space=pl.ANY),
        ],
        out_specs=[
            tok_spec, _META_SPEC,
            state_spec, state_spec,
            _const_spec(CNT_ROWS, LANES),
        ],
        out_shape=[
            tok_shape, jax.ShapeDtypeStruct((META_ROWS, n_blocks * MIX_ROWS), F32),
            state_shape, state_shape,
            jax.ShapeDtypeStruct((CNT_ROWS, LANES), F32),
        ],
        input_output_aliases=aliases,
        scratch_shapes=[pltpu.VMEM((MIX_ROWS, STATE_COLS), F32), pltpu.VMEM((SUBLANES, STATE_COLS), F32),
                        pltpu.VMEM((2, n_seq_groups, steps, SUBLANES, D_MODEL), F32),
                        pltpu.SemaphoreType.DMA((2,))],
        compiler_params=pltpu.CompilerParams(
            dimension_semantics=("arbitrary",), vmem_limit_bytes=VMEM_LIMIT_BYTES),
        name="s5_carry" if carry else "s5_step",
    )(*inputs)


def _sgu_kernel(pos_ref, src_hbm, g_ref, win_ref, ng_ref, nb_ref, ws_ref, bs_ref, wout_ref,
                gf_ref, wr_ref, cnt_in_ref, tok_alias_ref, *refs, n_blocks, block_off, block_diag, emit_v):
    del tok_alias_ref
    tok_ref, meta_ref, cnt_ref = refs[:3]
    v_ref = refs[3] if emit_v else None
    xbuf, sems = refs[-2:]
    x = _gathered_rows(pos_ref, block_off, n_blocks, src_hbm, xbuf, sems, MIX_ROWS)
    rows = x.shape[0]
    h = _rms(x, g_ref[...]).astype(BF16)
    hh = jax.nn.gelu(jnp.dot(h, win_ref[...], preferred_element_type=F32))
    u = hh[:, :SGU_WIDTH]
    v = hh[:, SGU_WIDTH:]
    vc = v - jnp.mean(v, axis=-1, keepdims=True)
    v = vc * lax.rsqrt(jnp.mean(vc * vc, axis=-1, keepdims=True) + EPS) * ng_ref[...] + nb_ref[...]
    if emit_v:
        v_ref[...] = v
    r = lax.broadcasted_iota(jnp.int32, (CHUNK, CHUNK), 0)
    c = lax.broadcasted_iota(jnp.int32, (CHUNK, CHUNK), 1)
    if block_diag:
        keep = (r // SUBLANES == c // SUBLANES) & (r >= c)
    else:
        keep = r >= c
    vb = v.astype(BF16)
    n_chunks = rows // CHUNK
    rows_tiled = lambda a: a if a.shape[0] == CHUNK else jnp.concatenate([a] * (CHUNK // a.shape[0]), axis=0)
    bias = rows_tiled(bs_ref[...])
    zs = []
    for hd in range(SGU_HEADS):
        wm = jnp.where(keep, rows_tiled(ws_ref[hd]), 0.0).astype(BF16)
        cols = slice(hd * SGU_HEAD_DIM, (hd + 1) * SGU_HEAD_DIM)
        vcat = jnp.concatenate([vb[ck * CHUNK:(ck + 1) * CHUNK, cols] for ck in range(n_chunks)], axis=-1)
        zcat = jnp.dot(wm, vcat, preferred_element_type=F32)
        zs.append([zcat[:, ck * SGU_HEAD_DIM:(ck + 1) * SGU_HEAD_DIM] for ck in range(n_chunks)])
    z = jnp.concatenate(
        [jnp.concatenate([zs[hd][ck] for hd in range(SGU_HEADS)], axis=-1) + bias
         for ck in range(n_chunks)], axis=0)
    o = jnp.dot((u * z).astype(BF16), wout_ref[...], preferred_element_type=F32)
    _emit_tokens(x + o, tok_ref, meta_ref, cnt_in_ref, cnt_ref, gf_ref, wr_ref)


def _sgu_layer(pos, src_tokens, g, win, ng, nb, ws, bs, wout, g_ffn, w_router, cnt_in, tok_prev,
               *, n_total_rows, n_blocks, block_off, block_diag, emit_v):
    tok_spec, tok_shape = _tokens_out(n_total_rows, block_off)
    out_shape = [tok_shape, jax.ShapeDtypeStruct((META_ROWS, n_blocks * MIX_ROWS), F32),
                 jax.ShapeDtypeStruct((CNT_ROWS, LANES), F32)]
    out_specs = [tok_spec, _META_SPEC, _const_spec(CNT_ROWS, LANES)]
    if emit_v:
        out_shape.append(jax.ShapeDtypeStruct((n_blocks * MIX_ROWS, SGU_WIDTH), F32))
        out_specs.append(pl.BlockSpec((MIX_ROWS, SGU_WIDTH), lambda i, *_: (i, 0)))
    aliases, tok_prev = _tokens_alias(tok_prev, 12)
    inputs = [src_tokens, g, win, ng, nb, ws, bs, wout, g_ffn, w_router, cnt_in, tok_prev]
    return pl.pallas_call(
        functools.partial(_sgu_kernel, n_blocks=n_blocks, block_off=block_off, block_diag=block_diag,
                          emit_v=emit_v),
        grid_spec=pltpu.PrefetchScalarGridSpec(
            num_scalar_prefetch=1,
            grid=(n_blocks,),
            in_specs=[pl.BlockSpec(memory_space=pl.ANY), _const_spec(1, D_MODEL),
                      _const_spec(D_MODEL, 2 * SGU_WIDTH), _const_spec(1, SGU_WIDTH), _const_spec(1, SGU_WIDTH),
                      _const_spec(*ws.shape), _const_spec(*bs.shape),
                      _const_spec(SGU_WIDTH, D_MODEL), _const_spec(1, D_MODEL),
                      _const_spec(2 * ROUTER_LANES, D_MODEL), _const_spec(CNT_ROWS, LANES),
                      pl.BlockSpec(memory_space=pl.ANY)],
            out_specs=out_specs,
            scratch_shapes=[pltpu.VMEM((GATHER_DEPTH, MIX_ROWS * TOKEN_CHUNKS, LANES), F32),
                            pltpu.SemaphoreType.DMA((GATHER_DEPTH,))],
        ),
        out_shape=out_shape,
        input_output_aliases=aliases,
        compiler_params=pltpu.CompilerParams(
            dimension_semantics=("arbitrary",), vmem_limit_bytes=VMEM_LIMIT_BYTES),
        name="sgu_block_diag" if block_diag else "sgu_chunked",
    )(pos, *inputs)


def _plan_kernel(cnt_ref, meta_ref, pos_ref, zslot_ref, ea_ref, eb_ref, nt_ref, off_ref):
    n_tiles_max = ea_ref.shape[0]
    tile0 = jnp.int32(0)
    for b in range(N_BUCKETS):
        n_tiles_b = lax.shift_right_logical(cnt_ref[b, 0].astype(jnp.int32) + (MOE_TILE - 1),
                                            MOE_TILE.bit_length() - 1)
        off_ref[b] = tile0 * MOE_TILE
        zslot_ref[b] = jnp.where(n_tiles_b > 0, (tile0 + n_tiles_b - 1) * MOE_TILE, -1)
        grp, pair = divmod(b, N_PAIRS)

        def fill(k, c, tile0=tile0, grp=grp, pair=pair):
            ea_ref[tile0 + k] = grp * EXPERTS_PER_GROUP + PAIR_A[pair]
            eb_ref[tile0 + k] = grp * EXPERTS_PER_GROUP + PAIR_B[pair]
            return c

        lax.fori_loop(0, n_tiles_b, fill, 0)
        tile0 = tile0 + n_tiles_b
    nt_ref[0] = tile0
    last = jnp.maximum(tile0 - 1, 0)
    last_a = ea_ref[last]
    last_b = eb_ref[last]

    def spare(j, c):
        ea_ref[j] = last_a
        eb_ref[j] = last_b
        return c

    lax.fori_loop(tile0, n_tiles_max, spare, 0)
    bucket = meta_ref[META_BUCKET:META_BUCKET + 1, :]
    off = jnp.zeros_like(bucket)
    for b in range(N_BUCKETS):
        off = jnp.where(bucket == b, off_ref[b].astype(F32), off)
    pos_ref[...] = (off + meta_ref[META_RANK:META_RANK + 1, :]).astype(jnp.int32)


def _moe_plan(meta, cnt, n_tiles_max):
    assert MOE_TILE & (MOE_TILE - 1) == 0
    i32 = jnp.int32
    smem = pl.BlockSpec(memory_space=pltpu.SMEM)
    vmem = pl.BlockSpec(memory_space=pltpu.VMEM)
    pos, zslot, e_a, e_b, n_tiles = pl.pallas_call(
        _plan_kernel,
        in_specs=[smem, vmem],
        out_specs=[vmem, smem, smem, smem, smem],
        out_shape=[jax.ShapeDtypeStruct((1, meta.shape[1]), i32), jax.ShapeDtypeStruct((N_BUCKETS,), i32),
                   jax.ShapeDtypeStruct((n_tiles_max,), i32), jax.ShapeDtypeStruct((n_tiles_max,), i32),
                   jax.ShapeDtypeStruct((1,), i32)],
        scratch_shapes=[pltpu.SMEM((N_BUCKETS,), i32)],
        name="moe_plan",
    )(cnt, meta)
    return pos.reshape(-1), zslot, e_a, e_b, n_tiles


def _dispatch_kernel(pos_ref, zslot_ref, tok_hbm, xs_ref, buf_ref, zbuf_ref, in_sems, out_sems, zsem, *, n_blocks):
    i = pl.program_id(0)
    n = MIX_ROWS
    routed = lambda ref, first, count: ref.at[pl.ds(first * ROUTED_ROWS, count * ROUTED_ROWS), :]
    slot = lax.rem(i, DISPATCH_DEPTH)
    nxt = lax.rem(i + 1, DISPATCH_DEPTH)
    load = lambda blk, s: pltpu.make_async_copy(routed(tok_hbm, blk * n, n), buf_ref.at[s], in_sems.at[s])
    wait_scatter = lambda s: pltpu.make_async_copy(buf_ref.at[s], routed(xs_ref, 0, n), out_sems.at[s]).wait()

    @pl.when(i == 0)
    def _():
        load(0, 0).start()
        zbuf_ref[...] = jnp.zeros_like(zbuf_ref)
        zero_copy = lambda b: pltpu.make_async_copy(zbuf_ref, routed(xs_ref, zslot_ref[b], MOE_TILE), zsem)
        for b in range(N_BUCKETS):
            @pl.when(zslot_ref[b] >= 0)
            def _():
                zero_copy(b).start()
        for b in range(N_BUCKETS):
            @pl.when(zslot_ref[b] >= 0)
            def _():
                zero_copy(b).wait()

    @pl.when(i >= DISPATCH_DEPTH - 1)
    def _():
        wait_scatter(nxt)

    @pl.when(i + 1 < n_blocks)
    def _():
        load(i + 1, nxt).start()

    load(i, slot).wait()

    def issue(blk, c):
        for k in range(GATHER_UNROLL):
            r = blk * GATHER_UNROLL + k
            pltpu.make_async_copy(routed(buf_ref.at[slot], r, 1), routed(xs_ref, pos_ref[i * n + r], 1),
                                  out_sems.at[slot]).start(priority=k % N_DMA_PRIORITIES)
        return c

    lax.fori_loop(0, n // GATHER_UNROLL, issue, 0)

    @pl.when(i == n_blocks - 1)
    def _():
        for back in range(min(DISPATCH_DEPTH - 1, n_blocks)):
            wait_scatter(lax.rem(i + DISPATCH_DEPTH - back, DISPATCH_DEPTH))


def _dispatch(tokens, pos, zslot, n_slots):
    n_blocks = tokens.shape[0] // (MIX_ROWS * ROUTED_ROWS)
    return pl.pallas_call(
        functools.partial(_dispatch_kernel, n_blocks=n_blocks),
        grid_spec=pltpu.PrefetchScalarGridSpec(
            num_scalar_prefetch=2,
            grid=(n_blocks,),
            in_specs=[pl.BlockSpec(memory_space=pl.ANY)],
            out_specs=pl.BlockSpec(memory_space=pl.ANY),
            scratch_shapes=[pltpu.VMEM((DISPATCH_DEPTH, MIX_ROWS * ROUTED_ROWS, LANES), F32),
                            pltpu.VMEM((MOE_TILE * ROUTED_ROWS, LANES), F32),
                            pltpu.SemaphoreType.DMA((DISPATCH_DEPTH,)), pltpu.SemaphoreType.DMA((DISPATCH_DEPTH,)),
                            pltpu.SemaphoreType.DMA],
        ),
        out_shape=jax.ShapeDtypeStruct((n_slots * ROUTED_ROWS, LANES), F32),
        compiler_params=pltpu.CompilerParams(
            dimension_semantics=("arbitrary",), vmem_limit_bytes=VMEM_LIMIT_BYTES),
        name="moe_dispatch",
    )(pos, zslot, tokens)


def _moe_kernel(ea_ref, eb_ref, nt_ref, xs_ref, g_ref, wg_a, wu_a, wd_a, wg_b, wu_b, wd_b, gf_ref,
                ys_ref, wgu_ref, wd_ref, *, final_norm):
    j = pl.program_id(0)

    @pl.when(j < nt_ref[0])
    def _():
        prev = jnp.maximum(j - 1, 0)
        for s, (e_ref, wg, wu, wd) in enumerate(((ea_ref, wg_a, wu_a, wd_a), (eb_ref, wg_b, wu_b, wd_b))):
            @pl.when((j == 0) | (e_ref[j] != e_ref[prev]))
            def _():
                def cast(c, carry, s=s, wg=wg, wu=wu):
                    rows = pl.ds(pl.multiple_of(c * CAST_ROWS, CAST_ROWS), CAST_ROWS)
                    wgu_ref[s, rows, :D_EXPERT] = wg[0, rows, :].astype(BF16)
                    wgu_ref[s, rows, D_EXPERT:] = wu[0, rows, :].astype(BF16)
                    return carry

                lax.fori_loop(0, D_MODEL // CAST_ROWS, cast, 0)
                wd_ref[s] = wd[0].astype(BF16)

        x = jnp.concatenate([xs_ref[pl.ds(c, MOE_TILE, stride=ROUTED_ROWS), :]
                             for c in range(TOKEN_CHUNKS)], axis=-1)
        record = xs_ref[pl.ds(ROUTED_META_ROW, MOE_TILE, stride=ROUTED_ROWS), :]
        xn = _rms(x, g_ref[...]).astype(BF16)
        out = None
        for s, lane in enumerate((META_W_A, META_W_B)):
            gu = jnp.dot(xn, wgu_ref[s], preferred_element_type=F32)
            gate = record[:, lane:lane + 1]
            hcur = (jax.nn.silu(gu[:, :D_EXPERT]) * gu[:, D_EXPERT:] * gate).astype(BF16)
            o = jnp.dot(hcur, wd_ref[s], preferred_element_type=F32)
            out = o if out is None else out + o
        y = x + out
        if final_norm:
            y = _rms(y, gf_ref[...])
        _rows_to_tokens(y, ys_ref)


def _moe_experts(xs, e_a, e_b, n_tiles, g, w_gate, w_up, w_down, g_final, *, layer, final_norm):
    n_tiles_max = e_a.shape[0]
    last = lambda j, nt: jnp.minimum(j, jnp.maximum(nt[0] - 1, 0))
    w_in = lambda which: pl.BlockSpec(
        (None, 1, D_MODEL, D_EXPERT), lambda j, ea, eb, nt: (layer, (ea, eb)[which][j], 0, 0))
    w_out = lambda which: pl.BlockSpec(
        (None, 1, D_EXPERT, D_MODEL), lambda j, ea, eb, nt: (layer, (ea, eb)[which][j], 0, 0))
    return pl.pallas_call(
        functools.partial(_moe_kernel, final_norm=final_norm),
        grid_spec=pltpu.PrefetchScalarGridSpec(
            num_scalar_prefetch=3,
            grid=(n_tiles_max,),
            in_specs=[pl.BlockSpec((MOE_TILE * ROUTED_ROWS, LANES), lambda j, ea, eb, nt: (last(j, nt), 0)),
                      _const_spec(1, D_MODEL), w_in(0), w_in(0), w_out(0), w_in(1), w_in(1), w_out(1),
                      _const_spec(1, D_MODEL)],
            out_specs=pl.BlockSpec((MOE_TILE * TOKEN_CHUNKS, LANES), lambda j, ea, eb, nt: (last(j, nt), 0)),
            scratch_shapes=[pltpu.VMEM((2, D_MODEL, 2 * D_EXPERT), BF16), pltpu.VMEM((2, D_EXPERT, D_MODEL), BF16)],
        ),
        out_shape=jax.ShapeDtypeStruct((n_tiles_max * MOE_TILE * TOKEN_CHUNKS, LANES), F32),
        compiler_params=pltpu.CompilerParams(
            dimension_semantics=("arbitrary",), vmem_limit_bytes=VMEM_LIMIT_BYTES),
        name="moe_experts_final" if final_norm else "moe_experts",
    )(e_a, e_b, n_tiles, xs, g, w_gate, w_up, w_down, w_gate, w_up, w_down, g_final)


def _moe_layer(tokens, meta, cnt, g, w_gate, w_up, w_down, g_final, *, layer, final_norm):
    n_tiles_max = meta.shape[1] // MOE_TILE + N_BUCKETS
    pos, zslot, e_a, e_b, n_tiles = _moe_plan(meta, cnt, n_tiles_max)
    xs = _dispatch(tokens, pos, zslot, n_tiles_max * MOE_TILE)
    ys = _moe_experts(xs, e_a, e_b, n_tiles, g, w_gate, w_up, w_down, g_final, layer=layer,
                      final_norm=final_norm)
    return ys, pos


def _ungather_kernel(pos_ref, src_hbm, *refs, n_blocks, n_first_blocks):
    o_refs, (xbuf, sems) = refs[:-2], refs[-2:]
    i = pl.program_id(0)
    x = _gathered_rows(pos_ref, 0, n_blocks, src_hbm, xbuf, sems, MIX_ROWS)

    @pl.when(i < n_first_blocks)
    def _():
        o_refs[0][...] = x

    @pl.when(i >= n_first_blocks)
    def _():
        o_refs[1][...] = x


def _ungather(ys, pos, n_first_rows):
    n_blocks = pos.shape[0] // MIX_ROWS
    n_first = n_first_rows // MIX_ROWS
    spec = lambda f: pl.BlockSpec((MIX_ROWS, D_MODEL), f)
    return pl.pallas_call(
        functools.partial(_ungather_kernel, n_blocks=n_blocks, n_first_blocks=n_first),
        grid_spec=pltpu.PrefetchScalarGridSpec(
            num_scalar_prefetch=1,
            grid=(n_blocks,),
            in_specs=[pl.BlockSpec(memory_space=pl.ANY)],
            out_specs=[spec(lambda i, pos: (jnp.minimum(i, n_first - 1), 0)),
                       spec(lambda i, pos: (jnp.maximum(i - n_first, 0), 0))],
            scratch_shapes=[pltpu.VMEM((GATHER_DEPTH, MIX_ROWS * TOKEN_CHUNKS, LANES), F32),
                            pltpu.SemaphoreType.DMA((GATHER_DEPTH,))],
        ),
        out_shape=[jax.ShapeDtypeStruct((n_first_rows, D_MODEL), F32),
                   jax.ShapeDtypeStruct((pos.shape[0] - n_first_rows, D_MODEL), F32)],
        compiler_params=pltpu.CompilerParams(
            dimension_semantics=("arbitrary",), vmem_limit_bytes=VMEM_LIMIT_BYTES),
        name="moe_ungather",
    )(pos, ys)


def _router_weights(w_rg, w_re):
    w = jnp.zeros((ROUTER_LANES, D_MODEL), F32)
    w = w.at[ROUTER_GROUP_ROW:ROUTER_GROUP_ROW + N_EXPERT_GROUPS].set(w_rg.astype(F32).T)
    w = w.at[ROUTER_EXPERT_ROW:ROUTER_EXPERT_ROW + N_EXPERTS].set(w_re.astype(F32).T)
    hi = w.astype(BF16)
    lo = (w - hi.astype(F32)).astype(BF16)
    return jnp.concatenate([hi, lo], axis=0)


def kernel(x_prompt, x_sample, state_ssm_re, state_ssm_im, norm_mix, norm_ffn, norm_final, ssm_w_in, ssm_lambda_re, ssm_lambda_im, ssm_log_step, ssm_b_re, ssm_b_im, ssm_c_re, ssm_c_im, ssm_d, ssm_w_out, sgu_w_in, sgu_norm_g, sgu_norm_b, sgu_w_s, sgu_b_s, sgu_w_out, moe_router_group, moe_router_expert, moe_w_gate, moe_w_up, moe_w_down):
    nb, ns, dm = x_prompt.shape
    db, ds, _ = x_sample.shape
    n_p, n_s = nb * ns, db * ds
    n_tok = n_p + n_s
    n_sg = db // SUBLANES
    steps_p = MIX_ROWS // SUBLANES
    sg_per_block = MIX_ROWS // (ds * SUBLANES)
    assert nb == SUBLANES and dm == D_MODEL and ns % steps_p == 0 and n_sg % sg_per_block == 0
    assert n_p % MIX_ROWS == 0 and n_s % MIX_ROWS == 0 and MIX_ROWS % MOE_TILE == 0 and CHUNK % ds == 0
    row = lambda a: a.astype(F32).reshape(1, -1)
    zero_cnt = jnp.zeros((CNT_ROWS, LANES), F32)

    lb_re, lb_im, bb_re, bb_im = _s5_prep(ssm_lambda_re, ssm_lambda_im, ssm_log_step, ssm_b_re, ssm_b_im)
    bmat, cmat, a_re, a_im, dvec = _s5_matrices(lb_re, lb_im, bb_re, bb_im, ssm_c_re, ssm_c_im, ssm_d)
    xp = x_prompt.astype(F32).reshape(1, nb, ns, dm)
    xs = x_sample.astype(F32).reshape(n_sg, SUBLANES, ds, dm)
    zero_state = (jnp.zeros((1, SUBLANES, STATE_HALF), F32),) * 2
    s0 = (state_ssm_re.astype(F32).reshape(n_sg, SUBLANES, STATE_HALF),
          state_ssm_im.astype(F32).reshape(n_sg, SUBLANES, STATE_HALF))
    wr0 = _router_weights(moe_router_group[0], moe_router_expert[0])
    s5 = functools.partial(_s5_layer, g=row(norm_mix[0]), win=ssm_w_in.astype(BF16), bmat=bmat, a_re=a_re,
                           a_im=a_im, cmat=cmat, dvec=dvec, wout=ssm_w_out.astype(BF16),
                           g_ffn=row(norm_ffn[0]), w_router=wr0, n_total_rows=n_tok)
    tok0, meta_p, re_p, im_p, cnt0 = s5(xp, zero_state, cnt_in=zero_cnt, tok_prev=None, block_off=0,
                                  n_seq_groups=1, steps=steps_p, carry=True)
    tok0, meta_s, re_s, im_s, cnt0 = s5(xs, s0, cnt_in=cnt0, tok_prev=tok0, block_off=n_p // MIX_ROWS,
                                  n_seq_groups=sg_per_block, steps=ds, carry=False)
    state = lambda a, n: a.reshape(n, SSM_GROUPS, SSM_STATE)
    ssm_re_p, ssm_im_p, ssm_re_s, ssm_im_s = state(re_p, nb), state(im_p, nb), state(re_s, db), state(im_s, db)
    ys0, pos0 = _moe_layer(tok0, jnp.concatenate([meta_p, meta_s], axis=1), cnt0, row(norm_ffn[0]),
                           moe_w_gate, moe_w_up, moe_w_down, row(norm_final), layer=0, final_norm=False)

    pos0 = jnp.concatenate([
        pos0[:n_p].reshape(ns, nb).T.reshape(-1),
        jnp.transpose(pos0[n_p:].reshape(n_sg, ds, SUBLANES), (0, 2, 1)).reshape(-1)])
    ws_p = sgu_w_s.astype(F32)[:, :CHUNK, :CHUNK]
    bs_p = jnp.repeat(sgu_b_s.astype(F32)[:, :CHUNK].T, SGU_HEAD_DIM, axis=1)
    ws_s = jnp.tile(sgu_w_s.astype(F32)[:, :ds, :ds], (1, 1, CHUNK // ds))
    bs_s = jnp.repeat(sgu_b_s.astype(F32)[:, :ds].T, SGU_HEAD_DIM, axis=1)
    wr1 = _router_weights(moe_router_group[1], moe_router_expert[1])
    sgu = functools.partial(_sgu_layer, pos0, ys0, g=row(norm_mix[1]), win=sgu_w_in.astype(BF16),
                            ng=row(sgu_norm_g), nb=row(sgu_norm_b), wout=sgu_w_out.astype(BF16),
                            g_ffn=row(norm_ffn[1]), w_router=wr1, n_total_rows=n_tok)
    tok1, meta_p, cnt1 = sgu(ws=ws_p, bs=bs_p, cnt_in=zero_cnt, tok_prev=None, n_blocks=n_p // MIX_ROWS,
                             block_off=0, block_diag=False, emit_v=False)
    tok1, meta_s, cnt1, v_s = sgu(ws=ws_s, bs=bs_s, cnt_in=cnt1, tok_prev=tok1, n_blocks=n_s // MIX_ROWS,
                                  block_off=n_p // MIX_ROWS, block_diag=True, emit_v=True)
    ys1, pos1 = _moe_layer(tok1, jnp.concatenate([meta_p, meta_s], axis=1), cnt1, row(norm_ffn[1]),
                           moe_w_gate, moe_w_up, moe_w_down, row(norm_final), layer=1, final_norm=True)
    y_p, y_s = _ungather(ys1, pos1, n_p)
    return (y_p.reshape(nb, ns, dm), y_s.reshape(db, ds, dm), ssm_re_p, ssm_im_p, ssm_re_s, ssm_im_s,
            v_s.reshape(db, ds, SGU_WIDTH))
```
